```python
import math
import jax, jax.numpy as jnp
from jax import lax
import numpy as np

D_MODEL = 1024
BATCH = 32
SEQ = 2048
DEPTH = 1
DEC_BATCH = 32
DEC_SEQ = 16
PAST_LEN = 1024

CHUNK = 64
HEAD_DIM = 64
H_A = 8
H_B = 8
D_A = H_A * 2 * HEAD_DIM
D_B = H_B * HEAD_DIM
BAND_CHUNKS = 8
REL_CLIP = 128
Q_BLOCK = 128
SPLITS = [D_A, 2 * D_A, 3 * D_A, 3 * D_A + D_B, 3 * D_A + 2 * D_B, 3 * D_A + 3 * D_B,
          3 * D_A + 3 * D_B + D_MODEL]
N_IN = 3 * D_A + 3 * D_B + 2 * D_MODEL
N_EXPERTS = 64
TOP_K = 8
N_GROUP = 8
TOPK_GROUP = 4
D_EXPERT = 256
ROUTED_SCALE = 2.5
EXPERT_BLOCK = 256
ALPHA = (2 * DEPTH) ** 0.25
BETA = (8 * DEPTH) ** -0.25
LN_EPS = 1e-5
NEG = -1e30

kernel_name = "streaming_diffattn_chunkband_moe_step"


def layer_norm(x):
    xf = x.astype(jnp.float32)
    mu = jnp.mean(xf, -1, keepdims=True)
    var = jnp.mean(jnp.square(xf - mu), -1, keepdims=True)
    return ((xf - mu) * lax.rsqrt(var + LN_EPS)).astype(x.dtype)


def rms_norm(x, g):
    xf = x.astype(jnp.float32)
    return (xf * lax.rsqrt(jnp.mean(xf * xf, -1, keepdims=True) + LN_EPS)).astype(x.dtype) * g


def modulate(x, shift, scale):
    return layer_norm(x) * (1.0 + scale) + shift


def post_norm(x, y, g, b):
    return layer_norm(ALPHA * x + y) * g + b


def conditioning(c, w_ada, b_ada):
    mod = jax.nn.silu(c) @ w_ada + b_ada
    return [m[:, None, :] for m in jnp.split(mod, 6, axis=-1)]


def alibi_slopes(n):
    return jnp.asarray(2.0 ** (-8.0 * np.arange(1, n + 1) / n), dtype=jnp.float32)


def diff_attention(q, k, v, tq, ts, lam, subln_g, lam_init):
    s = jnp.einsum('bqhjd,bshjd->bhjqs', q, k).astype(jnp.float32) * HEAD_DIM ** -0.5
    dist = jnp.abs(tq[:, None] - ts[None, :]).astype(jnp.float32)
    bias = -alibi_slopes(H_A)[:, None, None] * dist[None]
    mask = (ts[None, :] // CHUNK) <= (tq[:, None] // CHUNK)
    s = jnp.where(mask, s + bias[:, None], NEG)
    p = jax.nn.softmax(s, axis=-1)
    a = p[:, :, 0] - lam * p[:, :, 1]
    o = jnp.einsum('bhqs,bshe->bqhe', a.astype(v.dtype), v)
    o = rms_norm(o, subln_g) * (1.0 - lam_init)
    return o.reshape(o.shape[0], o.shape[1], D_A)


def diff_attention_prompt(q, k, v, lam, subln_g, lam_init):
    B, S = q.shape[:2]
    nqb = S // Q_BLOCK
    qb = q.reshape(B, nqb, Q_BLOCK, H_A, 2, HEAD_DIM).transpose(1, 0, 2, 3, 4, 5)
    ts = jnp.arange(S)

    def one_block(args):
        qi, b = args
        tq = b * Q_BLOCK + jnp.arange(Q_BLOCK)
        return diff_attention(qi, k, v, tq, ts, lam, subln_g, lam_init)

    o = lax.map(one_block, (qb, jnp.arange(nqb)))
    return o.transpose(1, 0, 2, 3).reshape(B, S, D_A)


def rel_bias_lookup(table, rel):
    return jnp.take(table, jnp.clip(rel, -REL_CLIP, REL_CLIP) + REL_CLIP, axis=1)


def band_attention_prompt(q, k, v, rel_table):
    B, S = q.shape[:2]
    nc = S // CHUNK
    nb = BAND_CHUNKS + 1
    pad = ((0, 0), (BAND_CHUNKS, 0), (0, 0), (0, 0), (0, 0))
    qc = q.reshape(B, nc, CHUNK, H_B, HEAD_DIM)
    kc = jnp.pad(k.reshape(B, nc, CHUNK, H_B, HEAD_DIM), pad)
    vc = jnp.pad(v.reshape(B, nc, CHUNK, H_B, HEAD_DIM), pad)
    s = jnp.concatenate([jnp.einsum('bncha,bnmha->bnhcm', qc, kc[:, j:j + nc]) for j in range(nb)],
                        axis=-1).astype(jnp.float32) * HEAD_DIM ** -0.5
    c = np.arange(CHUNK)
    rel = ((BAND_CHUNKS - np.arange(nb))[:, None, None] * CHUNK + c[None, :, None] - c[None, None, :])
    rel = rel.transpose(1, 0, 2).reshape(CHUNK, nb * CHUNK)
    bias = rel_bias_lookup(rel_table, jnp.asarray(rel, dtype=jnp.int32)).astype(jnp.float32)
    valid = (jnp.arange(nc)[:, None] - BAND_CHUNKS + jnp.arange(nb)[None, :]) >= 0
    valid = jnp.repeat(valid, CHUNK, axis=1)
    s = jnp.where(valid[:, None, None, :], s + bias, NEG)
    p = jax.nn.softmax(s, axis=-1).astype(v.dtype).reshape(B, nc, H_B, CHUNK, nb, CHUNK)
    o = jnp.einsum('bnhcm,bnmha->bncha', p[..., 0, :], vc[:, 0:nc])
    for j in range(1, nb):
        o = o + jnp.einsum('bnhcm,bnmha->bncha', p[..., j, :], vc[:, j:j + nc])
    return o.reshape(B, S, D_B)


def band_attention_sample(q, k, v, tq, ts, rel_table):
    s = jnp.einsum('bqha,bsha->bhqs', q, k).astype(jnp.float32) * HEAD_DIM ** -0.5
    bias = rel_bias_lookup(rel_table, tq[:, None] - ts[None, :]).astype(jnp.float32)
    cq, cs = tq // CHUNK, ts // CHUNK
    mask = (cs[None, :] <= cq[:, None]) & (cs[None, :] >= cq[:, None] - BAND_CHUNKS)
    s = jnp.where(mask, s + bias, NEG)
    p = jax.nn.softmax(s, axis=-1)
    o = jnp.einsum('bhqs,bsha->bqha', p.astype(v.dtype), v)
    return o.reshape(o.shape[0], o.shape[1], D_B)


def token_sublayer(x, shift, scale, gate, w_in, b_in, p_a, p_b, w_out, ln_g, ln_b, attend):
    u = modulate(x, shift, scale)
    B, S, _ = u.shape
    z = u @ w_in + b_in
    qa, ka, va, qb, kb, vb, ga, gb = jnp.split(z, SPLITS, axis=-1)
    qa = qa.reshape(B, S, H_A, 2, HEAD_DIM)
    ka = ka.reshape(B, S, H_A, 2, HEAD_DIM)
    va = va.reshape(B, S, H_A, 2 * HEAD_DIM)
    qb = qb.reshape(B, S, H_B, HEAD_DIM)
    kb = kb.reshape(B, S, H_B, HEAD_DIM)
    vb = vb.reshape(B, S, H_B, HEAD_DIM)
    oa, ob = attend(qa, ka, va, qb, kb, vb)
    m = jax.nn.sigmoid(ga) * (oa @ p_a) + jax.nn.sigmoid(gb) * (ob @ p_b)
    return post_norm(x, gate * (m @ w_out), ln_g, ln_b), (ka, va, kb, vb)


def route(h, w_router, e_bias):
    T = h.shape[0]
    scores = jax.nn.sigmoid((h @ w_router).astype(jnp.float32))
    biased = scores + e_bias.astype(jnp.float32)
    grp = jnp.sum(lax.top_k(biased.reshape(T, N_GROUP, N_EXPERTS // N_GROUP), 2)[0], -1)
    _, gidx = lax.top_k(grp, TOPK_GROUP)
    gmask = jnp.any(gidx[:, :, None] == jnp.arange(N_GROUP)[None, None, :], axis=1)
    emask = jnp.repeat(gmask, N_EXPERTS // N_GROUP, axis=1)
    _, idx = lax.top_k(jnp.where(emask, biased, -jnp.inf), TOP_K)
    w = jnp.take_along_axis(scores, idx, axis=1)
    w = w / jnp.sum(w, -1, keepdims=True) * ROUTED_SCALE
    return idx, w


def routed_experts(h, idx, wts, w1, w3, w2):
    T, D = h.shape
    A = T * TOP_K
    nb = -(-A // EXPERT_BLOCK) + N_EXPERTS
    flat_e = idx.reshape(A)
    flat_w = wts.reshape(A)
    order = jnp.argsort(flat_e)
    e_sorted = flat_e[order]
    counts = jnp.bincount(flat_e, length=N_EXPERTS)
    padded = (counts + EXPERT_BLOCK - 1) // EXPERT_BLOCK * EXPERT_BLOCK
    pad_end = jnp.cumsum(padded)
    pad_start = pad_end - padded
    start = jnp.cumsum(counts) - counts
    dest = pad_start[e_sorted] + jnp.arange(A) - start[e_sorted]
    n_slots = nb * EXPERT_BLOCK
    slot_tok = jnp.full((n_slots,), T, jnp.int32).at[dest].set((order // TOP_K).astype(jnp.int32))
    slot_w = jnp.zeros((n_slots,), h.dtype).at[dest].set(flat_w[order].astype(h.dtype))
    block_e = jnp.minimum(jnp.searchsorted(pad_end, jnp.arange(nb) * EXPERT_BLOCK, side='right'),
                          N_EXPERTS - 1)
    h_pad = jnp.concatenate([h, jnp.zeros((1, D), h.dtype)], axis=0)

    def expert_group(args):
        tok, wt, e = args
        xb = h_pad[tok]
        hid = jax.nn.silu(xb @ w1[e]) * (xb @ w3[e])
        return (hid @ w2[e]) * wt[:, None]

    yb = lax.map(expert_group, (slot_tok.reshape(nb, EXPERT_BLOCK),
                                slot_w.reshape(nb, EXPERT_BLOCK), block_e))
    out = jnp.zeros((T + 1, D), h.dtype).at[slot_tok].add(yb.reshape(n_slots, D))
    return out[:T]


def channel_sublayer(x, shift, scale, gate, w_router, e_bias, w1, w3, w2, ws1, ws3, ws2, ln_g, ln_b):
    u = modulate(x, shift, scale)
    B, S, D = u.shape
    h = u.reshape(B * S, D)
    idx, wts = route(h, w_router, e_bias)
    y = (jax.nn.silu(h @ ws1) * (h @ ws3)) @ ws2 + routed_experts(h, idx, wts, w1, w3, w2)
    return post_norm(x, gate * y.reshape(B, S, D), ln_g, ln_b)


def setup_inputs(seed: int = 0) -> dict:
    key = jax.random.key(seed)
    ks = jax.random.split(key, 40)
    f32 = jnp.float32

    def nrm(k, shape, scale):
        return jax.random.normal(k, shape, f32) * scale

    lb = min(BAND_CHUNKS * CHUNK, PAST_LEN)
    col_scale = jnp.concatenate([jnp.ones((2 * D_A,), f32), jnp.full((D_A,), BETA, f32),
                                 jnp.ones((2 * D_B,), f32), jnp.full((D_B,), BETA, f32),
                                 jnp.ones((2 * D_MODEL,), f32)])
    L, D, E, F = DEPTH, D_MODEL, N_EXPERTS, D_EXPERT
    return {
        "x_prompt": nrm(ks[0], (BATCH, SEQ, D), 1.0),
        "x_sample": nrm(ks[1], (DEC_BATCH, DEC_SEQ, D), 1.0),
        "cache_k_diff": nrm(ks[2], (L, DEC_BATCH, PAST_LEN, H_A, 2, HEAD_DIM), 1.0),
        "cache_v_diff": nrm(ks[3], (L, DEC_BATCH, PAST_LEN, H_A, 2 * HEAD_DIM), BETA),
        "cache_k_band": nrm(ks[4], (L, DEC_BATCH, lb, H_B, HEAD_DIM), 1.0),
        "cache_v_band": nrm(ks[5], (L, DEC_BATCH, lb, H_B, HEAD_DIM), BETA),
        "c_prompt": nrm(ks[6], (BATCH, D), 1.0),
        "c_sample": nrm(ks[7], (DEC_BATCH, D), 1.0),
        "w_ada": nrm(ks[8], (L, D, 6 * D), D ** -0.5),
        "b_ada": nrm(ks[9], (L, 6 * D), 0.02),
        "w_in": nrm(ks[10], (L, D, N_IN), D ** -0.5) * col_scale,
        "b_in": nrm(ks[11], (L, N_IN), 0.02),
        "lambda_q1": nrm(ks[12], (L, HEAD_DIM), 0.1),
        "lambda_k1": nrm(ks[13], (L, HEAD_DIM), 0.1),
        "lambda_q2": nrm(ks[14], (L, HEAD_DIM), 0.1),
        "lambda_k2": nrm(ks[15], (L, HEAD_DIM), 0.1),
        "subln_g": 1.0 + nrm(ks[16], (L, 2 * HEAD_DIM), 0.02),
        "rel_bias": nrm(ks[17], (L, H_B, 2 * REL_CLIP + 1), 0.1),
        "p_a": nrm(ks[18], (L, D_A, D), BETA * D_A ** -0.5),
        "p_b": nrm(ks[19], (L, D_B, D), BETA * D_B ** -0.5),
        "w_out": nrm(ks[20], (L, D, D), BETA * D ** -0.5),
        "ln1_g": 1.0 + nrm(ks[21], (L, D), 0.02),
        "ln1_b": nrm(ks[22], (L, D), 0.02),
        "w_router": nrm(ks[23], (L, D, E), D ** -0.5),
        "e_bias": nrm(ks[24], (L, E), 0.01),
        "w1": nrm(ks[25], (L, E, D, F), BETA * D ** -0.5),
        "w3": nrm(ks[26], (L, E, D, F), BETA * D ** -0.5),
        "w2": nrm(ks[27], (L, E, F, D), BETA * F ** -0.5),
        "ws1": nrm(ks[28], (L, D, F), BETA * D ** -0.5),
        "ws3": nrm(ks[29], (L, D, F), BETA * D ** -0.5),
        "ws2": nrm(ks[30], (L, F, D), BETA * F ** -0.5),
        "ln2_g": 1.0 + nrm(ks[31], (L, D), 0.02),
        "ln2_b": nrm(ks[32], (L, D), 0.02),
    }


def reference(x_prompt, x_sample, cache_k_diff, cache_v_diff, cache_k_band, cache_v_band,
              c_prompt, c_sample, w_ada, b_ada, w_in, b_in, lambda_q1, lambda_k1, lambda_q2,
              lambda_k2, subln_g, rel_bias, p_a, p_b, w_out, ln1_g, ln1_b, w_router, e_bias,
              w1, w3, w2, ws1, ws3, ws2, ln2_g, ln2_b):
    f32 = jnp.float32
    yp, ys = x_prompt, x_sample
    kdp, vdp, kbp, vbp, kds, vds, kbs, vbs = [], [], [], [], [], [], [], []
    for l in range(DEPTH):
        lam_init = 0.8 - 0.6 * math.exp(-0.3 * l)
        lam = (jnp.exp(jnp.sum(lambda_q1[l].astype(f32) * lambda_k1[l].astype(f32)))
               - jnp.exp(jnp.sum(lambda_q2[l].astype(f32) * lambda_k2[l].astype(f32))) + lam_init)

        def attend_prompt(qa, ka, va, qb, kb, vb):
            return (diff_attention_prompt(qa, ka, va, lam, subln_g[l], lam_init),
                    band_attention_prompt(qb, kb, vb, rel_bias[l]))

        sh1, sc1, g1, sh2, sc2, g2 = conditioning(c_prompt, w_ada[l], b_ada[l])
        yp, (ka, va, kb, vb) = token_sublayer(yp, sh1, sc1, g1, w_in[l], b_in[l], p_a[l], p_b[l],
                                              w_out[l], ln1_g[l], ln1_b[l], attend_prompt)
        yp = channel_sublayer(yp, sh2, sc2, g2, w_router[l], e_bias[l], w1[l], w3[l], w2[l],
                              ws1[l], ws3[l], ws2[l], ln2_g[l], ln2_b[l])
        lbp = min(BAND_CHUNKS * CHUNK, ka.shape[1])
        kdp.append(ka)
        vdp.append(va)
        kbp.append(kb[:, -lbp:])
        vbp.append(vb[:, -lbp:])

        def attend_sample(qa, ka, va, qb, kb, vb):
            past = cache_k_diff.shape[2]
            n = qa.shape[1]
            tq = past + jnp.arange(n)
            oa = diff_attention(qa, jnp.concatenate([cache_k_diff[l], ka], axis=1),
                                jnp.concatenate([cache_v_diff[l], va], axis=1),
                                tq, jnp.arange(past + n), lam, subln_g[l], lam_init)
            lb = cache_k_band.shape[2]
            ob = band_attention_sample(qb, jnp.concatenate([cache_k_band[l], kb], axis=1),
                                       jnp.concatenate([cache_v_band[l], vb], axis=1),
                                       tq, jnp.arange(past - lb, past + n), rel_bias[l])
            return oa, ob

        sh1, sc1, g1, sh2, sc2, g2 = conditioning(c_sample, w_ada[l], b_ada[l])
        ys, (ka, va, kb, vb) = token_sublayer(ys, sh1, sc1, g1, w_in[l], b_in[l], p_a[l], p_b[l],
                                              w_out[l], ln1_g[l], ln1_b[l], attend_sample)
        ys = channel_sublayer(ys, sh2, sc2, g2, w_router[l], e_bias[l], w1[l], w3[l], w2[l],
                              ws1[l], ws3[l], ws2[l], ln2_g[l], ln2_b[l])
        kds.append(ka)
        vds.append(va)
        kbs.append(kb)
        vbs.append(vb)

    return (yp, ys, jnp.stack(kdp), jnp.stack(vdp), jnp.stack(kbp), jnp.stack(vbp),
            jnp.stack(kds), jnp.stack(vds), jnp.stack(kbs), jnp.stack(vbs))
```

```python
import functools
import math

import jax
import jax.numpy as jnp
import numpy as np
from jax import lax
from jax.experimental import pallas as pl
from jax.experimental.pallas import tpu as pltpu

F32 = jnp.float32
BF16 = jnp.bfloat16

D_MODEL = 1024
CHUNK = 64
HEAD_DIM = 64
H_A = 8
H_B = 8
D_A = H_A * 2 * HEAD_DIM
D_B = H_B * HEAD_DIM
BAND_CHUNKS = 8
REL_CLIP = 128
N_IN = 3 * D_A + 3 * D_B + 2 * D_MODEL
N_EXPERTS = 64
TOP_K = 8
N_GROUP = 8
TOPK_GROUP = 4
GROUP_SIZE = N_EXPERTS // N_GROUP
D_EXPERT = 256
ROUTED_SCALE = 2.5
EXPERT_BLOCK = 256
LN_EPS = 1e-5
NEG = -1e30

VMEM_LIMIT = 56 * 1024 * 1024


def _cparams(sem):
    return pltpu.CompilerParams(dimension_semantics=sem, vmem_limit_bytes=VMEM_LIMIT)


def _ln(x):
    mu = jnp.mean(x, axis=-1, keepdims=True)
    xc = x - mu
    var = jnp.mean(xc * xc, axis=-1, keepdims=True)
    return xc * lax.rsqrt(var + LN_EPS)


def _rows(ref):
    v = ref[...]
    return v.reshape(v.shape[-2], v.shape[-1])


def _split_bf16(x):
    hi = x.astype(BF16)
    lo = (x - hi.astype(F32)).astype(BF16)
    return hi, lo


def _dot(a, b):
    return jnp.dot(a, b, preferred_element_type=F32)


def _dot_nt(a, b):
    return lax.dot_general(a, b, (((1,), (1,)), ((), ())), preferred_element_type=F32)


def _ada_kernel(c_ref, w_ref, b_ref, o_ref):
    c = c_ref[...]
    a = c * jax.nn.sigmoid(c)
    a_hi, a_lo = _split_bf16(a)
    w_hi, w_lo = _split_bf16(w_ref[...])
    o_ref[...] = _dot(a_hi, w_hi) + _dot(a_lo, w_hi) + _dot(a_hi, w_lo) + b_ref[...]


def _ada(c, w_ada, b_ada):
    n, d = c.shape
    nout = w_ada.shape[1]
    tn = 1024
    return pl.pallas_call(
        _ada_kernel,
        out_shape=jax.ShapeDtypeStruct((n, nout), F32),
        grid=(nout // tn,),
        in_specs=[pl.BlockSpec((n, d), lambda j: (0, 0)),
                  pl.BlockSpec((d, tn), lambda j: (0, j)),
                  pl.BlockSpec((1, tn), lambda j: (0, j))],
        out_specs=pl.BlockSpec((n, tn), lambda j: (0, j)),
        compiler_params=_cparams(("parallel",)),
        name="ada",
    )(c, w_ada, b_ada.reshape(1, nout))


_SEG_QA = (0, D_A)
_SEG_KA = (D_A, 2 * D_A)
_SEG_VA = (2 * D_A, 3 * D_A)
_SEG_QB = (3 * D_A, 3 * D_A + D_B)
_SEG_KB = (3 * D_A + D_B, 3 * D_A + 2 * D_B)
_SEG_VB = (3 * D_A + 2 * D_B, 3 * D_A + 3 * D_B)
_SEG_GA = (3 * D_A + 3 * D_B, 3 * D_A + 3 * D_B + D_MODEL)
_SEG_GB = (3 * D_A + 3 * D_B + D_MODEL, N_IN)


def _inproj_kernel(x_ref, sh_ref, sc_ref, w_ref, b_ref,
                   qa_ref, ka32_ref, va32_ref, ka16_ref, va16_ref,
                   qb_ref, kb16_ref, vb16_ref, sga_ref, sgb_ref, kbt_ref, vbt_ref):
    u = (_ln(x_ref[...]) * (1.0 + _rows(sc_ref)) + _rows(sh_ref)).astype(BF16)

    def seg(lo_hi):
        lo, hi = lo_hi
        return _dot(u, w_ref[:, lo:hi]) + b_ref[:, lo:hi]

    qa_ref[...] = seg(_SEG_QA).astype(BF16)
    ka = seg(_SEG_KA)
    ka32_ref[...] = ka
    ka16_ref[...] = ka.astype(BF16)
    va = seg(_SEG_VA)
    va32_ref[...] = va
    va16_ref[...] = va.astype(BF16)
    qb_ref[...] = seg(_SEG_QB).astype(BF16)
    kb = seg(_SEG_KB)
    kbt_ref[...] = kb
    kb16_ref[...] = kb.astype(BF16)
    vb = seg(_SEG_VB)
    vbt_ref[...] = vb
    vb16_ref[...] = vb.astype(BF16)
    sga_ref[...] = jax.nn.sigmoid(seg(_SEG_GA)).astype(BF16)
    sgb_ref[...] = jax.nn.sigmoid(seg(_SEG_GB)).astype(BF16)


def _mod_spec(per_row, tm, rows_per_seq):
    if per_row:
        return pl.BlockSpec((tm, D_MODEL), lambda i: (i, 0))
    tiles_per_seq = rows_per_seq // tm
    return pl.BlockSpec((1, 1, D_MODEL), lambda i: (i // tiles_per_seq, 0, 0))


def _inproj(x, shift, scale, w_in16, b_in, *, tm, rows_per_seq, tail_rows, per_row):
    r = x.shape[0]
    nseq = r // rows_per_seq
    n_i = rows_per_seq // tm
    n_t = tail_rows // tm

    def tail_map(i):
        return ((i // n_i) * n_t + jnp.maximum(i % n_i - (n_i - n_t), 0), 0)

    row = lambda w: pl.BlockSpec((tm, w), lambda i: (i, 0))
    out_shape = (
        jax.ShapeDtypeStruct((r, D_A), BF16),
        jax.ShapeDtypeStruct((r, D_A), F32),
        jax.ShapeDtypeStruct((r, D_A), F32),
        jax.ShapeDtypeStruct((r, D_A), BF16),
        jax.ShapeDtypeStruct((r, D_A), BF16),
        jax.ShapeDtypeStruct((r, D_B), BF16),
        jax.ShapeDtypeStruct((r, D_B), BF16),
        jax.ShapeDtypeStruct((r, D_B), BF16),
        jax.ShapeDtypeStruct((r, D_MODEL), BF16),
        jax.ShapeDtypeStruct((r, D_MODEL), BF16),
        jax.ShapeDtypeStruct((nseq * tail_rows, D_B), F32),
        jax.ShapeDtypeStruct((nseq * tail_rows, D_B), F32),
    )
    out_specs = (row(D_A), row(D_A), row(D_A), row(D_A), row(D_A),
                 row(D_B), row(D_B), row(D_B), row(D_MODEL), row(D_MODEL),
                 pl.BlockSpec((tm, D_B), tail_map), pl.BlockSpec((tm, D_B), tail_map))
    return pl.pallas_call(
        _inproj_kernel,
        out_shape=out_shape,
        grid=(r // tm,),
        in_specs=[row(D_MODEL), _mod_spec(per_row, tm, rows_per_seq), _mod_spec(per_row, tm, rows_per_seq),
                  pl.BlockSpec((D_MODEL, N_IN), lambda i: (0, 0)),
                  pl.BlockSpec((1, N_IN), lambda i: (0, 0))],
        out_specs=out_specs,
        compiler_params=_cparams(("arbitrary",)),
        name="inproj",
    )(x, shift, scale, w_in16, b_in)


def _lam_value(lam4_ref, lam_init):
    v = lam4_ref[...]
    a = jnp.sum(v[0:1, :] * v[1:2, :], axis=-1, keepdims=True)
    b = jnp.sum(v[2:3, :] * v[3:4, :], axis=-1, keepdims=True)
    return jnp.exp(a) - jnp.exp(b) + lam_init


def _stack_halves(q, scale):
    lane = lax.broadcasted_iota(jnp.int32, q.shape, 1)
    qs = q * jnp.asarray(scale, q.dtype)
    zero = jnp.zeros_like(qs)
    return jnp.concatenate([jnp.where(lane < HEAD_DIM, qs, zero), jnp.where(lane >= HEAD_DIM, qs, zero)], axis=0)


def _softmax_step(s, v, m, l, acc):
    m_new = jnp.maximum(m, jnp.max(s, axis=-1, keepdims=True))
    alpha = jnp.exp(m - m_new)
    p = jnp.exp(s - m_new)
    l_new = alpha * l + jnp.sum(p, axis=-1, keepdims=True)
    acc_new = alpha * acc + _dot(p.astype(BF16), v)
    return m_new, l_new, acc_new


def _diff_finish(acc, l, tq, lam, g, lam_init):
    o = acc / l
    o = o[:tq] - lam * o[tq:]
    o = o * lax.rsqrt(jnp.mean(o * o, axis=-1, keepdims=True) + LN_EPS)
    return o * g * (1.0 - lam_init)


def _diffattn_kernel(slopes_ref, lam4_ref, g_ref, q_ref, k_ref, v_ref, o_ref, *, tq, lam_init):
    h = pl.program_id(1)
    qi = pl.program_id(2)
    slope = slopes_ref[h]
    qm = _stack_halves(q_ref[0], HEAD_DIM ** -0.5)
    ii = lax.broadcasted_iota(jnp.int32, (tq, tq), 0)
    jj = lax.broadcasted_iota(jnp.int32, (tq, tq), 1)
    dij = (ii - jj).astype(F32)
    past_bias = -slope * dij

    def past_block(j, carry):
        m, l, acc = carry
        start = pl.multiple_of(j * tq, tq)
        kj = k_ref[0, pl.ds(start, tq), :]
        vj = v_ref[0, pl.ds(start, tq), :]
        s = _dot_nt(qm, kj).reshape(2, tq, tq)
        bias = past_bias - slope * ((qi - j) * tq).astype(F32)
        s = (s + bias[None]).reshape(2 * tq, tq)
        return _softmax_step(s, vj, m, l, acc)

    init = (jnp.full((2 * tq, 1), NEG, F32), jnp.zeros((2 * tq, 1), F32), jnp.zeros((2 * tq, 2 * HEAD_DIM), F32))
    m, l, acc = lax.fori_loop(0, qi, past_block, init)

    start = pl.multiple_of(qi * tq, tq)
    kd = k_ref[0, pl.ds(start, tq), :]
    vd = v_ref[0, pl.ds(start, tq), :]
    visible = (jj // CHUNK) <= (ii // CHUNK)
    diag_bias = jnp.where(visible, -slope * jnp.abs(dij), NEG)
    s = (_dot_nt(qm, kd).reshape(2, tq, tq) + diag_bias[None]).reshape(2 * tq, tq)
    m, l, acc = _softmax_step(s, vd, m, l, acc)

    lam = _lam_value(lam4_ref, lam_init)
    o_ref[0] = _diff_finish(acc, l, tq, lam, g_ref[...], lam_init).astype(o_ref.dtype)


def _diffattn(qa, ka, va, slopes, lam4, subln_g, lam_init, *, tq):
    b, s, _ = qa.shape
    kern = functools.partial(_diffattn_kernel, tq=tq, lam_init=lam_init)
    return pl.pallas_call(
        kern,
        out_shape=jax.ShapeDtypeStruct((b, s, D_A), BF16),
        grid=(b, H_A, s // tq),
        in_specs=[pl.BlockSpec(memory_space=pltpu.SMEM),
                  pl.BlockSpec((4, HEAD_DIM), lambda bi, h, i: (0, 0)),
                  pl.BlockSpec((1, 2 * HEAD_DIM), lambda bi, h, i: (0, 0)),
                  pl.BlockSpec((1, tq, 2 * HEAD_DIM), lambda bi, h, i: (bi, i, h)),
                  pl.BlockSpec((1, s, 2 * HEAD_DIM), lambda bi, h, i: (bi, 0, h)),
                  pl.BlockSpec((1, s, 2 * HEAD_DIM), lambda bi, h, i: (bi, 0, h))],
        out_specs=pl.BlockSpec((1, tq, 2 * HEAD_DIM), lambda bi, h, i: (bi, i, h)),
        compiler_params=_cparams(("parallel", "parallel", "arbitrary")),
        name="diffattn",
    )(slopes, lam4, subln_g.reshape(1, 2 * HEAD_DIM), qa, ka, va)


def _band_window(s, tq):
    return min(BAND_CHUNKS * CHUNK + tq, s)


def _band_bias_tiles(rel_table, s, tq):
    win = _band_window(s, tq)
    n_var = min(BAND_CHUNKS * CHUNK // tq + 1, s // tq)
    tiles = []
    i = np.arange(tq)[:, None]
    j = np.arange(win)[None, :]
    for var in range(n_var):
        q0 = var * tq
        k0 = max(q0 - BAND_CHUNKS * CHUNK, 0)
        t, sp = q0 + i, k0 + j
        valid = (sp // CHUNK <= t // CHUNK) & (sp // CHUNK >= t // CHUNK - BAND_CHUNKS)
        rel = np.clip(t - sp, -REL_CLIP, REL_CLIP) + REL_CLIP
        bias = jnp.take(rel_table, jnp.asarray(rel, jnp.int32), axis=1).astype(F32)
        tiles.append(jnp.where(jnp.asarray(valid)[None], bias, NEG))
    return jnp.stack(tiles)


def _pair_softmax_out(qm, kw, vw, bias2, t):
    s = (_dot_nt(qm, kw).reshape(2, t, -1) + bias2).reshape(2 * t, -1)
    m = jnp.max(s, axis=-1, keepdims=True)
    p = jnp.exp(s - m)
    l = jnp.sum(p, axis=-1, keepdims=True)
    o = _dot(p.astype(BF16), vw) / l
    lane = lax.broadcasted_iota(jnp.int32, (t, 2 * HEAD_DIM), 1)
    return jnp.where(lane < HEAD_DIM, o[:t], o[t:])


def _bandattn_kernel(bias_ref, q_ref, k_ref, v_ref, o_ref, *, tq, win, n_var):
    qi = pl.program_id(2)
    var = jnp.minimum(qi, n_var - 1)
    k0 = pl.multiple_of(jnp.maximum(qi * tq - BAND_CHUNKS * CHUNK, 0), tq)
    qm = _stack_halves(q_ref[0], HEAD_DIM ** -0.5)
    kw = k_ref[0, pl.ds(k0, win), :]
    vw = v_ref[0, pl.ds(k0, win), :]
    o_ref[0] = _pair_softmax_out(qm, kw, vw, bias_ref[var], tq).astype(o_ref.dtype)


def _bandattn(qb, kb, vb, bias_tiles, *, tq):
    b, s, _ = qb.shape
    n_var, _, _, win = bias_tiles.shape
    kern = functools.partial(_bandattn_kernel, tq=tq, win=win, n_var=n_var)
    return pl.pallas_call(
        kern,
        out_shape=jax.ShapeDtypeStruct((b, s, D_B), BF16),
        grid=(H_B // 2, b, s // tq),
        in_specs=[pl.BlockSpec((n_var, 2, tq, win), lambda hp, bi, i: (0, hp, 0, 0)),
                  pl.BlockSpec((1, tq, 2 * HEAD_DIM), lambda hp, bi, i: (bi, i, hp)),
                  pl.BlockSpec((1, s, 2 * HEAD_DIM), lambda hp, bi, i: (bi, 0, hp)),
                  pl.BlockSpec((1, s, 2 * HEAD_DIM), lambda hp, bi, i: (bi, 0, hp))],
        out_specs=pl.BlockSpec((1, tq, 2 * HEAD_DIM), lambda hp, bi, i: (bi, i, hp)),
        compiler_params=_cparams(("parallel", "parallel", "arbitrary")),
        name="bandattn",
    )(bias_tiles, qb, kb, vb)


def _sampattn_kernel(slopes_ref, lam4_ref, g_ref, bbias_ref, qa_ref, ckd_ref, cvd_ref, kan_ref, van_ref,
                     qb_ref, ckb_ref, cvb_ref, kbn_ref, vbn_ref, oa_ref, ob_ref, *, n, past, lam_init):
    lam = _lam_value(lam4_ref, lam_init)
    tq_pos = past + lax.broadcasted_iota(jnp.int32, (n, past), 0)
    ts_pos = lax.broadcasted_iota(jnp.int32, (n, past), 1)
    dist_c = jnp.abs(tq_pos - ts_pos).astype(F32)
    vis_c = (ts_pos // CHUNK) <= (tq_pos // CHUNK)
    ii = lax.broadcasted_iota(jnp.int32, (n, n), 0)
    jj = lax.broadcasted_iota(jnp.int32, (n, n), 1)
    dist_n = jnp.abs(ii - jj).astype(F32)
    vis_n = ((past + jj) // CHUNK) <= ((past + ii) // CHUNK)
    for h in range(H_A):
        sl = slice(h * 2 * HEAD_DIM, (h + 1) * 2 * HEAD_DIM)
        slope = slopes_ref[h]
        qm = _stack_halves(qa_ref[0, :, sl], HEAD_DIM ** -0.5)
        kc = ckd_ref[0, :, sl].astype(BF16)
        vc = cvd_ref[0, :, sl].astype(BF16)
        kn = kan_ref[0, :, sl]
        vn = van_ref[0, :, sl]
        bias_c = jnp.where(vis_c, -slope * dist_c, NEG)
        bias_n = jnp.where(vis_n, -slope * dist_n, NEG)
        s_c = (_dot_nt(qm, kc).reshape(2, n, past) + bias_c[None]).reshape(2 * n, past)
        s_n = (_dot_nt(qm, kn).reshape(2, n, n) + bias_n[None]).reshape(2 * n, n)
        m = jnp.maximum(jnp.max(s_c, axis=-1, keepdims=True), jnp.max(s_n, axis=-1, keepdims=True))
        p_c = jnp.exp(s_c - m)
        p_n = jnp.exp(s_n - m)
        l = jnp.sum(p_c, axis=-1, keepdims=True) + jnp.sum(p_n, axis=-1, keepdims=True)
        acc = _dot(p_c.astype(BF16), vc) + _dot(p_n.astype(BF16), vn)
        oa_ref[0, :, sl] = _diff_finish(acc, l, n, lam, g_ref[...], lam_init).astype(oa_ref.dtype)
    lb = ckb_ref.shape[1]
    for hp in range(H_B // 2):
        sl = slice(hp * 2 * HEAD_DIM, (hp + 1) * 2 * HEAD_DIM)
        qm = _stack_halves(qb_ref[0, :, sl], HEAD_DIM ** -0.5)
        kc = ckb_ref[0, :, sl].astype(BF16)
        vc = cvb_ref[0, :, sl].astype(BF16)
        kn = kbn_ref[0, :, sl]
        vn = vbn_ref[0, :, sl]
        bias = bbias_ref[2 * hp:2 * hp + 2]
        s_c = (_dot_nt(qm, kc).reshape(2, n, lb) + bias[:, :, :lb]).reshape(2 * n, lb)
        s_n = (_dot_nt(qm, kn).reshape(2, n, n) + bias[:, :, lb:]).reshape(2 * n, n)
        m = jnp.maximum(jnp.max(s_c, axis=-1, keepdims=True), jnp.max(s_n, axis=-1, keepdims=True))
        p_c = jnp.exp(s_c - m)
        p_n = jnp.exp(s_n - m)
        l = jnp.sum(p_c, axis=-1, keepdims=True) + jnp.sum(p_n, axis=-1, keepdims=True)
        o = (_dot(p_c.astype(BF16), vc) + _dot(p_n.astype(BF16), vn)) / l
        lane = lax.broadcasted_iota(jnp.int32, (n, 2 * HEAD_DIM), 1)
        ob_ref[0, :, sl] = jnp.where(lane < HEAD_DIM, o[:n], o[n:]).astype(ob_ref.dtype)


def _sample_band_bias(rel_table, n, past, lb):
    tq = past + np.arange(n)[:, None]
    ts = (past - lb) + np.arange(lb + n)[None, :]
    valid = (ts // CHUNK <= tq // CHUNK) & (ts // CHUNK >= tq // CHUNK - BAND_CHUNKS)
    rel = np.clip(tq - ts, -REL_CLIP, REL_CLIP) + REL_CLIP
    bias = jnp.take(rel_table, jnp.asarray(rel, jnp.int32), axis=1).astype(F32)
    return jnp.where(jnp.asarray(valid)[None], bias, NEG)


def _sampattn(qa, ckd, cvd, kan, van, qb, ckb, cvb, kbn, vbn, slopes, lam4, subln_g, bbias, lam_init):
    b, n, _ = qa.shape
    past = ckd.shape[1]
    lb = ckb.shape[1]
    kern = functools.partial(_sampattn_kernel, n=n, past=past, lam_init=lam_init)
    full = lambda shape: pl.BlockSpec(shape, lambda bi: (0,) * len(shape))
    per_b = lambda r, w: pl.BlockSpec((1, r, w), lambda bi: (bi, 0, 0))
    return pl.pallas_call(
        kern,
        out_shape=(jax.ShapeDtypeStruct((b, n, D_A), BF16), jax.ShapeDtypeStruct((b, n, D_B), BF16)),
        grid=(b,),
        in_specs=[pl.BlockSpec(memory_space=pltpu.SMEM), full((4, HEAD_DIM)), full((1, 2 * HEAD_DIM)),
                  full((H_B, n, lb + n)),
                  per_b(n, D_A), per_b(past, D_A), per_b(past, D_A), per_b(n, D_A), per_b(n, D_A),
                  per_b(n, D_B), per_b(lb, D_B), per_b(lb, D_B), per_b(n, D_B), per_b(n, D_B)],
        out_specs=(per_b(n, D_A), per_b(n, D_B)),
        compiler_params=_cparams(("parallel",)),
        name="sampattn",
    )(slopes, lam4, subln_g.reshape(1, 2 * HEAD_DIM), bbias, qa, ckd, cvd, kan, van, qb, ckb, cvb, kbn, vbn)


def _outproj_kernel(oa_ref, ob_ref, sga_ref, sgb_ref, x_ref, g1_ref, sh2_ref, sc2_ref, pa_ref, pb_ref, wo_ref,
                    lng_ref, lnb_ref, x1_ref, h_ref, *, alpha):
    mix = (sga_ref[...].astype(F32) * _dot(oa_ref[...], pa_ref[...])
           + sgb_ref[...].astype(F32) * _dot(ob_ref[...], pb_ref[...]))
    y = _dot(mix.astype(BF16), wo_ref[...])
    x1 = _ln(alpha * x_ref[...] + _rows(g1_ref) * y) * lng_ref[...] + lnb_ref[...]
    x1_ref[...] = x1
    h_ref[...] = (_ln(x1) * (1.0 + _rows(sc2_ref)) + _rows(sh2_ref)).astype(BF16)


def _outproj(oa, ob, sga, sgb, x, gate1, shift2, scale2, pa16, pb16, wo16, ln_g, ln_b, *, tm, rows_per_seq,
             per_row, alpha):
    r = x.shape[0]
    row = lambda w: pl.BlockSpec((tm, w), lambda i: (i, 0))
    full = lambda a, b: pl.BlockSpec((a, b), lambda i: (0, 0))
    mod = _mod_spec(per_row, tm, rows_per_seq)
    return pl.pallas_call(
        functools.partial(_outproj_kernel, alpha=alpha),
        out_shape=(jax.ShapeDtypeStruct((r, D_MODEL), F32), jax.ShapeDtypeStruct((r, D_MODEL), BF16)),
        grid=(r // tm,),
        in_specs=[row(D_A), row(D_B), row(D_MODEL), row(D_MODEL), row(D_MODEL), mod, mod, mod,
                  full(D_A, D_MODEL), full(D_B, D_MODEL), full(D_MODEL, D_MODEL),
                  full(1, D_MODEL), full(1, D_MODEL)],
        out_specs=(row(D_MODEL), row(D_MODEL)),
        compiler_params=_cparams(("parallel",)),
        name="outproj",
    )(oa, ob, sga, sgb, x, gate1, shift2, scale2, pa16, pb16, wo16, ln_g.reshape(1, -1), ln_b.reshape(1, -1))


def _first_index_of_max(x, iota, axis, size):
    m = jnp.max(x, axis=axis, keepdims=True)
    idx = jnp.min(jnp.where(x == m, iota, size), axis=axis, keepdims=True)
    return m, idx


def _router_kernel(h_ref, wr_hi_ref, wr_lo_ref, eb_ref, idx_ref, wt_ref, rank_ref, cnt_ref, run_ref, *, tr):
    step = pl.program_id(0)

    @pl.when(step == 0)
    def _():
        run_ref[...] = jnp.zeros_like(run_ref)

    h = h_ref[...]
    logits = _dot_nt(wr_hi_ref[...], h) + _dot_nt(wr_lo_ref[...], h)
    scores = jax.nn.sigmoid(logits)
    biased = scores + eb_ref[...]

    x3 = biased.reshape(N_GROUP, GROUP_SIZE, tr)
    io3 = lax.broadcasted_iota(jnp.int32, x3.shape, 1)
    m1, i1 = _first_index_of_max(x3, io3, 1, GROUP_SIZE)
    m2 = jnp.max(jnp.where(io3 == i1, -jnp.inf, x3), axis=1, keepdims=True)
    grp = (m1 + m2).reshape(N_GROUP, tr)

    iog = lax.broadcasted_iota(jnp.int32, grp.shape, 0)
    gsel = jnp.zeros(grp.shape, jnp.bool_)
    for _ in range(TOPK_GROUP):
        _, gi = _first_index_of_max(grp, iog, 0, N_GROUP)
        hit = iog == gi
        gsel = gsel | hit
        grp = jnp.where(hit, -jnp.inf, grp)
    emask = jnp.broadcast_to(gsel.reshape(N_GROUP, 1, tr), (N_GROUP, GROUP_SIZE, tr)).reshape(N_EXPERTS, tr)
    cand = jnp.where(emask, biased, -jnp.inf)

    ioe = lax.broadcasted_iota(jnp.int32, cand.shape, 0)
    hits = []
    sel = jnp.zeros(cand.shape, jnp.bool_)
    for _ in range(TOP_K):
        _, ei = _first_index_of_max(cand, ioe, 0, N_EXPERTS)
        hit = ioe == ei
        hits.append((ei, hit))
        sel = sel | hit
        cand = jnp.where(hit, -jnp.inf, cand)
    self32 = sel.astype(F32)
    ra = lax.broadcasted_iota(jnp.int32, (tr, tr), 0)
    rb = lax.broadcasted_iota(jnp.int32, (tr, tr), 1)
    upper = (ra < rb).astype(BF16)
    ranks = run_ref[...] + _dot(self32.astype(BF16), upper)

    ws = [jnp.sum(jnp.where(hit, scores, 0.0), axis=0, keepdims=True) for _, hit in hits]
    wsum = ws[0]
    for w in ws[1:]:
        wsum = wsum + w
    for k, (ei, hit) in enumerate(hits):
        idx_ref[k:k + 1, :] = ei
        wt_ref[k:k + 1, :] = ws[k] / wsum * ROUTED_SCALE
        rank_ref[k:k + 1, :] = jnp.sum(jnp.where(hit, ranks, 0.0), axis=0, keepdims=True).astype(jnp.int32)
    run_new = run_ref[...] + jnp.sum(self32, axis=1, keepdims=True)
    run_ref[...] = run_new
    cnt_ref[...] = jnp.broadcast_to(run_new, cnt_ref.shape).astype(jnp.int32)


def _router(h_all, w_router, e_bias, *, tr):
    t = h_all.shape[0]
    wr_t = w_router.T
    wr_hi = wr_t.astype(BF16)
    wr_lo = (wr_t - wr_hi.astype(F32)).astype(BF16)
    full = lambda a, b: pl.BlockSpec((a, b), lambda i: (0, 0))
    col = lambda: pl.BlockSpec((TOP_K, tr), lambda i: (0, i))
    return pl.pallas_call(
        functools.partial(_router_kernel, tr=tr),
        out_shape=(jax.ShapeDtypeStruct((TOP_K, t), jnp.int32), jax.ShapeDtypeStruct((TOP_K, t), F32),
                   jax.ShapeDtypeStruct((TOP_K, t), jnp.int32), jax.ShapeDtypeStruct((N_EXPERTS, 128), jnp.int32)),
        grid=(t // tr,),
        in_specs=[pl.BlockSpec((tr, D_MODEL), lambda i: (i, 0)), full(N_EXPERTS, D_MODEL),
                  full(N_EXPERTS, D_MODEL), full(N_EXPERTS, 1)],
        out_specs=(col(), col(), col(), full(N_EXPERTS, 128)),
        scratch_shapes=[pltpu.VMEM((N_EXPERTS, 1), F32)],
        compiler_params=_cparams(("arbitrary",)),
        name="router",
    )(h_all, wr_hi, wr_lo, e_bias.reshape(N_EXPERTS, 1).astype(F32))


def _experts_kernel(be_ref, nu_ref, x_ref, w13_ref, w2_ref, y_ref):
    @pl.when(pl.program_id(0) < nu_ref[0])
    def _():
        x = x_ref[...]
        a = _dot(x, w13_ref[0])
        hid = (a[:, :D_EXPERT] * jax.nn.sigmoid(a[:, :D_EXPERT])) * a[:, D_EXPERT:]
        y_ref[...] = _dot(hid.astype(BF16), w2_ref[0]).astype(y_ref.dtype)


def _experts(xs, w13, w2, block_e, n_used):
    n_slots = xs.shape[0]
    nb = n_slots // EXPERT_BLOCK
    last = lambda i, nu: jnp.minimum(i, nu[0] - 1)
    return pl.pallas_call(
        _experts_kernel,
        out_shape=jax.ShapeDtypeStruct((n_slots, D_MODEL), BF16),
        grid_spec=pltpu.PrefetchScalarGridSpec(
            num_scalar_prefetch=2,
            grid=(nb,),
            in_specs=[pl.BlockSpec((EXPERT_BLOCK, D_MODEL), lambda i, be, nu: (last(i, nu), 0)),
                      pl.BlockSpec((1, D_MODEL, 2 * D_EXPERT), lambda i, be, nu: (be[last(i, nu)], 0, 0)),
                      pl.BlockSpec((1, D_EXPERT, D_MODEL), lambda i, be, nu: (be[last(i, nu)], 0, 0))],
            out_specs=pl.BlockSpec((EXPERT_BLOCK, D_MODEL), lambda i, be, nu: (last(i, nu), 0)),
        ),
        compiler_params=_cparams(("arbitrary",)),
        name="experts",
    )(block_e, n_used, xs, w13, w2)


def _final_kernel(h_ref, x1_ref, yk_ref, wt_ref, g2_ref, ws13_ref, ws2_ref, lng_ref, lnb_ref, o_ref, *, alpha):
    h = h_ref[...]
    a = _dot(h, ws13_ref[...])
    hid = (a[:, :D_EXPERT] * jax.nn.sigmoid(a[:, :D_EXPERT])) * a[:, D_EXPERT:]
    y = _dot(hid.astype(BF16), ws2_ref[...])
    wt = wt_ref[...]
    for k in range(TOP_K):
        y = y + wt[:, k:k + 1] * yk_ref[k].astype(F32)
    o_ref[...] = _ln(alpha * x1_ref[...] + _rows(g2_ref) * y) * lng_ref[...] + lnb_ref[...]


def _final(h_all, x1, yk, wt, gate2, ws13, ws2, ln_g, ln_b, *, tm, row0, rows_per_seq, per_row, alpha):
    r = x1.shape[0]
    off = row0 // tm
    row = lambda w: pl.BlockSpec((tm, w), lambda i: (i, 0))
    row_off = lambda w: pl.BlockSpec((tm, w), lambda i: (i + off, 0))
    full = lambda a, b: pl.BlockSpec((a, b), lambda i: (0, 0))
    return pl.pallas_call(
        functools.partial(_final_kernel, alpha=alpha),
        out_shape=jax.ShapeDtypeStruct((r, D_MODEL), F32),
        grid=(r // tm,),
        in_specs=[row_off(D_MODEL), row(D_MODEL),
                  pl.BlockSpec((TOP_K, tm, D_MODEL), lambda i: (0, i + off, 0)),
                  row_off(TOP_K), _mod_spec(per_row, tm, rows_per_seq),
                  full(D_MODEL, 2 * D_EXPERT), full(D_EXPERT, D_MODEL), full(1, D_MODEL), full(1, D_MODEL)],
        out_specs=row(D_MODEL),
        compiler_params=_cparams(("parallel",)),
        name="final",
    )(h_all, x1, yk, wt, gate2, ws13, ws2, ln_g.reshape(1, -1), ln_b.reshape(1, -1))


def _pick_tile(n, pref):
    t = min(pref, n)
    while n % t:
        t //= 2
    return t


def kernel(x_prompt, x_sample, cache_k_diff, cache_v_diff, cache_k_band, cache_v_band, c_prompt, c_sample,
           w_ada, b_ada, w_in, b_in, lambda_q1, lambda_k1, lambda_q2, lambda_k2, subln_g, rel_bias, p_a, p_b,
           w_out, ln1_g, ln1_b, w_router, e_bias, w1, w3, w2, ws1, ws3, ws2, ln2_g, ln2_b):
    depth = w_in.shape[0]
    alpha = (2 * depth) ** 0.25
    bp, sp, d = x_prompt.shape
    bs, ss, _ = x_sample.shape
    past = cache_k_diff.shape[2]
    lb = cache_k_band.shape[2]
    tp, ts_ = bp * sp, bs * ss
    t_all = tp + ts_
    slopes = jnp.asarray(2.0 ** (-8.0 * np.arange(1, H_A + 1) / H_A), F32)

    yp = x_prompt.reshape(tp, d)
    ys = x_sample.reshape(ts_, d)
    outs = [[] for _ in range(8)]
    tail_p = min(BAND_CHUNKS * CHUNK, sp)
    tm_p = _pick_tile(math.gcd(sp, tail_p), 256)
    tq = _pick_tile(sp, 256)

    for l in range(depth):
        lam_init = 0.8 - 0.6 * math.exp(-0.3 * l)
        lam4 = jnp.stack([lambda_q1[l], lambda_k1[l], lambda_q2[l], lambda_k2[l]]).astype(F32)
        w_in16 = w_in[l].astype(BF16)
        b_in2 = b_in[l].reshape(1, N_IN)
        pa16, pb16, wo16 = p_a[l].astype(BF16), p_b[l].astype(BF16), w_out[l].astype(BF16)

        mod = _ada(jnp.concatenate([c_prompt, c_sample], axis=0), w_ada[l], b_ada[l])
        mod_p = [m.reshape(bp, 1, d) for m in jnp.split(mod[:bp], 6, axis=-1)]
        mod_s = [jnp.repeat(m, ss, axis=0) for m in jnp.split(mod[bp:], 6, axis=-1)]

        (qa, ka32, va32, ka16, va16, qb, kb16, vb16, sga, sgb, kbt, vbt) = _inproj(
            yp, mod_p[0], mod_p[1], w_in16, b_in2, tm=tm_p, rows_per_seq=sp, tail_rows=tail_p, per_row=False)
        r3 = lambda a: a.reshape(bp, sp, a.shape[-1])
        oa = _diffattn(r3(qa), r3(ka16), r3(va16), slopes, lam4, subln_g[l], lam_init, tq=tq)
        ob = _bandattn(r3(qb), r3(kb16), r3(vb16), _band_bias_tiles(rel_bias[l], sp, tq), tq=tq)
        x1p, hp = _outproj(oa.reshape(tp, D_A), ob.reshape(tp, D_B), sga, sgb, yp, mod_p[2], mod_p[3], mod_p[4],
                           pa16, pb16, wo16, ln1_g[l], ln1_b[l], tm=_pick_tile(sp, 512), rows_per_seq=sp,
                           per_row=False, alpha=alpha)
        outs[0].append(ka32.reshape(bp, sp, H_A, 2, HEAD_DIM))
        outs[1].append(va32.reshape(bp, sp, H_A, 2 * HEAD_DIM))
        outs[2].append(kbt.reshape(bp, tail_p, H_B, HEAD_DIM))
        outs[3].append(vbt.reshape(bp, tail_p, H_B, HEAD_DIM))

        (qa_s, ka32_s, va32_s, ka16_s, va16_s, qb_s, kb16_s, vb16_s, sga_s, sgb_s, kbt_s, vbt_s) = _inproj(
            ys, mod_s[0], mod_s[1], w_in16, b_in2, tm=ts_, rows_per_seq=ts_, tail_rows=ts_, per_row=True)
        s3 = lambda a: a.reshape(bs, ss, a.shape[-1])
        oa_s, ob_s = _sampattn(
            s3(qa_s), cache_k_diff[l].reshape(bs, past, D_A), cache_v_diff[l].reshape(bs, past, D_A),
            s3(ka16_s), s3(va16_s), s3(qb_s), cache_k_band[l].reshape(bs, lb, D_B),
            cache_v_band[l].reshape(bs, lb, D_B), s3(kb16_s), s3(vb16_s), slopes, lam4, subln_g[l],
            _sample_band_bias(rel_bias[l], ss, past, lb), lam_init)
        x1s, hs = _outproj(oa_s.reshape(ts_, D_A), ob_s.reshape(ts_, D_B), sga_s, sgb_s, ys, mod_s[2], mod_s[3],
                           mod_s[4], pa16, pb16, wo16, ln1_g[l], ln1_b[l], tm=ts_, rows_per_seq=ts_, per_row=True,
                           alpha=alpha)
        outs[4].append(ka32_s.reshape(bs, ss, H_A, 2, HEAD_DIM))
        outs[5].append(va32_s.reshape(bs, ss, H_A, 2 * HEAD_DIM))
        outs[6].append(kbt_s.reshape(bs, ss, H_B, HEAD_DIM))
        outs[7].append(vbt_s.reshape(bs, ss, H_B, HEAD_DIM))

        h_all = jnp.concatenate([hp, hs], axis=0)
        tr = _pick_tile(t_all, 512)
        idx_t, wt_t, rank_t, cnt = _router(h_all, w_router[l], e_bias[l], tr=tr)
        counts = cnt[:, 0]
        padded = (counts + EXPERT_BLOCK - 1) // EXPERT_BLOCK * EXPERT_BLOCK
        pad_end = jnp.cumsum(padded)
        pad_start = pad_end - padded
        dest = pad_start[idx_t] + rank_t
        nb = -(-(t_all * TOP_K) // EXPERT_BLOCK) + N_EXPERTS
        n_slots = nb * EXPERT_BLOCK
        block_e = jnp.minimum(jnp.searchsorted(pad_end, jnp.arange(nb) * EXPERT_BLOCK, side='right'),
                              N_EXPERTS - 1).astype(jnp.int32)
        n_used = (pad_end[-1:] // EXPERT_BLOCK).astype(jnp.int32)
        tok = jnp.broadcast_to(jnp.arange(t_all, dtype=jnp.int32)[None], (TOP_K, t_all))
        slot_tok = jnp.full((n_slots,), t_all, jnp.int32).at[dest.reshape(-1)].set(tok.reshape(-1))
        h_pad = jnp.concatenate([h_all, jnp.zeros((1, d), BF16)], axis=0)
        xs = h_pad[slot_tok]
        w13 = jnp.concatenate([w1[l], w3[l]], axis=-1).astype(BF16)
        yb = _experts(xs, w13, w2[l].astype(BF16), block_e, n_used)
        yk = yb[dest]
        wt = wt_t.T
        ws13 = jnp.concatenate([ws1[l], ws3[l]], axis=-1).astype(BF16)
        ws2b = ws2[l].astype(BF16)
        yp = _final(h_all, x1p, yk, wt, mod_p[5], ws13, ws2b, ln2_g[l], ln2_b[l], tm=_pick_tile(sp, 256), row0=0,
                    rows_per_seq=sp, per_row=False, alpha=alpha)
        ys = _final(h_all, x1s, yk, wt, mod_s[5], ws13, ws2b, ln2_g[l], ln2_b[l], tm=ts_, row0=tp,
                    rows_per_seq=ts_, per_row=True, alpha=alpha)

    return (yp.reshape(bp, sp, d), ys.reshape(bs, ss, d)) + tuple(jnp.stack(o) for o in outs)
```

```python
import functools
import math

import jax
import jax.numpy as jnp
import numpy as np
from jax import lax
from jax.experimental import pallas as pl
from jax.experimental.pallas import tpu as pltpu

F32 = jnp.float32
BF16 = jnp.bfloat16

D_MODEL = 1024
CHUNK = 64
HEAD_DIM = 64
H_A = 8
H_B = 8
D_A = H_A * 2 * HEAD_DIM
D_B = H_B * HEAD_DIM
BAND_CHUNKS = 8
REL_CLIP = 128
N_IN = 3 * D_A + 3 * D_B + 2 * D_MODEL
N_EXPERTS = 64
TOP_K = 8
N_GROUP = 8
TOPK_GROUP = 4
GROUP_SIZE = N_EXPERTS // N_GROUP
D_EXPERT = 256
ROUTED_SCALE = 2.5
EXPERT_BLOCK = 256
LN_EPS = 1e-5
NEG = -1e30

VMEM_LIMIT = 56 * 1024 * 1024


def _cparams(sem):
    return pltpu.CompilerParams(dimension_semantics=sem, vmem_limit_bytes=VMEM_LIMIT)


def _ln(x):
    mu = jnp.mean(x, axis=-1, keepdims=True)
    xc = x - mu
    var = jnp.mean(xc * xc, axis=-1, keepdims=True)
    return xc * lax.rsqrt(var + LN_EPS)


def _rows(ref):
    v = ref[...]
    return v.reshape(v.shape[-2], v.shape[-1])


def _split_bf16(x):
    hi = x.astype(BF16)
    lo = (x - hi.astype(F32)).astype(BF16)
    return hi, lo


def _dot(a, b):
    return jnp.dot(a, b, preferred_element_type=F32)


def _dot_nt(a, b):
    return lax.dot_general(a, b, (((1,), (1,)), ((), ())), preferred_element_type=F32)


PACKED = D_MODEL // 2


def _pack_pair(a, b):
    ua = lax.bitcast_convert_type(a.astype(BF16).astype(F32), jnp.uint32)
    ub = lax.bitcast_convert_type(b.astype(BF16).astype(F32), jnp.uint32)
    return ua | (ub >> 16)


def _unpack_pair(u):
    a = lax.bitcast_convert_type(u & jnp.uint32(0xFFFF0000), F32).astype(BF16)
    b = lax.bitcast_convert_type(u << 16, F32).astype(BF16)
    return a, b


def _dot_packed(u, w):
    a, b = _unpack_pair(u)
    return _dot(a, w[:PACKED]) + _dot(b, w[PACKED:])


def _ada_kernel(c_ref, w_ref, b_ref, o_ref):
    c = c_ref[...]
    a = c * jax.nn.sigmoid(c)
    a_hi, a_lo = _split_bf16(a)
    w_hi, w_lo = _split_bf16(w_ref[...])
    o_ref[...] = _dot(a_hi, w_hi) + _dot(a_lo, w_hi) + _dot(a_hi, w_lo) + b_ref[...]


def _ada(c, w_ada, b_ada):
    n, d = c.shape
    nout = w_ada.shape[1]
    tn = 1024
    return pl.pallas_call(
        _ada_kernel,
        out_shape=jax.ShapeDtypeStruct((n, nout), F32),
        grid=(nout // tn,),
        in_specs=[pl.BlockSpec((n, d), lambda j: (0, 0)),
                  pl.BlockSpec((d, tn), lambda j: (0, j)),
                  pl.BlockSpec((1, tn), lambda j: (0, j))],
        out_specs=pl.BlockSpec((n, tn), lambda j: (0, j)),
        compiler_params=_cparams(("parallel",)),
        name="ada",
    )(c, w_ada, b_ada.reshape(1, nout))


_SEG_QA = (0, D_A)
_SEG_KA = (D_A, 2 * D_A)
_SEG_VA = (2 * D_A, 3 * D_A)
_SEG_QB = (3 * D_A, 3 * D_A + D_B)
_SEG_KB = (3 * D_A + D_B, 3 * D_A + 2 * D_B)
_SEG_VB = (3 * D_A + 2 * D_B, 3 * D_A + 3 * D_B)
_SEG_GA = (3 * D_A + 3 * D_B, 3 * D_A + 3 * D_B + D_MODEL)
_SEG_GB = (3 * D_A + 3 * D_B + D_MODEL, N_IN)


def _inproj_kernel(x_ref, sh_ref, sc_ref, w_ref, b_ref,
                   qa_ref, ka32_ref, va32_ref, ka16_ref, va16_ref,
                   qb_ref, kb16_ref, vb16_ref, sga_ref, sgb_ref, kbt_ref, vbt_ref):
    u = (_ln(x_ref[...]) * (1.0 + _rows(sc_ref)) + _rows(sh_ref)).astype(BF16)

    def seg(lo_hi):
        lo, hi = lo_hi
        return _dot(u, w_ref[:, lo:hi]) + b_ref[:, lo:hi]

    qa_ref[...] = seg(_SEG_QA).astype(BF16)
    ka = seg(_SEG_KA)
    ka32_ref[...] = ka
    ka16_ref[...] = ka.astype(BF16)
    va = seg(_SEG_VA)
    va32_ref[...] = va
    va16_ref[...] = va.astype(BF16)
    qb_ref[...] = seg(_SEG_QB).astype(BF16)
    kb = seg(_SEG_KB)
    kbt_ref[...] = kb
    kb16_ref[...] = kb.astype(BF16)
    vb = seg(_SEG_VB)
    vbt_ref[...] = vb
    vb16_ref[...] = vb.astype(BF16)
    sga_ref[...] = jax.nn.sigmoid(seg(_SEG_GA)).astype(BF16)
    sgb_ref[...] = jax.nn.sigmoid(seg(_SEG_GB)).astype(BF16)


def _mod_spec(per_row, tm, rows_per_seq):
    if per_row:
        return pl.BlockSpec((tm, D_MODEL), lambda i: (i, 0))
    tiles_per_seq = rows_per_seq // tm
    return pl.BlockSpec((1, 1, D_MODEL), lambda i: (i // tiles_per_seq, 0, 0))


def _inproj(x, shift, scale, w_in16, b_in, *, tm, rows_per_seq, tail_rows, per_row):
    r = x.shape[0]
    nseq = r // rows_per_seq
    n_i = rows_per_seq // tm
    n_t = tail_rows // tm

    def tail_map(i):
        return ((i // n_i) * n_t + jnp.maximum(i % n_i - (n_i - n_t), 0), 0)

    row = lambda w: pl.BlockSpec((tm, w), lambda i: (i, 0))
    out_shape = (
        jax.ShapeDtypeStruct((r, D_A), BF16),
        jax.ShapeDtypeStruct((r, D_A), F32),
        jax.ShapeDtypeStruct((r, D_A), F32),
        jax.ShapeDtypeStruct((r, D_A), BF16),
        jax.ShapeDtypeStruct((r, D_A), BF16),
        jax.ShapeDtypeStruct((r, D_B), BF16),
        jax.ShapeDtypeStruct((r, D_B), BF16),
        jax.ShapeDtypeStruct((r, D_B), BF16),
        jax.ShapeDtypeStruct((r, D_MODEL), BF16),
        jax.ShapeDtypeStruct((r, D_MODEL), BF16),
        jax.ShapeDtypeStruct((nseq * tail_rows, D_B), F32),
        jax.ShapeDtypeStruct((nseq * tail_rows, D_B), F32),
    )
    out_specs = (row(D_A), row(D_A), row(D_A), row(D_A), row(D_A),
                 row(D_B), row(D_B), row(D_B), row(D_MODEL), row(D_MODEL),
                 pl.BlockSpec((tm, D_B), tail_map), pl.BlockSpec((tm, D_B), tail_map))
    return pl.pallas_call(
        _inproj_kernel,
        out_shape=out_shape,
        grid=(r // tm,),
        in_specs=[row(D_MODEL), _mod_spec(per_row, tm, rows_per_seq), _mod_spec(per_row, tm, rows_per_seq),
                  pl.BlockSpec((D_MODEL, N_IN), lambda i: (0, 0)),
                  pl.BlockSpec((1, N_IN), lambda i: (0, 0))],
        out_specs=out_specs,
        compiler_params=_cparams(("arbitrary",)),
        name="inproj",
    )(x, shift, scale, w_in16, b_in)


def _lam_value(lam4_ref, lam_init):
    v = lam4_ref[...]
    a = jnp.sum(v[0:1, :] * v[1:2, :], axis=-1, keepdims=True)
    b = jnp.sum(v[2:3, :] * v[3:4, :], axis=-1, keepdims=True)
    return jnp.exp(a) - jnp.exp(b) + lam_init


def _stack_halves(q, scale):
    lane = lax.broadcasted_iota(jnp.int32, q.shape, 1)
    qs = q * jnp.asarray(scale, q.dtype)
    zero = jnp.zeros_like(qs)
    return jnp.concatenate([jnp.where(lane < HEAD_DIM, qs, zero), jnp.where(lane >= HEAD_DIM, qs, zero)], axis=0)


def _softmax_step(s, v, m, l, acc):
    m_new = jnp.maximum(m, jnp.max(s, axis=-1, keepdims=True))
    alpha = jnp.exp(m - m_new)
    p = jnp.exp(s - m_new)
    l_new = alpha * l + jnp.sum(p, axis=-1, keepdims=True)
    acc_new = alpha * acc + _dot(p.astype(BF16), v)
    return m_new, l_new, acc_new


def _diff_finish(acc, l, tq, lam, g, lam_init):
    o = acc / l
    o = o[:tq] - lam * o[tq:]
    o = o * lax.rsqrt(jnp.mean(o * o, axis=-1, keepdims=True) + LN_EPS)
    return o * g * (1.0 - lam_init)


def _diffattn_kernel(slopes_ref, lam4_ref, g_ref, q_ref, k_ref, v_ref, o_ref, *, tq, lam_init):
    h = pl.program_id(1)
    qi = pl.program_id(2)
    slope = slopes_ref[h]
    qm = _stack_halves(q_ref[0], HEAD_DIM ** -0.5)
    ii = lax.broadcasted_iota(jnp.int32, (tq, tq), 0)
    jj = lax.broadcasted_iota(jnp.int32, (tq, tq), 1)
    dij = (ii - jj).astype(F32)
    past_bias = -slope * dij

    def past_block(j, carry):
        m, l, acc = carry
        start = pl.multiple_of(j * tq, tq)
        kj = k_ref[0, pl.ds(start, tq), :]
        vj = v_ref[0, pl.ds(start, tq), :]
        s = _dot_nt(qm, kj).reshape(2, tq, tq)
        bias = past_bias - slope * ((qi - j) * tq).astype(F32)
        s = (s + bias[None]).reshape(2 * tq, tq)
        return _softmax_step(s, vj, m, l, acc)

    init = (jnp.full((2 * tq, 1), NEG, F32), jnp.zeros((2 * tq, 1), F32), jnp.zeros((2 * tq, 2 * HEAD_DIM), F32))
    m, l, acc = lax.fori_loop(0, qi, past_block, init)

    start = pl.multiple_of(qi * tq, tq)
    kd = k_ref[0, pl.ds(start, tq), :]
    vd = v_ref[0, pl.ds(start, tq), :]
    visible = (jj // CHUNK) <= (ii // CHUNK)
    diag_bias = jnp.where(visible, -slope * jnp.abs(dij), NEG)
    s = (_dot_nt(qm, kd).reshape(2, tq, tq) + diag_bias[None]).reshape(2 * tq, tq)
    m, l, acc = _softmax_step(s, vd, m, l, acc)

    lam = _lam_value(lam4_ref, lam_init)
    o_ref[0] = _diff_finish(acc, l, tq, lam, g_ref[...], lam_init).astype(o_ref.dtype)


def _diffattn(qa, ka, va, slopes, lam4, subln_g, lam_init, *, tq):
    b, s, _ = qa.shape
    kern = functools.partial(_diffattn_kernel, tq=tq, lam_init=lam_init)
    return pl.pallas_call(
        kern,
        out_shape=jax.ShapeDtypeStruct((b, s, D_A), BF16),
        grid=(b, H_A, s // tq),
        in_specs=[pl.BlockSpec(memory_space=pltpu.SMEM),
                  pl.BlockSpec((4, HEAD_DIM), lambda bi, h, i: (0, 0)),
                  pl.BlockSpec((1, 2 * HEAD_DIM), lambda bi, h, i: (0, 0)),
                  pl.BlockSpec((1, tq, 2 * HEAD_DIM), lambda bi, h, i: (bi, i, h)),
                  pl.BlockSpec((1, s, 2 * HEAD_DIM), lambda bi, h, i: (bi, 0, h)),
                  pl.BlockSpec((1, s, 2 * HEAD_DIM), lambda bi, h, i: (bi, 0, h))],
        out_specs=pl.BlockSpec((1, tq, 2 * HEAD_DIM), lambda bi, h, i: (bi, i, h)),
        compiler_params=_cparams(("parallel", "parallel", "arbitrary")),
        name="diffattn",
    )(slopes, lam4, subln_g.reshape(1, 2 * HEAD_DIM), qa, ka, va)


def _band_window(s, tq):
    return min(BAND_CHUNKS * CHUNK + tq, s)


def _band_bias_tiles(rel_table, s, tq):
    win = _band_window(s, tq)
    n_var = min(BAND_CHUNKS * CHUNK // tq + 1, s // tq)
    tiles = []
    i = np.arange(tq)[:, None]
    j = np.arange(win)[None, :]
    for var in range(n_var):
        q0 = var * tq
        k0 = max(q0 - BAND_CHUNKS * CHUNK, 0)
        t, sp = q0 + i, k0 + j
        valid = (sp // CHUNK <= t // CHUNK) & (sp // CHUNK >= t // CHUNK - BAND_CHUNKS)
        rel = np.clip(t - sp, -REL_CLIP, REL_CLIP) + REL_CLIP
        bias = jnp.take(rel_table, jnp.asarray(rel, jnp.int32), axis=1).astype(F32)
        tiles.append(jnp.where(jnp.asarray(valid)[None], bias, NEG))
    return jnp.stack(tiles)


def _pair_softmax_out(qm, kw, vw, bias2, t):
    s = (_dot_nt(qm, kw).reshape(2, t, -1) + bias2).reshape(2 * t, -1)
    m = jnp.max(s, axis=-1, keepdims=True)
    p = jnp.exp(s - m)
    l = jnp.sum(p, axis=-1, keepdims=True)
    o = _dot(p.astype(BF16), vw) / l
    lane = lax.broadcasted_iota(jnp.int32, (t, 2 * HEAD_DIM), 1)
    return jnp.where(lane < HEAD_DIM, o[:t], o[t:])


def _bandattn_kernel(bias_ref, q_ref, k_ref, v_ref, o_ref, *, tq, win, n_var):
    qi = pl.program_id(2)
    var = jnp.minimum(qi, n_var - 1)
    k0 = pl.multiple_of(jnp.maximum(qi * tq - BAND_CHUNKS * CHUNK, 0), tq)
    qm = _stack_halves(q_ref[0], HEAD_DIM ** -0.5)
    kw = k_ref[0, pl.ds(k0, win), :]
    vw = v_ref[0, pl.ds(k0, win), :]
    o_ref[0] = _pair_softmax_out(qm, kw, vw, bias_ref[var], tq).astype(o_ref.dtype)


def _bandattn(qb, kb, vb, bias_tiles, *, tq):
    b, s, _ = qb.shape
    n_var, _, _, win = bias_tiles.shape
    kern = functools.partial(_bandattn_kernel, tq=tq, win=win, n_var=n_var)
    return pl.pallas_call(
        kern,
        out_shape=jax.ShapeDtypeStruct((b, s, D_B), BF16),
        grid=(H_B // 2, b, s // tq),
        in_specs=[pl.BlockSpec((n_var, 2, tq, win), lambda hp, bi, i: (0, hp, 0, 0)),
                  pl.BlockSpec((1, tq, 2 * HEAD_DIM), lambda hp, bi, i: (bi, i, hp)),
                  pl.BlockSpec((1, s, 2 * HEAD_DIM), lambda hp, bi, i: (bi, 0, hp)),
                  pl.BlockSpec((1, s, 2 * HEAD_DIM), lambda hp, bi, i: (bi, 0, hp))],
        out_specs=pl.BlockSpec((1, tq, 2 * HEAD_DIM), lambda hp, bi, i: (bi, i, hp)),
        compiler_params=_cparams(("parallel", "parallel", "arbitrary")),
        name="bandattn",
    )(bias_tiles, qb, kb, vb)


def _sampattn_kernel(slopes_ref, lam4_ref, g_ref, bbias_ref, qa_ref, ckd_ref, cvd_ref, kan_ref, van_ref,
                     qb_ref, ckb_ref, cvb_ref, kbn_ref, vbn_ref, oa_ref, ob_ref, *, n, past, lam_init):
    lam = _lam_value(lam4_ref, lam_init)
    tq_pos = past + lax.broadcasted_iota(jnp.int32, (n, past), 0)
    ts_pos = lax.broadcasted_iota(jnp.int32, (n, past), 1)
    dist_c = jnp.abs(tq_pos - ts_pos).astype(F32)
    vis_c = (ts_pos // CHUNK) <= (tq_pos // CHUNK)
    ii = lax.broadcasted_iota(jnp.int32, (n, n), 0)
    jj = lax.broadcasted_iota(jnp.int32, (n, n), 1)
    dist_n = jnp.abs(ii - jj).astype(F32)
    vis_n = ((past + jj) // CHUNK) <= ((past + ii) // CHUNK)
    for h in range(H_A):
        sl = slice(h * 2 * HEAD_DIM, (h + 1) * 2 * HEAD_DIM)
        slope = slopes_ref[h]
        qm = _stack_halves(qa_ref[0, :, sl], HEAD_DIM ** -0.5)
        kc = ckd_ref[0, :, sl].astype(BF16)
        vc = cvd_ref[0, :, sl].astype(BF16)
        kn = kan_ref[0, :, sl]
        vn = van_ref[0, :, sl]
        bias_c = jnp.where(vis_c, -slope * dist_c, NEG)
        bias_n = jnp.where(vis_n, -slope * dist_n, NEG)
        s_c = (_dot_nt(qm, kc).reshape(2, n, past) + bias_c[None]).reshape(2 * n, past)
        s_n = (_dot_nt(qm, kn).reshape(2, n, n) + bias_n[None]).reshape(2 * n, n)
        m = jnp.maximum(jnp.max(s_c, axis=-1, keepdims=True), jnp.max(s_n, axis=-1, keepdims=True))
        p_c = jnp.exp(s_c - m)
        p_n = jnp.exp(s_n - m)
        l = jnp.sum(p_c, axis=-1, keepdims=True) + jnp.sum(p_n, axis=-1, keepdims=True)
        acc = _dot(p_c.astype(BF16), vc) + _dot(p_n.astype(BF16), vn)
        oa_ref[0, :, sl] = _diff_finish(acc, l, n, lam, g_ref[...], lam_init).astype(oa_ref.dtype)
    lb = ckb_ref.shape[1]
    for hp in range(H_B // 2):
        sl = slice(hp * 2 * HEAD_DIM, (hp + 1) * 2 * HEAD_DIM)
        qm = _stack_halves(qb_ref[0, :, sl], HEAD_DIM ** -0.5)
        kc = ckb_ref[0, :, sl].astype(BF16)
        vc = cvb_ref[0, :, sl].astype(BF16)
        kn = kbn_ref[0, :, sl]
        vn = vbn_ref[0, :, sl]
        bias = bbias_ref[2 * hp:2 * hp + 2]
        s_c = (_dot_nt(qm, kc).reshape(2, n, lb) + bias[:, :, :lb]).reshape(2 * n, lb)
        s_n = (_dot_nt(qm, kn).reshape(2, n, n) + bias[:, :, lb:]).reshape(2 * n, n)
        m = jnp.maximum(jnp.max(s_c, axis=-1, keepdims=True), jnp.max(s_n, axis=-1, keepdims=True))
        p_c = jnp.exp(s_c - m)
        p_n = jnp.exp(s_n - m)
        l = jnp.sum(p_c, axis=-1, keepdims=True) + jnp.sum(p_n, axis=-1, keepdims=True)
        o = (_dot(p_c.astype(BF16), vc) + _dot(p_n.astype(BF16), vn)) / l
        lane = lax.broadcasted_iota(jnp.int32, (n, 2 * HEAD_DIM), 1)
        ob_ref[0, :, sl] = jnp.where(lane < HEAD_DIM, o[:n], o[n:]).astype(ob_ref.dtype)


def _sample_band_bias(rel_table, n, past, lb):
    tq = past + np.arange(n)[:, None]
    ts = (past - lb) + np.arange(lb + n)[None, :]
    valid = (ts // CHUNK <= tq // CHUNK) & (ts // CHUNK >= tq // CHUNK - BAND_CHUNKS)
    rel = np.clip(tq - ts, -REL_CLIP, REL_CLIP) + REL_CLIP
    bias = jnp.take(rel_table, jnp.asarray(rel, jnp.int32), axis=1).astype(F32)
    return jnp.where(jnp.asarray(valid)[None], bias, NEG)


def _sampattn(qa, ckd, cvd, kan, van, qb, ckb, cvb, kbn, vbn, slopes, lam4, subln_g, bbias, lam_init):
    b, n, _ = qa.shape
    past = ckd.shape[1]
    lb = ckb.shape[1]
    kern = functools.partial(_sampattn_kernel, n=n, past=past, lam_init=lam_init)
    full = lambda shape: pl.BlockSpec(shape, lambda bi: (0,) * len(shape))
    per_b = lambda r, w: pl.BlockSpec((1, r, w), lambda bi: (bi, 0, 0))
    return pl.pallas_call(
        kern,
        out_shape=(jax.ShapeDtypeStruct((b, n, D_A), BF16), jax.ShapeDtypeStruct((b, n, D_B), BF16)),
        grid=(b,),
        in_specs=[pl.BlockSpec(memory_space=pltpu.SMEM), full((4, HEAD_DIM)), full((1, 2 * HEAD_DIM)),
                  full((H_B, n, lb + n)),
                  per_b(n, D_A), per_b(past, D_A), per_b(past, D_A), per_b(n, D_A), per_b(n, D_A),
                  per_b(n, D_B), per_b(lb, D_B), per_b(lb, D_B), per_b(n, D_B), per_b(n, D_B)],
        out_specs=(per_b(n, D_A), per_b(n, D_B)),
        compiler_params=_cparams(("parallel",)),
        name="sampattn",
    )(slopes, lam4, subln_g.reshape(1, 2 * HEAD_DIM), bbias, qa, ckd, cvd, kan, van, qb, ckb, cvb, kbn, vbn)


def _outproj_kernel(oa_ref, ob_ref, sga_ref, sgb_ref, x_ref, g1_ref, sh2_ref, sc2_ref, pa_ref, pb_ref, wo_ref,
                    lng_ref, lnb_ref, x1_ref, h_ref, *, alpha):
    mix = (sga_ref[...].astype(F32) * _dot(oa_ref[...], pa_ref[...])
           + sgb_ref[...].astype(F32) * _dot(ob_ref[...], pb_ref[...]))
    y = _dot(mix.astype(BF16), wo_ref[...])
    x1 = _ln(alpha * x_ref[...] + _rows(g1_ref) * y) * lng_ref[...] + lnb_ref[...]
    x1_ref[...] = x1
    u2 = _ln(x1) * (1.0 + _rows(sc2_ref)) + _rows(sh2_ref)
    h_ref[...] = _pack_pair(u2[:, :PACKED], u2[:, PACKED:])


def _outproj(oa, ob, sga, sgb, x, gate1, shift2, scale2, pa16, pb16, wo16, ln_g, ln_b, *, tm, rows_per_seq,
             per_row, alpha):
    r = x.shape[0]
    row = lambda w: pl.BlockSpec((tm, w), lambda i: (i, 0))
    full = lambda a, b: pl.BlockSpec((a, b), lambda i: (0, 0))
    mod = _mod_spec(per_row, tm, rows_per_seq)
    return pl.pallas_call(
        functools.partial(_outproj_kernel, alpha=alpha),
        out_shape=(jax.ShapeDtypeStruct((r, D_MODEL), F32), jax.ShapeDtypeStruct((r, PACKED), jnp.uint32)),
        grid=(r // tm,),
        in_specs=[row(D_A), row(D_B), row(D_MODEL), row(D_MODEL), row(D_MODEL), mod, mod, mod,
                  full(D_A, D_MODEL), full(D_B, D_MODEL), full(D_MODEL, D_MODEL),
                  full(1, D_MODEL), full(1, D_MODEL)],
        out_specs=(row(D_MODEL), row(PACKED)),
        compiler_params=_cparams(("parallel",)),
        name="outproj",
    )(oa, ob, sga, sgb, x, gate1, shift2, scale2, pa16, pb16, wo16, ln_g.reshape(1, -1), ln_b.reshape(1, -1))


def _first_index_of_max(x, iota, axis, size):
    m = jnp.max(x, axis=axis, keepdims=True)
    idx = jnp.min(jnp.where(x == m, iota, size), axis=axis, keepdims=True)
    return m, idx


def _router_kernel(h_ref, wr_hi_ref, wr_lo_ref, eb_ref, idx_ref, wt_ref, rank_ref, cnt_ref, run_ref, *, tr):
    step = pl.program_id(0)

    @pl.when(step == 0)
    def _():
        run_ref[...] = jnp.zeros_like(run_ref)

    ha, hb = _unpack_pair(h_ref[...])
    wr_hi, wr_lo = wr_hi_ref[...], wr_lo_ref[...]
    logits = (_dot_nt(wr_hi[:, :PACKED], ha) + _dot_nt(wr_hi[:, PACKED:], hb)
              + _dot_nt(wr_lo[:, :PACKED], ha) + _dot_nt(wr_lo[:, PACKED:], hb))
    scores = jax.nn.sigmoid(logits)
    biased = scores + eb_ref[...]

    x3 = biased.reshape(N_GROUP, GROUP_SIZE, tr)
    io3 = lax.broadcasted_iota(jnp.int32, x3.shape, 1)
    m1, i1 = _first_index_of_max(x3, io3, 1, GROUP_SIZE)
    m2 = jnp.max(jnp.where(io3 == i1, -jnp.inf, x3), axis=1, keepdims=True)
    grp = (m1 + m2).reshape(N_GROUP, tr)

    iog = lax.broadcasted_iota(jnp.int32, grp.shape, 0)
    gsel = jnp.zeros(grp.shape, jnp.bool_)
    for _ in range(TOPK_GROUP):
        _, gi = _first_index_of_max(grp, iog, 0, N_GROUP)
        hit = iog == gi
        gsel = gsel | hit
        grp = jnp.where(hit, -jnp.inf, grp)
    emask = jnp.broadcast_to(gsel.reshape(N_GROUP, 1, tr), (N_GROUP, GROUP_SIZE, tr)).reshape(N_EXPERTS, tr)
    cand = jnp.where(emask, biased, -jnp.inf)

    ioe = lax.broadcasted_iota(jnp.int32, cand.shape, 0)
    hits = []
    sel = jnp.zeros(cand.shape, jnp.bool_)
    for _ in range(TOP_K):
        _, ei = _first_index_of_max(cand, ioe, 0, N_EXPERTS)
        hit = ioe == ei
        hits.append((ei, hit))
        sel = sel | hit
        cand = jnp.where(hit, -jnp.inf, cand)
    self32 = sel.astype(F32)
    ra = lax.broadcasted_iota(jnp.int32, (tr, tr), 0)
    rb = lax.broadcasted_iota(jnp.int32, (tr, tr), 1)
    upper = (ra < rb).astype(BF16)
    ranks = run_ref[...] + _dot(self32.astype(BF16), upper)

    ws = [jnp.sum(jnp.where(hit, scores, 0.0), axis=0, keepdims=True) for _, hit in hits]
    wsum = ws[0]
    for w in ws[1:]:
        wsum = wsum + w
    for k, (ei, hit) in enumerate(hits):
        idx_ref[k:k + 1, :] = ei
        wt_ref[k:k + 1, :] = ws[k] / wsum * ROUTED_SCALE
        rank_ref[k:k + 1, :] = jnp.sum(jnp.where(hit, ranks, 0.0), axis=0, keepdims=True).astype(jnp.int32)
    run_new = run_ref[...] + jnp.sum(self32, axis=1, keepdims=True)
    run_ref[...] = run_new
    cnt_ref[...] = jnp.broadcast_to(run_new, cnt_ref.shape).astype(jnp.int32)


def _router(h_all, w_router, e_bias, *, tr):
    t = h_all.shape[0]
    wr_t = w_router.T
    wr_hi = wr_t.astype(BF16)
    wr_lo = (wr_t - wr_hi.astype(F32)).astype(BF16)
    full = lambda a, b: pl.BlockSpec((a, b), lambda i: (0, 0))
    col = lambda: pl.BlockSpec((TOP_K, tr), lambda i: (0, i))
    return pl.pallas_call(
        functools.partial(_router_kernel, tr=tr),
        out_shape=(jax.ShapeDtypeStruct((TOP_K, t), jnp.int32), jax.ShapeDtypeStruct((TOP_K, t), F32),
                   jax.ShapeDtypeStruct((TOP_K, t), jnp.int32), jax.ShapeDtypeStruct((N_EXPERTS, 128), jnp.int32)),
        grid=(t // tr,),
        in_specs=[pl.BlockSpec((tr, PACKED), lambda i: (i, 0)), full(N_EXPERTS, D_MODEL),
                  full(N_EXPERTS, D_MODEL), full(N_EXPERTS, 1)],
        out_specs=(col(), col(), col(), full(N_EXPERTS, 128)),
        scratch_shapes=[pltpu.VMEM((N_EXPERTS, 1), F32)],
        compiler_params=_cparams(("arbitrary",)),
        name="router",
    )(h_all, wr_hi, wr_lo, e_bias.reshape(N_EXPERTS, 1).astype(F32))


def _dest_kernel(ps_ref, idx_ref, rank_ref, dest_ref):
    idx = idx_ref[...]
    base = jnp.zeros(idx.shape, jnp.int32)
    for e in range(N_EXPERTS):
        base = jnp.where(idx == e, ps_ref[e], base)
    dest_ref[...] = base + rank_ref[...]


def _dest(pad_start, idx_t, rank_t, *, td):
    t = idx_t.shape[1]
    col = lambda: pl.BlockSpec((TOP_K, td), lambda i: (0, i))
    return pl.pallas_call(
        _dest_kernel,
        out_shape=jax.ShapeDtypeStruct((TOP_K, t), jnp.int32),
        grid=(t // td,),
        in_specs=[pl.BlockSpec(memory_space=pltpu.SMEM), col(), col()],
        out_specs=col(),
        compiler_params=_cparams(("parallel",)),
        name="dest",
    )(pad_start, idx_t, rank_t)


def _row_copy(src_ref, src_row, dst_ref, dst_row, sem):
    return pltpu.make_async_copy(src_ref.at[pl.ds(src_row, 1)], dst_ref.at[pl.ds(dst_row, 1)], sem)


def _dispatch_kernel(fill_lo_ref, fill_hi_ref, dest_hbm, h_ref, xs_hbm, dest_smem, zero_ref, dsem, rsem, *, tm):
    i = pl.program_id(0)
    n = pl.num_programs(0)
    slot = i % 2

    def dest_copy(tile, s):
        return pltpu.make_async_copy(dest_hbm.at[tile], dest_smem.at[s], dsem.at[s])

    @pl.when(i == 0)
    def _():
        dest_copy(0, 0).start()
        zero_ref[...] = jnp.zeros_like(zero_ref)
        for e in range(N_EXPERTS):
            lo, hi = fill_lo_ref[e], fill_hi_ref[e]

            def fill(r, c):
                _row_copy(zero_ref, 0, xs_hbm, r, rsem).start()
                return c

            def drain(r, c):
                _row_copy(zero_ref, 0, xs_hbm, r, rsem).wait()
                return c

            lax.fori_loop(lo, hi, fill, 0)
            lax.fori_loop(lo, hi, drain, 0)

        first_free = fill_hi_ref[N_EXPERTS - 1] // zero_ref.shape[0]

        def tail_copy(b):
            start = pl.multiple_of(b * zero_ref.shape[0], zero_ref.shape[0])
            return pltpu.make_async_copy(zero_ref, xs_hbm.at[pl.ds(start, zero_ref.shape[0])], rsem)

        def fill_tail(b, c):
            tail_copy(b).start()
            return c

        def drain_tail(b, c):
            tail_copy(b).wait()
            return c

        lax.fori_loop(first_free, xs_hbm.shape[0] // zero_ref.shape[0], fill_tail, 0)
        lax.fori_loop(first_free, xs_hbm.shape[0] // zero_ref.shape[0], drain_tail, 0)

    dest_copy(i, slot).wait()

    @pl.when(i + 1 < n)
    def _():
        dest_copy(i + 1, 1 - slot).start()

    def issue(t, c):
        for k in range(TOP_K):
            _row_copy(h_ref, t, xs_hbm, dest_smem[slot, k, t], rsem).start()
        return c

    lax.fori_loop(0, tm, issue, 0)
    for _ in range(TOP_K):
        pltpu.make_async_copy(h_ref, xs_hbm.at[pl.ds(0, tm)], rsem).wait()


def _dispatch(h_all, dest3, fill_lo, fill_hi, n_slots, *, tm):
    t = h_all.shape[0]
    return pl.pallas_call(
        functools.partial(_dispatch_kernel, tm=tm),
        out_shape=jax.ShapeDtypeStruct((n_slots, PACKED), jnp.uint32),
        grid_spec=pltpu.PrefetchScalarGridSpec(
            num_scalar_prefetch=2,
            grid=(t // tm,),
            in_specs=[pl.BlockSpec(memory_space=pl.ANY),
                      pl.BlockSpec((tm, PACKED), lambda i, lo, hi: (i, 0))],
            out_specs=pl.BlockSpec(memory_space=pl.ANY),
            scratch_shapes=[pltpu.SMEM((2, TOP_K, tm), jnp.int32), pltpu.VMEM((EXPERT_BLOCK, PACKED), jnp.uint32),
                            pltpu.SemaphoreType.DMA((2,)), pltpu.SemaphoreType.DMA(())],
        ),
        compiler_params=_cparams(("arbitrary",)),
        name="dispatch",
    )(fill_lo, fill_hi, dest3, h_all)


def _experts_kernel(be_ref, nu_ref, x_ref, w13_ref, w2_ref, y_ref):
    @pl.when(pl.program_id(0) < nu_ref[0])
    def _():
        a = _dot_packed(x_ref[...], w13_ref[0])
        hid = (a[:, :D_EXPERT] * jax.nn.sigmoid(a[:, :D_EXPERT])) * a[:, D_EXPERT:]
        y = _dot(hid.astype(BF16), w2_ref[0])
        y_ref[...] = _pack_pair(y[:, :PACKED], y[:, PACKED:])

    @pl.when(pl.program_id(0) >= nu_ref[0])
    def _():
        y_ref[...] = jnp.zeros_like(y_ref)


def _experts(xs, w13, w2, block_e, n_used):
    n_slots = xs.shape[0]
    nb = n_slots // EXPERT_BLOCK
    last = lambda i, nu: jnp.minimum(i, nu[0] - 1)
    return pl.pallas_call(
        _experts_kernel,
        out_shape=jax.ShapeDtypeStruct((n_slots, PACKED), jnp.uint32),
        grid_spec=pltpu.PrefetchScalarGridSpec(
            num_scalar_prefetch=2,
            grid=(nb,),
            in_specs=[pl.BlockSpec((EXPERT_BLOCK, PACKED), lambda i, be, nu: (last(i, nu), 0)),
                      pl.BlockSpec((1, D_MODEL, 2 * D_EXPERT), lambda i, be, nu: (be[last(i, nu)], 0, 0)),
                      pl.BlockSpec((1, D_EXPERT, D_MODEL), lambda i, be, nu: (be[last(i, nu)], 0, 0))],
            out_specs=pl.BlockSpec((EXPERT_BLOCK, PACKED), lambda i, be, nu: (i, 0)),
        ),
        compiler_params=_cparams(("arbitrary",)),
        name="experts",
    )(block_e, n_used, xs, w13, w2)


def _final_kernel(dest_hbm, yb_hbm, h_ref, x1_ref, wt_ref, g2_ref, ws13_ref, ws2_ref, lng_ref, lnb_ref, o_ref,
                  dest_smem, ybuf, dsem, rsem, *, alpha, tm, off):
    i = pl.program_id(0)
    n = pl.num_programs(0)
    slot = i % 2

    def gather_tile(tile, s):
        cp = pltpu.make_async_copy(dest_hbm.at[tile + off], dest_smem, dsem)
        cp.start()
        cp.wait()

        def issue(t, c):
            for k in range(TOP_K):
                _row_copy(yb_hbm, dest_smem[k, t], ybuf.at[s, k], t, rsem.at[s]).start()
            return c

        lax.fori_loop(0, tm, issue, 0)

    @pl.when(i == 0)
    def _():
        gather_tile(0, 0)

    @pl.when(i + 1 < n)
    def _():
        gather_tile(i + 1, 1 - slot)

    a = _dot_packed(h_ref[...], ws13_ref[...])
    hid = (a[:, :D_EXPERT] * jax.nn.sigmoid(a[:, :D_EXPERT])) * a[:, D_EXPERT:]
    y = _dot(hid.astype(BF16), ws2_ref[...])

    for k in range(TOP_K):
        pltpu.make_async_copy(yb_hbm.at[pl.ds(0, tm)], ybuf.at[slot, k], rsem.at[slot]).wait()
    wt = wt_ref[...]
    ya = jnp.zeros((tm, PACKED), F32)
    yb = jnp.zeros((tm, PACKED), F32)
    for k in range(TOP_K):
        ea, eb = _unpack_pair(ybuf[slot, k])
        ya = ya + wt[:, k:k + 1] * ea.astype(F32)
        yb = yb + wt[:, k:k + 1] * eb.astype(F32)
    y = y + jnp.concatenate([ya, yb], axis=1)
    o_ref[...] = _ln(alpha * x1_ref[...] + _rows(g2_ref) * y) * lng_ref[...] + lnb_ref[...]


def _final(dest3, yb, h_all, x1, wt, gate2, ws13, ws2, ln_g, ln_b, *, tm, row0, rows_per_seq, per_row, alpha):
    r = x1.shape[0]
    off = row0 // tm
    row = lambda w: pl.BlockSpec((tm, w), lambda i: (i, 0))
    row_off = lambda w: pl.BlockSpec((tm, w), lambda i: (i + off, 0))
    full = lambda a, b: pl.BlockSpec((a, b), lambda i: (0, 0))
    any_spec = pl.BlockSpec(memory_space=pl.ANY)
    return pl.pallas_call(
        functools.partial(_final_kernel, alpha=alpha, tm=tm, off=off),
        out_shape=jax.ShapeDtypeStruct((r, D_MODEL), F32),
        grid=(r // tm,),
        in_specs=[any_spec, any_spec, row_off(PACKED), row(D_MODEL), row_off(TOP_K),
                  _mod_spec(per_row, tm, rows_per_seq),
                  full(D_MODEL, 2 * D_EXPERT), full(D_EXPERT, D_MODEL), full(1, D_MODEL), full(1, D_MODEL)],
        out_specs=row(D_MODEL),
        scratch_shapes=[pltpu.SMEM((TOP_K, tm), jnp.int32), pltpu.VMEM((2, TOP_K, tm, PACKED), jnp.uint32),
                        pltpu.SemaphoreType.DMA(()), pltpu.SemaphoreType.DMA((2,))],
        compiler_params=_cparams(("arbitrary",)),
        name="final",
    )(dest3, yb, h_all, x1, wt, gate2, ws13, ws2, ln_g.reshape(1, -1), ln_b.reshape(1, -1))


def _pick_tile(n, pref):
    t = min(pref, n)
    while n % t:
        t //= 2
    return t


def kernel(x_prompt, x_sample, cache_k_diff, cache_v_diff, cache_k_band, cache_v_band, c_prompt, c_sample,
           w_ada, b_ada, w_in, b_in, lambda_q1, lambda_k1, lambda_q2, lambda_k2, subln_g, rel_bias, p_a, p_b,
           w_out, ln1_g, ln1_b, w_router, e_bias, w1, w3, w2, ws1, ws3, ws2, ln2_g, ln2_b):
    depth = w_in.shape[0]
    alpha = (2 * depth) ** 0.25
    bp, sp, d = x_prompt.shape
    bs, ss, _ = x_sample.shape
    past = cache_k_diff.shape[2]
    lb = cache_k_band.shape[2]
    tp, ts_ = bp * sp, bs * ss
    t_all = tp + ts_
    slopes = jnp.asarray(2.0 ** (-8.0 * np.arange(1, H_A + 1) / H_A), F32)

    yp = x_prompt.reshape(tp, d)
    ys = x_sample.reshape(ts_, d)
    outs = [[] for _ in range(8)]
    tail_p = min(BAND_CHUNKS * CHUNK, sp)
    tm_p = _pick_tile(math.gcd(sp, tail_p), 256)
    tq = _pick_tile(sp, 256)

    for l in range(depth):
        lam_init = 0.8 - 0.6 * math.exp(-0.3 * l)
        lam4 = jnp.stack([lambda_q1[l], lambda_k1[l], lambda_q2[l], lambda_k2[l]]).astype(F32)
        w_in16 = w_in[l].astype(BF16)
        b_in2 = b_in[l].reshape(1, N_IN)
        pa16, pb16, wo16 = p_a[l].astype(BF16), p_b[l].astype(BF16), w_out[l].astype(BF16)

        mod = _ada(jnp.concatenate([c_prompt, c_sample], axis=0), w_ada[l], b_ada[l])
        mod_p = [m.reshape(bp, 1, d) for m in jnp.split(mod[:bp], 6, axis=-1)]
        mod_s = [jnp.repeat(m, ss, axis=0) for m in jnp.split(mod[bp:], 6, axis=-1)]

        (qa, ka32, va32, ka16, va16, qb, kb16, vb16, sga, sgb, kbt, vbt) = _inproj(
            yp, mod_p[0], mod_p[1], w_in16, b_in2, tm=tm_p, rows_per_seq=sp, tail_rows=tail_p, per_row=False)
        r3 = lambda a: a.reshape(bp, sp, a.shape[-1])
        oa = _diffattn(r3(qa), r3(ka16), r3(va16), slopes, lam4, subln_g[l], lam_init, tq=tq)
        ob = _bandattn(r3(qb), r3(kb16), r3(vb16), _band_bias_tiles(rel_bias[l], sp, tq), tq=tq)
        x1p, hp = _outproj(oa.reshape(tp, D_A), ob.reshape(tp, D_B), sga, sgb, yp, mod_p[2], mod_p[3], mod_p[4],
                           pa16, pb16, wo16, ln1_g[l], ln1_b[l], tm=_pick_tile(sp, 512), rows_per_seq=sp,
                           per_row=False, alpha=alpha)
        outs[0].append(ka32.reshape(bp, sp, H_A, 2, HEAD_DIM))
        outs[1].append(va32.reshape(bp, sp, H_A, 2 * HEAD_DIM))
        outs[2].append(kbt.reshape(bp, tail_p, H_B, HEAD_DIM))
        outs[3].append(vbt.reshape(bp, tail_p, H_B, HEAD_DIM))

        (qa_s, ka32_s, va32_s, ka16_s, va16_s, qb_s, kb16_s, vb16_s, sga_s, sgb_s, kbt_s, vbt_s) = _inproj(
            ys, mod_s[0], mod_s[1], w_in16, b_in2, tm=ts_, rows_per_seq=ts_, tail_rows=ts_, per_row=True)
        s3 = lambda a: a.reshape(bs, ss, a.shape[-1])
        oa_s, ob_s = _sampattn(
            s3(qa_s), cache_k_diff[l].reshape(bs, past, D_A), cache_v_diff[l].reshape(bs, past, D_A),
            s3(ka16_s), s3(va16_s), s3(qb_s), cache_k_band[l].reshape(bs, lb, D_B),
            cache_v_band[l].reshape(bs, lb, D_B), s3(kb16_s), s3(vb16_s), slopes, lam4, subln_g[l],
            _sample_band_bias(rel_bias[l], ss, past, lb), lam_init)
        x1s, hs = _outproj(oa_s.reshape(ts_, D_A), ob_s.reshape(ts_, D_B), sga_s, sgb_s, ys, mod_s[2], mod_s[3],
                           mod_s[4], pa16, pb16, wo16, ln1_g[l], ln1_b[l], tm=ts_, rows_per_seq=ts_, per_row=True,
                           alpha=alpha)
        outs[4].append(ka32_s.reshape(bs, ss, H_A, 2, HEAD_DIM))
        outs[5].append(va32_s.reshape(bs, ss, H_A, 2 * HEAD_DIM))
        outs[6].append(kbt_s.reshape(bs, ss, H_B, HEAD_DIM))
        outs[7].append(vbt_s.reshape(bs, ss, H_B, HEAD_DIM))

        h_all = jnp.concatenate([hp, hs], axis=0)
        tr = _pick_tile(t_all, 512)
        idx_t, wt_t, rank_t, cnt = _router(h_all, w_router[l], e_bias[l], tr=tr)
        counts = cnt[:, 0]
        padded = (counts + EXPERT_BLOCK - 1) // EXPERT_BLOCK * EXPERT_BLOCK
        pad_end = jnp.cumsum(padded)
        pad_start = pad_end - padded
        dest = _dest(pad_start.astype(jnp.int32), idx_t, rank_t, td=_pick_tile(t_all, 2048))
        nb = -(-(t_all * TOP_K) // EXPERT_BLOCK) + N_EXPERTS
        n_slots = nb * EXPERT_BLOCK
        block_starts = jnp.arange(nb, dtype=jnp.int32) * EXPERT_BLOCK
        block_e = jnp.minimum(jnp.sum(pad_end[None, :] <= block_starts[:, None], axis=1),
                              N_EXPERTS - 1).astype(jnp.int32)
        n_used = (pad_end[-1:] // EXPERT_BLOCK).astype(jnp.int32)
        tiled = lambda tm: dest.reshape(TOP_K, t_all // tm, tm).transpose(1, 0, 2)
        tm_d = _pick_tile(math.gcd(sp, t_all), 256)
        xs = _dispatch(h_all, tiled(tm_d), (pad_start + counts).astype(jnp.int32), pad_end.astype(jnp.int32),
                       n_slots, tm=tm_d)
        w13 = jnp.concatenate([w1[l], w3[l]], axis=-1).astype(BF16)
        yb = _experts(xs, w13, w2[l].astype(BF16), block_e, n_used)
        wt = wt_t.T
        ws13 = jnp.concatenate([ws1[l], ws3[l]], axis=-1).astype(BF16)
        ws2b = ws2[l].astype(BF16)
        yp = _final(tiled(tm_d), yb, h_all, x1p, wt, mod_p[5], ws13, ws2b, ln2_g[l], ln2_b[l], tm=tm_d, row0=0,
                    rows_per_seq=sp, per_row=False, alpha=alpha)
        ys = _final(tiled(ts_), yb, h_all, x1s, wt, mod_s[5], ws13, ws2b, ln2_g[l], ln2_b[l], tm=ts_, row0=tp,
                    rows_per_seq=ts_, per_row=True, alpha=alpha)

    return (yp.reshape(bp, sp, d), ys.reshape(bs, ss, d)) + tuple(jnp.stack(o) for o in outs)
```

```python
import functools
import math

import jax
import jax.numpy as jnp
import numpy as np
from jax import lax
from jax.experimental import pallas as pl
from jax.experimental.pallas import tpu as pltpu

F32 = jnp.float32
BF16 = jnp.bfloat16

D_MODEL = 1024
CHUNK = 64
HEAD_DIM = 64
H_A = 8
H_B = 8
D_A = H_A * 2 * HEAD_DIM
D_B = H_B * HEAD_DIM
BAND_CHUNKS = 8
REL_CLIP = 128
N_IN = 3 * D_A + 3 * D_B + 2 * D_MODEL
N_EXPERTS = 64
TOP_K = 8
N_GROUP = 8
TOPK_GROUP = 4
GROUP_SIZE = N_EXPERTS // N_GROUP
D_EXPERT = 256
ROUTED_SCALE = 2.5
EXPERT_BLOCK = 512
LN_EPS = 1e-5
NEG = -1e30
LOG2E = math.log2(math.e)
Q_SCALE = HEAD_DIM ** -0.5 * LOG2E

VMEM_LIMIT = 56 * 1024 * 1024


def _cparams(sem):
    return pltpu.CompilerParams(dimension_semantics=sem, vmem_limit_bytes=VMEM_LIMIT)


def _ln(x):
    mu = jnp.mean(x, axis=-1, keepdims=True)
    xc = x - mu
    var = jnp.mean(xc * xc, axis=-1, keepdims=True)
    return xc * lax.rsqrt(var + LN_EPS)


def _rows(ref):
    v = ref[...]
    return v.reshape(v.shape[-2], v.shape[-1])


def _split_bf16(x):
    hi = x.astype(BF16)
    lo = (x - hi.astype(F32)).astype(BF16)
    return hi, lo


def _dot(a, b):
    return jnp.dot(a, b, preferred_element_type=F32)


def _dot_nt(a, b):
    return lax.dot_general(a, b, (((1,), (1,)), ((), ())), preferred_element_type=F32)


PACKED = D_MODEL // 2


def _pack_pair(a, b):
    ua = lax.bitcast_convert_type(a.astype(BF16).astype(F32), jnp.uint32)
    ub = lax.bitcast_convert_type(b.astype(BF16).astype(F32), jnp.uint32)
    return ua | (ub >> 16)


def _unpack_pair(u):
    a = lax.bitcast_convert_type(u & jnp.uint32(0xFFFF0000), F32).astype(BF16)
    b = lax.bitcast_convert_type(u << 16, F32).astype(BF16)
    return a, b


def _dot_packed(u, w):
    a, b = _unpack_pair(u)
    return _dot(a, w[:PACKED]) + _dot(b, w[PACKED:])


def _ada_kernel(c_ref, w_ref, b_ref, o_ref):
    c = c_ref[...]
    a = c * jax.nn.sigmoid(c)
    a_hi, a_lo = _split_bf16(a)
    w_hi, w_lo = _split_bf16(w_ref[...])
    o_ref[...] = _dot(a_hi, w_hi) + _dot(a_lo, w_hi) + _dot(a_hi, w_lo) + b_ref[...]


def _ada(c, w_ada, b_ada):
    n, d = c.shape
    nout = w_ada.shape[1]
    tn = 1024
    return pl.pallas_call(
        _ada_kernel,
        out_shape=jax.ShapeDtypeStruct((n, nout), F32),
        grid=(nout // tn,),
        in_specs=[pl.BlockSpec((n, d), lambda j: (0, 0)),
                  pl.BlockSpec((d, tn), lambda j: (0, j)),
                  pl.BlockSpec((1, tn), lambda j: (0, j))],
        out_specs=pl.BlockSpec((n, tn), lambda j: (0, j)),
        compiler_params=_cparams(("parallel",)),
        name="ada",
    )(c, w_ada, b_ada.reshape(1, nout))


_SEG_QA = (0, D_A)
_SEG_KA = (D_A, 2 * D_A)
_SEG_VA = (2 * D_A, 3 * D_A)
_SEG_QB = (3 * D_A, 3 * D_A + D_B)
_SEG_KB = (3 * D_A + D_B, 3 * D_A + 2 * D_B)
_SEG_VB = (3 * D_A + 2 * D_B, 3 * D_A + 3 * D_B)
_SEG_GA = (3 * D_A + 3 * D_B, 3 * D_A + 3 * D_B + D_MODEL)
_SEG_GB = (3 * D_A + 3 * D_B + D_MODEL, N_IN)


def _inproj_kernel(x_ref, sh_ref, sc_ref, w_ref, b_ref,
                   qa_ref, ka32_ref, va32_ref, ka16_ref, va16_ref,
                   qb_ref, kb16_ref, vb16_ref, sga_ref, sgb_ref, kbt_ref, vbt_ref):
    u = (_ln(x_ref[...]) * (1.0 + _rows(sc_ref)) + _rows(sh_ref)).astype(BF16)

    def seg(lo_hi):
        lo, hi = lo_hi
        return _dot(u, w_ref[:, lo:hi]) + b_ref[:, lo:hi]

    qa_ref[...] = (seg(_SEG_QA) * Q_SCALE).astype(BF16)
    ka = seg(_SEG_KA)
    ka32_ref[...] = ka
    ka16_ref[...] = ka.astype(BF16)
    va = seg(_SEG_VA)
    va32_ref[...] = va
    va16_ref[...] = va.astype(BF16)
    qb_ref[...] = (seg(_SEG_QB) * Q_SCALE).astype(BF16)
    kb = seg(_SEG_KB)
    kbt_ref[...] = kb
    kb16_ref[...] = kb.astype(BF16)
    vb = seg(_SEG_VB)
    vbt_ref[...] = vb
    vb16_ref[...] = vb.astype(BF16)
    sga_ref[...] = jax.nn.sigmoid(seg(_SEG_GA)).astype(BF16)
    sgb_ref[...] = jax.nn.sigmoid(seg(_SEG_GB)).astype(BF16)


def _mod_spec(per_row, tm, rows_per_seq):
    if per_row:
        return pl.BlockSpec((tm, D_MODEL), lambda i: (i, 0))
    tiles_per_seq = rows_per_seq // tm
    return pl.BlockSpec((1, 1, D_MODEL), lambda i: (i // tiles_per_seq, 0, 0))


def _inproj(x, shift, scale, w_in16, b_in, *, tm, rows_per_seq, tail_rows, per_row):
    r = x.shape[0]
    nseq = r // rows_per_seq
    n_i = rows_per_seq // tm
    n_t = tail_rows // tm

    def tail_map(i):
        return ((i // n_i) * n_t + jnp.maximum(i % n_i - (n_i - n_t), 0), 0)

    row = lambda w: pl.BlockSpec((tm, w), lambda i: (i, 0))
    out_shape = (
        jax.ShapeDtypeStruct((r, D_A), BF16),
        jax.ShapeDtypeStruct((r, D_A), F32),
        jax.ShapeDtypeStruct((r, D_A), F32),
        jax.ShapeDtypeStruct((r, D_A), BF16),
        jax.ShapeDtypeStruct((r, D_A), BF16),
        jax.ShapeDtypeStruct((r, D_B), BF16),
        jax.ShapeDtypeStruct((r, D_B), BF16),
        jax.ShapeDtypeStruct((r, D_B), BF16),
        jax.ShapeDtypeStruct((r, D_MODEL), BF16),
        jax.ShapeDtypeStruct((r, D_MODEL), BF16),
        jax.ShapeDtypeStruct((nseq * tail_rows, D_B), F32),
        jax.ShapeDtypeStruct((nseq * tail_rows, D_B), F32),
    )
    out_specs = (row(D_A), row(D_A), row(D_A), row(D_A), row(D_A),
                 row(D_B), row(D_B), row(D_B), row(D_MODEL), row(D_MODEL),
                 pl.BlockSpec((tm, D_B), tail_map), pl.BlockSpec((tm, D_B), tail_map))
    return pl.pallas_call(
        _inproj_kernel,
        out_shape=out_shape,
        grid=(r // tm,),
        in_specs=[row(D_MODEL), _mod_spec(per_row, tm, rows_per_seq), _mod_spec(per_row, tm, rows_per_seq),
                  pl.BlockSpec((D_MODEL, N_IN), lambda i: (0, 0)),
                  pl.BlockSpec((1, N_IN), lambda i: (0, 0))],
        out_specs=out_specs,
        compiler_params=_cparams(("arbitrary",)),
        name="inproj",
    )(x, shift, scale, w_in16, b_in)


N_AUG = 3


def _lam_value(lam4_ref, lam_init):
    v = lam4_ref[...]
    a = jnp.sum(v[0:1, :] * v[1:2, :], axis=-1, keepdims=True)
    b = jnp.sum(v[2:3, :] * v[3:4, :], axis=-1, keepdims=True)
    return jnp.exp(a) - jnp.exp(b) + lam_init


def _half_masks(shape):
    lane = lax.broadcasted_iota(jnp.int32, shape, 1)
    return lane < HEAD_DIM, lane >= HEAD_DIM


def _stack_halves(q):
    lo, hi = _half_masks(q.shape)
    zero = jnp.zeros_like(q)
    return jnp.concatenate([jnp.where(lo, q, zero), jnp.where(hi, q, zero)], axis=0)


def _softmax_step(s, v, m, l, acc):
    m_new = jnp.maximum(m, jnp.max(s, axis=-1, keepdims=True))
    alpha = jnp.exp2(m - m_new)
    p = jnp.exp2(s - m_new)
    l_new = alpha * l + jnp.sum(p, axis=-1, keepdims=True)
    acc_new = alpha * acc + _dot(p.astype(BF16), v)
    return m_new, l_new, acc_new


def _diff_finish(o0, o1, lam, g, lam_init):
    o = o0 - lam * o1
    o = o * lax.rsqrt(jnp.mean(o * o, axis=-1, keepdims=True) + LN_EPS)
    return o * g * (1.0 - lam_init)


def _key_bias_columns(slopes, s):
    x = (np.float32(LOG2E) * slopes.astype(np.float32))[:, None] * np.arange(s, dtype=np.float32)[None, :]
    cols = np.zeros(x.shape + (2 * HEAD_DIM,), np.float32)
    for c in range(N_AUG):
        cols[:, :, c] = x.astype(BF16).astype(np.float32)
        x = x - cols[:, :, c]
    return jnp.asarray(cols, BF16)


def _diffattn_kernel(cs_ref, lam4_ref, g_ref, q_ref, k_ref, ka_ref, v_ref, o_ref, *, t, lam_init):
    h = pl.program_id(1)
    qi = pl.program_id(2)
    cs = cs_ref[h]
    q = q_ref[0]
    lo, hi = _half_masks(q.shape)
    lane = lax.broadcasted_iota(jnp.int32, q.shape, 1)
    zero = jnp.zeros_like(q)
    ones = (lane < N_AUG).astype(BF16)
    q_ext = (jnp.concatenate([jnp.where(lo, q, zero), ones], axis=1),
             jnp.concatenate([jnp.where(hi, q, zero), ones], axis=1))

    def block(j, carry, extra):
        start = pl.multiple_of(j * t, t)
        k_ext = jnp.concatenate([k_ref[0, pl.ds(start, t), :], ka_ref[0, pl.ds(start, t), :]], axis=1)
        vj = v_ref[0, pl.ds(start, t), :]
        out = []
        for half in range(2):
            s = _dot_nt(q_ext[half], k_ext)
            if extra is not None:
                s = s + extra
            out.append(_softmax_step(s, vj, *carry[half]))
        return tuple(out)

    init1 = (jnp.full((t, 1), NEG, F32), jnp.zeros((t, 1), F32), jnp.zeros((t, 2 * HEAD_DIM), F32))
    carry = lax.fori_loop(0, qi, lambda j, c: block(j, c, None), (init1, init1))

    ii = lax.broadcasted_iota(jnp.int32, (t, t), 0)
    jj = lax.broadcasted_iota(jnp.int32, (t, t), 1)
    fix = jnp.where(jj > ii, (2.0 * cs) * (ii - jj).astype(F32), 0.0)
    fix = jnp.where((jj // CHUNK) <= (ii // CHUNK), fix, NEG)
    (m0, l0, a0), (m1, l1, a1) = block(qi, carry, fix)

    lam = _lam_value(lam4_ref, lam_init)
    o_ref[0] = _diff_finish(a0 / l0, a1 / l1, lam, g_ref[...], lam_init).astype(o_ref.dtype)


def _diffattn(qa, ka, ka_aug, va, cslopes, lam4, subln_g, lam_init, *, t):
    b, s, _ = qa.shape
    kern = functools.partial(_diffattn_kernel, t=t, lam_init=lam_init)
    hd2 = 2 * HEAD_DIM
    return pl.pallas_call(
        kern,
        out_shape=jax.ShapeDtypeStruct((b, s, D_A), BF16),
        grid=(b, H_A, s // t),
        in_specs=[pl.BlockSpec(memory_space=pltpu.SMEM),
                  pl.BlockSpec((4, HEAD_DIM), lambda bi, h, i: (0, 0)),
                  pl.BlockSpec((1, hd2), lambda bi, h, i: (0, 0)),
                  pl.BlockSpec((1, t, hd2), lambda bi, h, i: (bi, i, h)),
                  pl.BlockSpec((1, s, hd2), lambda bi, h, i: (bi, 0, h)),
                  pl.BlockSpec((1, s, hd2), lambda bi, h, i: (h, 0, 0)),
                  pl.BlockSpec((1, s, hd2), lambda bi, h, i: (bi, 0, h))],
        out_specs=pl.BlockSpec((1, t, hd2), lambda bi, h, i: (bi, i, h)),
        compiler_params=_cparams(("parallel", "parallel", "arbitrary")),
        name="diffattn",
    )(cslopes, lam4, subln_g.reshape(1, hd2), qa, ka, ka_aug, va)


def _toeplitz_bias(rel_table, rows, cols, delta):
    length = rows + cols - 1
    rel = np.clip(delta + rows - 1 - np.arange(length), -REL_CLIP, REL_CLIP) + REL_CLIP
    u = jnp.take(rel_table.astype(F32) * LOG2E, jnp.asarray(rel, jnp.int32), axis=1)
    u = jnp.pad(u, ((0, 0), (0, 1)))
    r = jnp.tile(u, (1, rows))[:, :rows * length].reshape(-1, rows, length)
    return r[:, :, rows - 1:rows - 1 + cols]


def _band_valid(t_pos, s_pos):
    return (s_pos // CHUNK <= t_pos // CHUNK) & (s_pos // CHUNK >= t_pos // CHUNK - BAND_CHUNKS)


def _band_window(s, tq):
    return min(BAND_CHUNKS * CHUNK + tq, s)


def _band_bias_tiles(rel_table, s, tq):
    win = _band_window(s, tq)
    n_var = min(BAND_CHUNKS * CHUNK // tq + 1, s // tq)
    tiles = []
    for var in range(n_var):
        q0 = var * tq
        k0 = max(q0 - BAND_CHUNKS * CHUNK, 0)
        valid = _band_valid(q0 + np.arange(tq)[:, None], k0 + np.arange(win)[None, :])
        tiles.append(jnp.where(jnp.asarray(valid)[None], _toeplitz_bias(rel_table, tq, win, q0 - k0), NEG))
    return jnp.stack(tiles)


def _pair_softmax_out(qm, kw, vw, bias2, t):
    s = (_dot_nt(qm, kw).reshape(2, t, -1) + bias2).reshape(2 * t, -1)
    m = jnp.max(s, axis=-1, keepdims=True)
    p = jnp.exp2(s - m)
    l = jnp.sum(p, axis=-1, keepdims=True)
    o = _dot(p.astype(BF16), vw) / l
    lo, _ = _half_masks((t, 2 * HEAD_DIM))
    return jnp.where(lo, o[:t], o[t:])


def _bandattn_kernel(bias_ref, q_ref, k_ref, v_ref, o_ref, *, tq, win, n_var):
    qi = pl.program_id(2)
    var = jnp.minimum(qi, n_var - 1)
    k0 = pl.multiple_of(jnp.maximum(qi * tq - BAND_CHUNKS * CHUNK, 0), tq)
    qm = _stack_halves(q_ref[0])
    kw = k_ref[0, pl.ds(k0, win), :]
    vw = v_ref[0, pl.ds(k0, win), :]
    o_ref[0] = _pair_softmax_out(qm, kw, vw, bias_ref[var], tq).astype(o_ref.dtype)


def _bandattn(qb, kb, vb, bias_tiles, *, tq):
    b, s, _ = qb.shape
    n_var, _, _, win = bias_tiles.shape
    kern = functools.partial(_bandattn_kernel, tq=tq, win=win, n_var=n_var)
    return pl.pallas_call(
        kern,
        out_shape=jax.ShapeDtypeStruct((b, s, D_B), BF16),
        grid=(H_B // 2, b, s // tq),
        in_specs=[pl.BlockSpec((n_var, 2, tq, win), lambda hp, bi, i: (0, hp, 0, 0)),
                  pl.BlockSpec((1, tq, 2 * HEAD_DIM), lambda hp, bi, i: (bi, i, hp)),
                  pl.BlockSpec((1, s, 2 * HEAD_DIM), lambda hp, bi, i: (bi, 0, hp)),
                  pl.BlockSpec((1, s, 2 * HEAD_DIM), lambda hp, bi, i: (bi, 0, hp))],
        out_specs=pl.BlockSpec((1, tq, 2 * HEAD_DIM), lambda hp, bi, i: (bi, i, hp)),
        compiler_params=_cparams(("parallel", "parallel", "arbitrary")),
        name="bandattn",
    )(bias_tiles, qb, kb, vb)


def _two_part_softmax(s_c, s_n, vc, vn):
    m = jnp.maximum(jnp.max(s_c, axis=-1, keepdims=True), jnp.max(s_n, axis=-1, keepdims=True))
    p_c = jnp.exp2(s_c - m)
    p_n = jnp.exp2(s_n - m)
    l = jnp.sum(p_c, axis=-1, keepdims=True) + jnp.sum(p_n, axis=-1, keepdims=True)
    return (_dot(p_c.astype(BF16), vc) + _dot(p_n.astype(BF16), vn)) / l


def _sampattn_kernel(cs_ref, lam4_ref, g_ref, bbias_ref, qa_ref, ckd_ref, cvd_ref, kan_ref, van_ref,
                     qb_ref, ckb_ref, cvb_ref, kbn_ref, vbn_ref, oa_ref, ob_ref, *, n, past, lam_init):
    lam = _lam_value(lam4_ref, lam_init)
    tq_pos = past + lax.broadcasted_iota(jnp.int32, (n, past), 0)
    ts_pos = lax.broadcasted_iota(jnp.int32, (n, past), 1)
    dist_c = jnp.abs(tq_pos - ts_pos).astype(F32)
    vis_c = (ts_pos // CHUNK) <= (tq_pos // CHUNK)
    ii = lax.broadcasted_iota(jnp.int32, (n, n), 0)
    jj = lax.broadcasted_iota(jnp.int32, (n, n), 1)
    dist_n = jnp.abs(ii - jj).astype(F32)
    vis_n = ((past + jj) // CHUNK) <= ((past + ii) // CHUNK)
    for h in range(H_A):
        sl = slice(h * 2 * HEAD_DIM, (h + 1) * 2 * HEAD_DIM)
        cs = cs_ref[h]
        qm = _stack_halves(qa_ref[0, :, sl])
        bias_c = jnp.where(vis_c, -cs * dist_c, NEG)
        bias_n = jnp.where(vis_n, -cs * dist_n, NEG)
        s_c = (_dot_nt(qm, ckd_ref[0, :, sl].astype(BF16)).reshape(2, n, past) + bias_c[None]).reshape(2 * n, past)
        s_n = (_dot_nt(qm, kan_ref[0, :, sl]).reshape(2, n, n) + bias_n[None]).reshape(2 * n, n)
        o = _two_part_softmax(s_c, s_n, cvd_ref[0, :, sl].astype(BF16), van_ref[0, :, sl])
        oa_ref[0, :, sl] = _diff_finish(o[:n], o[n:], lam, g_ref[...], lam_init).astype(oa_ref.dtype)
    lb = ckb_ref.shape[1]
    for hp in range(H_B // 2):
        sl = slice(hp * 2 * HEAD_DIM, (hp + 1) * 2 * HEAD_DIM)
        qm = _stack_halves(qb_ref[0, :, sl])
        bias = bbias_ref[2 * hp:2 * hp + 2]
        s_c = (_dot_nt(qm, ckb_ref[0, :, sl].astype(BF16)).reshape(2, n, lb) + bias[:, :, :lb]).reshape(2 * n, lb)
        s_n = (_dot_nt(qm, kbn_ref[0, :, sl]).reshape(2, n, n) + bias[:, :, lb:]).reshape(2 * n, n)
        o = _two_part_softmax(s_c, s_n, cvb_ref[0, :, sl].astype(BF16), vbn_ref[0, :, sl])
        lo, _ = _half_masks((n, 2 * HEAD_DIM))
        ob_ref[0, :, sl] = jnp.where(lo, o[:n], o[n:]).astype(ob_ref.dtype)


def _sample_band_bias(rel_table, n, past, lb):
    valid = _band_valid(past + np.arange(n)[:, None], (past - lb) + np.arange(lb + n)[None, :])
    return jnp.where(jnp.asarray(valid)[None], _toeplitz_bias(rel_table, n, lb + n, lb), NEG)


def _sampattn(qa, ckd, cvd, kan, van, qb, ckb, cvb, kbn, vbn, cslopes, lam4, subln_g, bbias, lam_init):
    b, n, _ = qa.shape
    past = ckd.shape[1]
    lb = ckb.shape[1]
    kern = functools.partial(_sampattn_kernel, n=n, past=past, lam_init=lam_init)
    full = lambda shape: pl.BlockSpec(shape, lambda bi: (0,) * len(shape))
    per_b = lambda r, w: pl.BlockSpec((1, r, w), lambda bi: (bi, 0, 0))
    return pl.pallas_call(
        kern,
        out_shape=(jax.ShapeDtypeStruct((b, n, D_A), BF16), jax.ShapeDtypeStruct((b, n, D_B), BF16)),
        grid=(b,),
        in_specs=[pl.BlockSpec(memory_space=pltpu.SMEM), full((4, HEAD_DIM)), full((1, 2 * HEAD_DIM)),
                  full((H_B, n, lb + n)),
                  per_b(n, D_A), per_b(past, D_A), per_b(past, D_A), per_b(n, D_A), per_b(n, D_A),
                  per_b(n, D_B), per_b(lb, D_B), per_b(lb, D_B), per_b(n, D_B), per_b(n, D_B)],
        out_specs=(per_b(n, D_A), per_b(n, D_B)),
        compiler_params=_cparams(("parallel",)),
        name="sampattn",
    )(cslopes, lam4, subln_g.reshape(1, 2 * HEAD_DIM), bbias, qa, ckd, cvd, kan, van, qb, ckb, cvb, kbn, vbn)


def _outproj_kernel(oa_ref, ob_ref, sga_ref, sgb_ref, x_ref, g1_ref, sh2_ref, sc2_ref, pa_ref, pb_ref, wo_ref,
                    lng_ref, lnb_ref, x1_ref, h_ref, *, alpha):
    mix = (sga_ref[...].astype(F32) * _dot(oa_ref[...], pa_ref[...])
           + sgb_ref[...].astype(F32) * _dot(ob_ref[...], pb_ref[...]))
    y = _dot(mix.astype(BF16), wo_ref[...])
    x1 = _ln(alpha * x_ref[...] + _rows(g1_ref) * y) * lng_ref[...] + lnb_ref[...]
    x1_ref[...] = x1
    u2 = _ln(x1) * (1.0 + _rows(sc2_ref)) + _rows(sh2_ref)
    h_ref[...] = _pack_pair(u2[:, :PACKED], u2[:, PACKED:])


def _outproj(oa, ob, sga, sgb, x, gate1, shift2, scale2, pa16, pb16, wo16, ln_g, ln_b, *, tm, rows_per_seq,
             per_row, alpha):
    r = x.shape[0]
    row = lambda w: pl.BlockSpec((tm, w), lambda i: (i, 0))
    full = lambda a, b: pl.BlockSpec((a, b), lambda i: (0, 0))
    mod = _mod_spec(per_row, tm, rows_per_seq)
    return pl.pallas_call(
        functools.partial(_outproj_kernel, alpha=alpha),
        out_shape=(jax.ShapeDtypeStruct((r, D_MODEL), F32), jax.ShapeDtypeStruct((r, PACKED), jnp.uint32)),
        grid=(r // tm,),
        in_specs=[row(D_A), row(D_B), row(D_MODEL), row(D_MODEL), row(D_MODEL), mod, mod, mod,
                  full(D_A, D_MODEL), full(D_B, D_MODEL), full(D_MODEL, D_MODEL),
                  full(1, D_MODEL), full(1, D_MODEL)],
        out_specs=(row(D_MODEL), row(PACKED)),
        compiler_params=_cparams(("parallel",)),
        name="outproj",
    )(oa, ob, sga, sgb, x, gate1, shift2, scale2, pa16, pb16, wo16, ln_g.reshape(1, -1), ln_b.reshape(1, -1))


def _first_index_of_max(x, iota, axis, size):
    m = jnp.max(x, axis=axis, keepdims=True)
    idx = jnp.min(jnp.where(x == m, iota, size), axis=axis, keepdims=True)
    return m, idx


def _router_kernel(h_ref, wr_hi_ref, wr_lo_ref, eb_ref, idx_ref, wt_ref, rank_ref, cnt_ref, run_ref, *, tr):
    step = pl.program_id(0)

    @pl.when(step == 0)
    def _():
        run_ref[...] = jnp.zeros_like(run_ref)

    ha, hb = _unpack_pair(h_ref[...])
    wr_hi, wr_lo = wr_hi_ref[...], wr_lo_ref[...]
    logits = (_dot_nt(wr_hi[:, :PACKED], ha) + _dot_nt(wr_hi[:, PACKED:], hb)
              + _dot_nt(wr_lo[:, :PACKED], ha) + _dot_nt(wr_lo[:, PACKED:], hb))
    scores = jax.nn.sigmoid(logits)
    biased = scores + eb_ref[...]

    x3 = biased.reshape(N_GROUP, GROUP_SIZE, tr)
    io3 = lax.broadcasted_iota(jnp.int32, x3.shape, 1)
    m1, i1 = _first_index_of_max(x3, io3, 1, GROUP_SIZE)
    m2 = jnp.max(jnp.where(io3 == i1, -jnp.inf, x3), axis=1, keepdims=True)
    grp = (m1 + m2).reshape(N_GROUP, tr)

    iog = lax.broadcasted_iota(jnp.int32, grp.shape, 0)
    gsel = jnp.zeros(grp.shape, jnp.bool_)
    for _ in range(TOPK_GROUP):
        _, gi = _first_index_of_max(grp, iog, 0, N_GROUP)
        hit = iog == gi
        gsel = gsel | hit
        grp = jnp.where(hit, -jnp.inf, grp)
    emask = jnp.broadcast_to(gsel.reshape(N_GROUP, 1, tr), (N_GROUP, GROUP_SIZE, tr)).reshape(N_EXPERTS, tr)
    cand = jnp.where(emask, biased, -jnp.inf)

    ioe = lax.broadcasted_iota(jnp.int32, cand.shape, 0)
    hits = []
    sel = jnp.zeros(cand.shape, jnp.bool_)
    for _ in range(TOP_K):
        _, ei = _first_index_of_max(cand, ioe, 0, N_EXPERTS)
        hit = ioe == ei
        hits.append((ei, hit))
        sel = sel | hit
        cand = jnp.where(hit, -jnp.inf, cand)
    self32 = sel.astype(F32)
    ra = lax.broadcasted_iota(jnp.int32, (tr, tr), 0)
    rb = lax.broadcasted_iota(jnp.int32, (tr, tr), 1)
    upper = (ra < rb).astype(BF16)
    ranks = run_ref[...] + _dot(self32.astype(BF16), upper)

    ws = [jnp.sum(jnp.where(hit, scores, 0.0), axis=0, keepdims=True) for _, hit in hits]
    wsum = ws[0]
    for w in ws[1:]:
        wsum = wsum + w
    for k, (ei, hit) in enumerate(hits):
        idx_ref[k:k + 1, :] = ei
        wt_ref[k:k + 1, :] = ws[k] / wsum * ROUTED_SCALE
        rank_ref[k:k + 1, :] = jnp.sum(jnp.where(hit, ranks, 0.0), axis=0, keepdims=True).astype(jnp.int32)
    run_new = run_ref[...] + jnp.sum(self32, axis=1, keepdims=True)
    run_ref[...] = run_new
    cnt_ref[...] = jnp.broadcast_to(run_new, cnt_ref.shape).astype(jnp.int32)


def _router(h_all, w_router, e_bias, *, tr):
    t = h_all.shape[0]
    wr_t = w_router.T
    hi32 = lax.bitcast_convert_type(lax.bitcast_convert_type(wr_t, jnp.uint32) & jnp.uint32(0xFFFF0000), F32)
    wr_hi = hi32.astype(BF16)
    wr_lo = (wr_t - hi32).astype(BF16)
    full = lambda a, b: pl.BlockSpec((a, b), lambda i: (0, 0))
    col = lambda: pl.BlockSpec((TOP_K, tr), lambda i: (0, i))
    return pl.pallas_call(
        functools.partial(_router_kernel, tr=tr),
        out_shape=(jax.ShapeDtypeStruct((TOP_K, t), jnp.int32), jax.ShapeDtypeStruct((TOP_K, t), F32),
                   jax.ShapeDtypeStruct((TOP_K, t), jnp.int32), jax.ShapeDtypeStruct((N_EXPERTS, 128), jnp.int32)),
        grid=(t // tr,),
        in_specs=[pl.BlockSpec((tr, PACKED), lambda i: (i, 0)), full(N_EXPERTS, D_MODEL),
                  full(N_EXPERTS, D_MODEL), full(N_EXPERTS, 1)],
        out_specs=(col(), col(), col(), full(N_EXPERTS, 128)),
        scratch_shapes=[pltpu.VMEM((N_EXPERTS, 1), F32)],
        compiler_params=_cparams(("arbitrary",)),
        name="router",
    )(h_all, wr_hi, wr_lo, e_bias.reshape(N_EXPERTS, 1).astype(F32))


def _dest_kernel(ps_ref, idx_ref, rank_ref, dest_ref):
    idx = idx_ref[...]
    base = jnp.zeros(idx.shape, jnp.int32)
    for e in range(N_EXPERTS):
        base = jnp.where(idx == e, ps_ref[e], base)
    dest_ref[...] = base + rank_ref[...]


def _dest(pad_start, idx_t, rank_t, *, td):
    t = idx_t.shape[1]
    col = lambda: pl.BlockSpec((TOP_K, td), lambda i: (0, i))
    return pl.pallas_call(
        _dest_kernel,
        out_shape=jax.ShapeDtypeStruct((TOP_K, t), jnp.int32),
        grid=(t // td,),
        in_specs=[pl.BlockSpec(memory_space=pltpu.SMEM), col(), col()],
        out_specs=col(),
        compiler_params=_cparams(("parallel",)),
        name="dest",
    )(pad_start, idx_t, rank_t)


def _row_copy(src_ref, src_row, dst_ref, dst_row, sem):
    return pltpu.make_async_copy(src_ref.at[pl.ds(src_row, 1)], dst_ref.at[pl.ds(dst_row, 1)], sem)


def _dispatch_kernel(fill_lo_ref, fill_hi_ref, dest_hbm, h_ref, xs_hbm, dest_smem, zero_ref, dsem, rsem, *, tm):
    i = pl.program_id(0)
    n = pl.num_programs(0)
    slot = i % 2

    def dest_copy(tile, s):
        return pltpu.make_async_copy(dest_hbm.at[tile], dest_smem.at[s], dsem.at[s])

    @pl.when(i == 0)
    def _():
        dest_copy(0, 0).start()
        zero_ref[...] = jnp.zeros_like(zero_ref)
        for e in range(N_EXPERTS):
            lo, hi = fill_lo_ref[e], fill_hi_ref[e]

            def fill(r, c):
                _row_copy(zero_ref, 0, xs_hbm, r, rsem).start()
                return c

            def drain(r, c):
                _row_copy(zero_ref, 0, xs_hbm, r, rsem).wait()
                return c

            lax.fori_loop(lo, hi, fill, 0)
            lax.fori_loop(lo, hi, drain, 0)

        first_free = fill_hi_ref[N_EXPERTS - 1] // zero_ref.shape[0]

        def tail_copy(b):
            start = pl.multiple_of(b * zero_ref.shape[0], zero_ref.shape[0])
            return pltpu.make_async_copy(zero_ref, xs_hbm.at[pl.ds(start, zero_ref.shape[0])], rsem)

        def fill_tail(b, c):
            tail_copy(b).start()
            return c

        def drain_tail(b, c):
            tail_copy(b).wait()
            return c

        lax.fori_loop(first_free, xs_hbm.shape[0] // zero_ref.shape[0], fill_tail, 0)
        lax.fori_loop(first_free, xs_hbm.shape[0] // zero_ref.shape[0], drain_tail, 0)

    dest_copy(i, slot).wait()

    @pl.when(i + 1 < n)
    def _():
        dest_copy(i + 1, 1 - slot).start()

    def issue(t, c):
        for k in range(TOP_K):
            _row_copy(h_ref, t, xs_hbm, dest_smem[slot, k, t], rsem).start()
        return c

    lax.fori_loop(0, tm, issue, 0)
    for _ in range(TOP_K):
        pltpu.make_async_copy(h_ref, xs_hbm.at[pl.ds(0, tm)], rsem).wait()


def _dispatch(h_all, dest3, fill_lo, fill_hi, n_slots, *, tm):
    t = h_all.shape[0]
    return pl.pallas_call(
        functools.partial(_dispatch_kernel, tm=tm),
        out_shape=jax.ShapeDtypeStruct((n_slots, PACKED), jnp.uint32),
        grid_spec=pltpu.PrefetchScalarGridSpec(
            num_scalar_prefetch=2,
            grid=(t // tm,),
            in_specs=[pl.BlockSpec(memory_space=pl.ANY),
                      pl.BlockSpec((tm, PACKED), lambda i, lo, hi: (i, 0))],
            out_specs=pl.BlockSpec(memory_space=pl.ANY),
            scratch_shapes=[pltpu.SMEM((2, TOP_K, tm), jnp.int32), pltpu.VMEM((EXPERT_BLOCK, PACKED), jnp.uint32),
                            pltpu.SemaphoreType.DMA((2,)), pltpu.SemaphoreType.DMA(())],
        ),
        compiler_params=_cparams(("arbitrary",)),
        name="dispatch",
    )(fill_lo, fill_hi, dest3, h_all)


def _experts_kernel(be_ref, nu_ref, x_ref, w13_ref, w2_ref, y_ref):
    @pl.when(pl.program_id(0) < nu_ref[0])
    def _():
        a = _dot_packed(x_ref[...], w13_ref[0])
        hid = (a[:, :D_EXPERT] * jax.nn.sigmoid(a[:, :D_EXPERT])) * a[:, D_EXPERT:]
        y = _dot(hid.astype(BF16), w2_ref[0])
        y_ref[...] = _pack_pair(y[:, :PACKED], y[:, PACKED:])

    @pl.when(pl.program_id(0) >= nu_ref[0])
    def _():
        y_ref[...] = jnp.zeros_like(y_ref)


def _experts(xs, w13, w2, block_e, n_used):
    n_slots = xs.shape[0]
    nb = n_slots // EXPERT_BLOCK
    last = lambda i, nu: jnp.minimum(i, nu[0] - 1)
    return pl.pallas_call(
        _experts_kernel,
        out_shape=jax.ShapeDtypeStruct((n_slots, PACKED), jnp.uint32),
        grid_spec=pltpu.PrefetchScalarGridSpec(
            num_scalar_prefetch=2,
            grid=(nb,),
            in_specs=[pl.BlockSpec((EXPERT_BLOCK, PACKED), lambda i, be, nu: (last(i, nu), 0)),
                      pl.BlockSpec((1, D_MODEL, 2 * D_EXPERT), lambda i, be, nu: (be[last(i, nu)], 0, 0)),
                      pl.BlockSpec((1, D_EXPERT, D_MODEL), lambda i, be, nu: (be[last(i, nu)], 0, 0))],
            out_specs=pl.BlockSpec((EXPERT_BLOCK, PACKED), lambda i, be, nu: (i, 0)),
        ),
        compiler_params=_cparams(("arbitrary",)),
        name="experts",
    )(block_e, n_used, xs, w13, w2)


def _final_kernel(dest_hbm, yb_hbm, h_ref, x1_ref, wt_ref, g2_ref, ws13_ref, ws2_ref, lng_ref, lnb_ref, o_ref,
                  dest_smem, ybuf, dsem, rsem, *, alpha, tm, off):
    i = pl.program_id(0)
    n = pl.num_programs(0)
    slot = i % 2

    def gather_tile(tile, s):
        cp = pltpu.make_async_copy(dest_hbm.at[tile + off], dest_smem, dsem)
        cp.start()
        cp.wait()

        def issue(t, c):
            for k in range(TOP_K):
                _row_copy(yb_hbm, dest_smem[k, t], ybuf.at[s, k], t, rsem.at[s]).start()
            return c

        lax.fori_loop(0, tm, issue, 0)

    @pl.when(i == 0)
    def _():
        gather_tile(0, 0)

    @pl.when(i + 1 < n)
    def _():
        gather_tile(i + 1, 1 - slot)

    a = _dot_packed(h_ref[...], ws13_ref[...])
    hid = (a[:, :D_EXPERT] * jax.nn.sigmoid(a[:, :D_EXPERT])) * a[:, D_EXPERT:]
    y = _dot(hid.astype(BF16), ws2_ref[...])

    for k in range(TOP_K):
        pltpu.make_async_copy(yb_hbm.at[pl.ds(0, tm)], ybuf.at[slot, k], rsem.at[slot]).wait()
    wt = wt_ref[...]
    ya = jnp.zeros((tm, PACKED), F32)
    yb = jnp.zeros((tm, PACKED), F32)
    for k in range(TOP_K):
        ea, eb = _unpack_pair(ybuf[slot, k])
        ya = ya + wt[:, k:k + 1] * ea.astype(F32)
        yb = yb + wt[:, k:k + 1] * eb.astype(F32)
    y = y + jnp.concatenate([ya, yb], axis=1)
    o_ref[...] = _ln(alpha * x1_ref[...] + _rows(g2_ref) * y) * lng_ref[...] + lnb_ref[...]


def _final(dest3, yb, h_all, x1, wt, gate2, ws13, ws2, ln_g, ln_b, *, tm, row0, rows_per_seq, per_row, alpha):
    r = x1.shape[0]
    off = row0 // tm
    row = lambda w: pl.BlockSpec((tm, w), lambda i: (i, 0))
    row_off = lambda w: pl.BlockSpec((tm, w), lambda i: (i + off, 0))
    full = lambda a, b: pl.BlockSpec((a, b), lambda i: (0, 0))
    any_spec = pl.BlockSpec(memory_space=pl.ANY)
    return pl.pallas_call(
        functools.partial(_final_kernel, alpha=alpha, tm=tm, off=off),
        out_shape=jax.ShapeDtypeStruct((r, D_MODEL), F32),
        grid=(r // tm,),
        in_specs=[any_spec, any_spec, row_off(PACKED), row(D_MODEL), row_off(TOP_K),
                  _mod_spec(per_row, tm, rows_per_seq),
                  full(D_MODEL, 2 * D_EXPERT), full(D_EXPERT, D_MODEL), full(1, D_MODEL), full(1, D_MODEL)],
        out_specs=row(D_MODEL),
        scratch_shapes=[pltpu.SMEM((TOP_K, tm), jnp.int32), pltpu.VMEM((2, TOP_K, tm, PACKED), jnp.uint32),
                        pltpu.SemaphoreType.DMA(()), pltpu.SemaphoreType.DMA((2,))],
        compiler_params=_cparams(("arbitrary",)),
        name="final",
    )(dest3, yb, h_all, x1, wt, gate2, ws13, ws2, ln_g.reshape(1, -1), ln_b.reshape(1, -1))


def _pick_tile(n, pref):
    t = min(pref, n)
    while n % t:
        t //= 2
    return t


def kernel(x_prompt, x_sample, cache_k_diff, cache_v_diff, cache_k_band, cache_v_band, c_prompt, c_sample,
           w_ada, b_ada, w_in, b_in, lambda_q1, lambda_k1, lambda_q2, lambda_k2, subln_g, rel_bias, p_a, p_b,
           w_out, ln1_g, ln1_b, w_router, e_bias, w1, w3, w2, ws1, ws3, ws2, ln2_g, ln2_b):
    depth = w_in.shape[0]
    alpha = (2 * depth) ** 0.25
    bp, sp, d = x_prompt.shape
    bs, ss, _ = x_sample.shape
    past = cache_k_diff.shape[2]
    lb = cache_k_band.shape[2]
    tp, ts_ = bp * sp, bs * ss
    t_all = tp + ts_
    slopes = (2.0 ** (-8.0 * np.arange(1, H_A + 1) / H_A)).astype(np.float32)
    cslopes = jnp.asarray(np.float32(LOG2E) * slopes)

    yp = x_prompt.reshape(tp, d)
    ys = x_sample.reshape(ts_, d)
    outs = [[] for _ in range(8)]
    tail_p = min(BAND_CHUNKS * CHUNK, sp)
    tm_p = _pick_tile(math.gcd(sp, tail_p), 256)
    tq = _pick_tile(sp, 256)
    ta = _pick_tile(sp, 512)
    ka_aug = _key_bias_columns(slopes, sp)

    for l in range(depth):
        lam_init = 0.8 - 0.6 * math.exp(-0.3 * l)
        lam4 = jnp.stack([lambda_q1[l], lambda_k1[l], lambda_q2[l], lambda_k2[l]]).astype(F32)
        w_in16 = w_in[l].astype(BF16)
        b_in2 = b_in[l].reshape(1, N_IN)
        pa16, pb16, wo16 = p_a[l].astype(BF16), p_b[l].astype(BF16), w_out[l].astype(BF16)

        mod = _ada(jnp.concatenate([c_prompt, c_sample], axis=0), w_ada[l], b_ada[l])
        mod_p = [m.reshape(bp, 1, d) for m in jnp.split(mod[:bp], 6, axis=-1)]
        mod_s = [jnp.repeat(m, ss, axis=0) for m in jnp.split(mod[bp:], 6, axis=-1)]

        (qa, ka32, va32, ka16, va16, qb, kb16, vb16, sga, sgb, kbt, vbt) = _inproj(
            yp, mod_p[0], mod_p[1], w_in16, b_in2, tm=tm_p, rows_per_seq=sp, tail_rows=tail_p, per_row=False)
        r3 = lambda a: a.reshape(bp, sp, a.shape[-1])
        oa = _diffattn(r3(qa), r3(ka16), ka_aug, r3(va16), cslopes, lam4, subln_g[l], lam_init, t=ta)
        ob = _bandattn(r3(qb), r3(kb16), r3(vb16), _band_bias_tiles(rel_bias[l], sp, tq), tq=tq)
        x1p, hp = _outproj(oa.reshape(tp, D_A), ob.reshape(tp, D_B), sga, sgb, yp, mod_p[2], mod_p[3], mod_p[4],
                           pa16, pb16, wo16, ln1_g[l], ln1_b[l], tm=_pick_tile(sp, 512), rows_per_seq=sp,
                           per_row=False, alpha=alpha)
        outs[0].append(ka32.reshape(bp, sp, H_A, 2, HEAD_DIM))
        outs[1].append(va32.reshape(bp, sp, H_A, 2 * HEAD_DIM))
        outs[2].append(kbt.reshape(bp, tail_p, H_B, HEAD_DIM))
        outs[3].append(vbt.reshape(bp, tail_p, H_B, HEAD_DIM))

        (qa_s, ka32_s, va32_s, ka16_s, va16_s, qb_s, kb16_s, vb16_s, sga_s, sgb_s, kbt_s, vbt_s) = _inproj(
            ys, mod_s[0], mod_s[1], w_in16, b_in2, tm=ts_, rows_per_seq=ts_, tail_rows=ts_, per_row=True)
        s3 = lambda a: a.reshape(bs, ss, a.shape[-1])
        oa_s, ob_s = _sampattn(
            s3(qa_s), cache_k_diff[l].reshape(bs, past, D_A), cache_v_diff[l].reshape(bs, past, D_A),
            s3(ka16_s), s3(va16_s), s3(qb_s), cache_k_band[l].reshape(bs, lb, D_B),
            cache_v_band[l].reshape(bs, lb, D_B), s3(kb16_s), s3(vb16_s), cslopes, lam4, subln_g[l],
            _sample_band_bias(rel_bias[l], ss, past, lb), lam_init)
        x1s, hs = _outproj(oa_s.reshape(ts_, D_A), ob_s.reshape(ts_, D_B), sga_s, sgb_s, ys, mod_s[2], mod_s[3],
                           mod_s[4], pa16, pb16, wo16, ln1_g[l], ln1_b[l], tm=ts_, rows_per_seq=ts_, per_row=True,
                           alpha=alpha)
        outs[4].append(ka32_s.reshape(bs, ss, H_A, 2, HEAD_DIM))
        outs[5].append(va32_s.reshape(bs, ss, H_A, 2 * HEAD_DIM))
        outs[6].append(kbt_s.reshape(bs, ss, H_B, HEAD_DIM))
        outs[7].append(vbt_s.reshape(bs, ss, H_B, HEAD_DIM))

        h_all = jnp.concatenate([hp, hs], axis=0)
        tr = _pick_tile(t_all, 512)
        idx_t, wt_t, rank_t, cnt = _router(h_all, w_router[l], e_bias[l], tr=tr)
        counts = cnt[:, 0]
        padded = (counts + EXPERT_BLOCK - 1) // EXPERT_BLOCK * EXPERT_BLOCK
        pad_end = jnp.cumsum(padded)
        pad_start = pad_end - padded
        dest = _dest(pad_start.astype(jnp.int32), idx_t, rank_t, td=_pick_tile(t_all, 2048))
        nb = -(-(t_all * TOP_K) // EXPERT_BLOCK) + N_EXPERTS
        n_slots = nb * EXPERT_BLOCK
        block_starts = jnp.arange(nb, dtype=jnp.int32) * EXPERT_BLOCK
        block_e = jnp.minimum(jnp.sum(pad_end[None, :] <= block_starts[:, None], axis=1),
                              N_EXPERTS - 1).astype(jnp.int32)
        n_used = (pad_end[-1:] // EXPERT_BLOCK).astype(jnp.int32)
        tiled = lambda tm: dest.reshape(TOP_K, t_all // tm, tm).transpose(1, 0, 2)
        tm_d = _pick_tile(math.gcd(sp, t_all), 256)
        xs = _dispatch(h_all, tiled(tm_d), (pad_start + counts).astype(jnp.int32), pad_end.astype(jnp.int32),
                       n_slots, tm=tm_d)
        w13 = jnp.concatenate([w1[l], w3[l]], axis=-1).astype(BF16)
        yb = _experts(xs, w13, w2[l].astype(BF16), block_e, n_used)
        wt = wt_t.T
        ws13 = jnp.concatenate([ws1[l], ws3[l]], axis=-1).astype(BF16)
        ws2b = ws2[l].astype(BF16)
        yp = _final(tiled(tm_d), yb, h_all, x1p, wt, mod_p[5], ws13, ws2b, ln2_g[l], ln2_b[l], tm=tm_d, row0=0,
                    rows_per_seq=sp, per_row=False, alpha=alpha)
        ys = _final(tiled(ts_), yb, h_all, x1s, wt, mod_s[5], ws13, ws2b, ln2_g[l], ln2_b[l], tm=ts_, row0=tp,
                    rows_per_seq=ts_, per_row=True, alpha=alpha)

    return (yp.reshape(bp, sp, d), ys.reshape(bs, ss, d)) + tuple(jnp.stack(o) for o in outs)
```

```python
import functools
import math

import jax
import jax.numpy as jnp
import numpy as np
from jax import lax
from jax.experimental import pallas as pl
from jax.experimental.pallas import tpu as pltpu

F32 = jnp.float32
BF16 = jnp.bfloat16

D_MODEL = 1024
CHUNK = 64
HEAD_DIM = 64
H_A = 8
H_B = 8
D_A = H_A * 2 * HEAD_DIM
D_B = H_B * HEAD_DIM
BAND_CHUNKS = 8
REL_CLIP = 128
N_IN = 3 * D_A + 3 * D_B + 2 * D_MODEL
N_EXPERTS = 64
TOP_K = 8
N_GROUP = 8
TOPK_GROUP = 4
GROUP_SIZE = N_EXPERTS // N_GROUP
D_EXPERT = 256
ROUTED_SCALE = 2.5
EXPERT_BLOCK = 512
LN_EPS = 1e-5
NEG = -1e30
LOG2E = math.log2(math.e)
Q_SCALE = HEAD_DIM ** -0.5 * LOG2E

VMEM_LIMIT = 56 * 1024 * 1024


def _cparams(sem):
    return pltpu.CompilerParams(dimension_semantics=sem, vmem_limit_bytes=VMEM_LIMIT)


def _ln(x):
    mu = jnp.mean(x, axis=-1, keepdims=True)
    xc = x - mu
    var = jnp.mean(xc * xc, axis=-1, keepdims=True)
    return xc * lax.rsqrt(var + LN_EPS)


def _rows(ref):
    v = ref[...]
    return v.reshape(v.shape[-2], v.shape[-1])


def _split_bf16(x):
    hi = x.astype(BF16)
    lo = (x - hi.astype(F32)).astype(BF16)
    return hi, lo


def _dot(a, b):
    return jnp.dot(a, b, preferred_element_type=F32)


def _dot_nt(a, b):
    return lax.dot_general(a, b, (((1,), (1,)), ((), ())), preferred_element_type=F32)


PACKED = D_MODEL // 2


def _pack_pair(a, b):
    ua = lax.bitcast_convert_type(a.astype(BF16).astype(F32), jnp.uint32)
    ub = lax.bitcast_convert_type(b.astype(BF16).astype(F32), jnp.uint32)
    return ua | (ub >> 16)


def _unpack_pair(u):
    a = lax.bitcast_convert_type(u & jnp.uint32(0xFFFF0000), F32).astype(BF16)
    b = lax.bitcast_convert_type(u << 16, F32).astype(BF16)
    return a, b


def _dot_packed(u, w):
    a, b = _unpack_pair(u)
    return _dot(a, w[:PACKED]) + _dot(b, w[PACKED:])


def _ada_kernel(c_ref, w_ref, b_ref, o_ref):
    c = c_ref[...]
    a = c * jax.nn.sigmoid(c)
    a_hi, a_lo = _split_bf16(a)
    w_hi, w_lo = _split_bf16(w_ref[...])
    o_ref[...] = _dot(a_hi, w_hi) + _dot(a_lo, w_hi) + _dot(a_hi, w_lo) + b_ref[...]


def _ada(c, w_ada, b_ada):
    n, d = c.shape
    nout = w_ada.shape[1]
    tn = 1024
    return pl.pallas_call(
        _ada_kernel,
        out_shape=jax.ShapeDtypeStruct((n, nout), F32),
        grid=(nout // tn,),
        in_specs=[pl.BlockSpec((n, d), lambda j: (0, 0)),
                  pl.BlockSpec((d, tn), lambda j: (0, j)),
                  pl.BlockSpec((1, tn), lambda j: (0, j))],
        out_specs=pl.BlockSpec((n, tn), lambda j: (0, j)),
        compiler_params=_cparams(("parallel",)),
        name="ada",
    )(c, w_ada, b_ada.reshape(1, nout))


_SEG_QA = (0, D_A)
_SEG_KA = (D_A, 2 * D_A)
_SEG_VA = (2 * D_A, 3 * D_A)
_SEG_QB = (3 * D_A, 3 * D_A + D_B)
_SEG_KB = (3 * D_A + D_B, 3 * D_A + 2 * D_B)
_SEG_VB = (3 * D_A + 2 * D_B, 3 * D_A + 3 * D_B)
_SEG_GA = (3 * D_A + 3 * D_B, 3 * D_A + 3 * D_B + D_MODEL)
_SEG_GB = (3 * D_A + 3 * D_B + D_MODEL, N_IN)


def _inproj_kernel(x_ref, sh_ref, sc_ref, w_ref, b_ref,
                   qa_ref, ka32_ref, va32_ref, ka16_ref, va16_ref,
                   qb_ref, kb16_ref, vb16_ref, sga_ref, sgb_ref, kbt_ref, vbt_ref):
    u = (_ln(x_ref[...]) * (1.0 + _rows(sc_ref)) + _rows(sh_ref)).astype(BF16)

    def seg(lo_hi):
        lo, hi = lo_hi
        return _dot(u, w_ref[:, lo:hi]) + b_ref[:, lo:hi]

    qa_ref[...] = (seg(_SEG_QA) * Q_SCALE).astype(BF16)
    ka = seg(_SEG_KA)
    ka32_ref[...] = ka
    ka16_ref[...] = ka.astype(BF16)
    va = seg(_SEG_VA)
    va32_ref[...] = va
    va16_ref[...] = va.astype(BF16)
    qb_ref[...] = (seg(_SEG_QB) * Q_SCALE).astype(BF16)
    kb = seg(_SEG_KB)
    kbt_ref[...] = kb
    kb16_ref[...] = kb.astype(BF16)
    vb = seg(_SEG_VB)
    vbt_ref[...] = vb
    vb16_ref[...] = vb.astype(BF16)
    sga_ref[...] = jax.nn.sigmoid(seg(_SEG_GA)).astype(BF16)
    sgb_ref[...] = jax.nn.sigmoid(seg(_SEG_GB)).astype(BF16)


def _mod_spec(per_row, tm, rows_per_seq):
    if per_row:
        return pl.BlockSpec((tm, D_MODEL), lambda i: (i, 0))
    tiles_per_seq = rows_per_seq // tm
    return pl.BlockSpec((1, 1, D_MODEL), lambda i: (i // tiles_per_seq, 0, 0))


def _inproj(x, shift, scale, w_in16, b_in, *, tm, rows_per_seq, tail_rows, per_row):
    r = x.shape[0]
    nseq = r // rows_per_seq
    n_i = rows_per_seq // tm
    n_t = tail_rows // tm

    def tail_map(i):
        return ((i // n_i) * n_t + jnp.maximum(i % n_i - (n_i - n_t), 0), 0)

    row = lambda w: pl.BlockSpec((tm, w), lambda i: (i, 0))
    out_shape = (
        jax.ShapeDtypeStruct((r, D_A), BF16),
        jax.ShapeDtypeStruct((r, D_A), F32),
        jax.ShapeDtypeStruct((r, D_A), F32),
        jax.ShapeDtypeStruct((r, D_A), BF16),
        jax.ShapeDtypeStruct((r, D_A), BF16),
        jax.ShapeDtypeStruct((r, D_B), BF16),
        jax.ShapeDtypeStruct((r, D_B), BF16),
        jax.ShapeDtypeStruct((r, D_B), BF16),
        jax.ShapeDtypeStruct((r, D_MODEL), BF16),
        jax.ShapeDtypeStruct((r, D_MODEL), BF16),
        jax.ShapeDtypeStruct((nseq * tail_rows, D_B), F32),
        jax.ShapeDtypeStruct((nseq * tail_rows, D_B), F32),
    )
    out_specs = (row(D_A), row(D_A), row(D_A), row(D_A), row(D_A),
                 row(D_B), row(D_B), row(D_B), row(D_MODEL), row(D_MODEL),
                 pl.BlockSpec((tm, D_B), tail_map), pl.BlockSpec((tm, D_B), tail_map))
    return pl.pallas_call(
        _inproj_kernel,
        out_shape=out_shape,
        grid=(r // tm,),
        in_specs=[row(D_MODEL), _mod_spec(per_row, tm, rows_per_seq), _mod_spec(per_row, tm, rows_per_seq),
                  pl.BlockSpec((D_MODEL, N_IN), lambda i: (0, 0)),
                  pl.BlockSpec((1, N_IN), lambda i: (0, 0))],
        out_specs=out_specs,
        compiler_params=_cparams(("arbitrary",)),
        name="inproj",
    )(x, shift, scale, w_in16, b_in)


N_AUG = 3


def _lam_value(lam4_ref, lam_init):
    v = lam4_ref[...]
    a = jnp.sum(v[0:1, :] * v[1:2, :], axis=-1, keepdims=True)
    b = jnp.sum(v[2:3, :] * v[3:4, :], axis=-1, keepdims=True)
    return jnp.exp(a) - jnp.exp(b) + lam_init


def _half_masks(shape):
    lane = lax.broadcasted_iota(jnp.int32, shape, 1)
    return lane < HEAD_DIM, lane >= HEAD_DIM


def _stack_halves(q):
    lo, hi = _half_masks(q.shape)
    zero = jnp.zeros_like(q)
    return jnp.concatenate([jnp.where(lo, q, zero), jnp.where(hi, q, zero)], axis=0)


def _softmax_step(s, v, m, l, acc):
    m_new = jnp.maximum(m, jnp.max(s, axis=-1, keepdims=True))
    alpha = jnp.exp2(m - m_new)
    p = jnp.exp2(s - m_new)
    l_new = alpha * l + jnp.sum(p, axis=-1, keepdims=True)
    acc_new = alpha * acc + _dot(p.astype(BF16), v)
    return m_new, l_new, acc_new


def _diff_finish(o0, o1, lam, g, lam_init):
    o = o0 - lam * o1
    o = o * lax.rsqrt(jnp.mean(o * o, axis=-1, keepdims=True) + LN_EPS)
    return o * g * (1.0 - lam_init)


def _key_bias_columns(slopes, s):
    x = (np.float32(LOG2E) * slopes.astype(np.float32))[:, None] * np.arange(s, dtype=np.float32)[None, :]
    cols = np.zeros(x.shape + (2 * HEAD_DIM,), np.float32)
    for c in range(N_AUG):
        cols[:, :, c] = x.astype(BF16).astype(np.float32)
        x = x - cols[:, :, c]
    return jnp.asarray(cols, BF16)


def _diffattn_kernel(cs_ref, lam4_ref, g_ref, q_ref, k_ref, ka_ref, v_ref, o_ref, *, t, lam_init):
    h = pl.program_id(1)
    qi = pl.program_id(2)
    cs = cs_ref[h]
    q = q_ref[0]
    lo, hi = _half_masks(q.shape)
    lane = lax.broadcasted_iota(jnp.int32, q.shape, 1)
    zero = jnp.zeros_like(q)
    ones = (lane < N_AUG).astype(BF16)
    q_ext = (jnp.concatenate([jnp.where(lo, q, zero), ones], axis=1),
             jnp.concatenate([jnp.where(hi, q, zero), ones], axis=1))

    def block(j, carry, extra):
        start = pl.multiple_of(j * t, t)
        k_ext = jnp.concatenate([k_ref[0, pl.ds(start, t), :], ka_ref[0, pl.ds(start, t), :]], axis=1)
        vj = v_ref[0, pl.ds(start, t), :]
        out = []
        for half in range(2):
            s = _dot_nt(q_ext[half], k_ext)
            if extra is not None:
                s = s + extra
            out.append(_softmax_step(s, vj, *carry[half]))
        return tuple(out)

    init1 = (jnp.full((t, 1), NEG, F32), jnp.zeros((t, 1), F32), jnp.zeros((t, 2 * HEAD_DIM), F32))
    carry = lax.fori_loop(0, qi, lambda j, c: block(j, c, None), (init1, init1))

    ii = lax.broadcasted_iota(jnp.int32, (t, t), 0)
    jj = lax.broadcasted_iota(jnp.int32, (t, t), 1)
    fix = jnp.where(jj > ii, (2.0 * cs) * (ii - jj).astype(F32), 0.0)
    fix = jnp.where((jj // CHUNK) <= (ii // CHUNK), fix, NEG)
    (m0, l0, a0), (m1, l1, a1) = block(qi, carry, fix)

    lam = _lam_value(lam4_ref, lam_init)
    o_ref[0] = _diff_finish(a0 / l0, a1 / l1, lam, g_ref[...], lam_init).astype(o_ref.dtype)


def _diffattn(qa, ka, ka_aug, va, cslopes, lam4, subln_g, lam_init, *, t):
    b, s, _ = qa.shape
    kern = functools.partial(_diffattn_kernel, t=t, lam_init=lam_init)
    hd2 = 2 * HEAD_DIM
    return pl.pallas_call(
        kern,
        out_shape=jax.ShapeDtypeStruct((b, s, D_A), BF16),
        grid=(b, H_A, s // t),
        in_specs=[pl.BlockSpec(memory_space=pltpu.SMEM),
                  pl.BlockSpec((4, HEAD_DIM), lambda bi, h, i: (0, 0)),
                  pl.BlockSpec((1, hd2), lambda bi, h, i: (0, 0)),
                  pl.BlockSpec((1, t, hd2), lambda bi, h, i: (bi, i, h)),
                  pl.BlockSpec((1, s, hd2), lambda bi, h, i: (bi, 0, h)),
                  pl.BlockSpec((1, s, hd2), lambda bi, h, i: (h, 0, 0)),
                  pl.BlockSpec((1, s, hd2), lambda bi, h, i: (bi, 0, h))],
        out_specs=pl.BlockSpec((1, t, hd2), lambda bi, h, i: (bi, i, h)),
        compiler_params=_cparams(("parallel", "parallel", "arbitrary")),
        name="diffattn",
    )(cslopes, lam4, subln_g.reshape(1, hd2), qa, ka, ka_aug, va)


def _toeplitz_bias(rel_table, rows, cols, delta):
    length = rows + cols - 1
    rel = np.clip(delta + rows - 1 - np.arange(length), -REL_CLIP, REL_CLIP) + REL_CLIP
    u = jnp.take(rel_table.astype(F32) * LOG2E, jnp.asarray(rel, jnp.int32), axis=1)
    u = jnp.pad(u, ((0, 0), (0, 1)))
    r = jnp.tile(u, (1, rows))[:, :rows * length].reshape(-1, rows, length)
    return r[:, :, rows - 1:rows - 1 + cols]


def _band_valid(t_pos, s_pos):
    return (s_pos // CHUNK <= t_pos // CHUNK) & (s_pos // CHUNK >= t_pos // CHUNK - BAND_CHUNKS)


def _band_window(s, tq):
    return min(BAND_CHUNKS * CHUNK + tq, s)


def _band_bias_tiles(rel_table, s, tq):
    win = _band_window(s, tq)
    n_var = min(BAND_CHUNKS * CHUNK // tq + 1, s // tq)
    tiles = []
    for var in range(n_var):
        q0 = var * tq
        k0 = max(q0 - BAND_CHUNKS * CHUNK, 0)
        valid = _band_valid(q0 + np.arange(tq)[:, None], k0 + np.arange(win)[None, :])
        tiles.append(jnp.where(jnp.asarray(valid)[None], _toeplitz_bias(rel_table, tq, win, q0 - k0), NEG))
    return jnp.stack(tiles)


def _pair_softmax_out(qm, kw, vw, bias2, t):
    s = (_dot_nt(qm, kw).reshape(2, t, -1) + bias2).reshape(2 * t, -1)
    m = jnp.max(s, axis=-1, keepdims=True)
    p = jnp.exp2(s - m)
    l = jnp.sum(p, axis=-1, keepdims=True)
    o = _dot(p.astype(BF16), vw) / l
    lo, _ = _half_masks((t, 2 * HEAD_DIM))
    return jnp.where(lo, o[:t], o[t:])


def _bandattn_kernel(bias_ref, q_ref, k_ref, v_ref, o_ref, *, tq, win, n_var):
    qi = pl.program_id(2)
    var = jnp.minimum(qi, n_var - 1)
    k0 = pl.multiple_of(jnp.maximum(qi * tq - BAND_CHUNKS * CHUNK, 0), tq)
    qm = _stack_halves(q_ref[0])
    kw = k_ref[0, pl.ds(k0, win), :]
    vw = v_ref[0, pl.ds(k0, win), :]
    o_ref[0] = _pair_softmax_out(qm, kw, vw, bias_ref[var], tq).astype(o_ref.dtype)


def _bandattn(qb, kb, vb, bias_tiles, *, tq):
    b, s, _ = qb.shape
    n_var, _, _, win = bias_tiles.shape
    kern = functools.partial(_bandattn_kernel, tq=tq, win=win, n_var=n_var)
    return pl.pallas_call(
        kern,
        out_shape=jax.ShapeDtypeStruct((b, s, D_B), BF16),
        grid=(H_B // 2, b, s // tq),
        in_specs=[pl.BlockSpec((n_var, 2, tq, win), lambda hp, bi, i: (0, hp, 0, 0)),
                  pl.BlockSpec((1, tq, 2 * HEAD_DIM), lambda hp, bi, i: (bi, i, hp)),
                  pl.BlockSpec((1, s, 2 * HEAD_DIM), lambda hp, bi, i: (bi, 0, hp)),
                  pl.BlockSpec((1, s, 2 * HEAD_DIM), lambda hp, bi, i: (bi, 0, hp))],
        out_specs=pl.BlockSpec((1, tq, 2 * HEAD_DIM), lambda hp, bi, i: (bi, i, hp)),
        compiler_params=_cparams(("parallel", "parallel", "arbitrary")),
        name="bandattn",
    )(bias_tiles, qb, kb, vb)


def _two_part_softmax(s_c, s_n, vc, vn):
    m = jnp.maximum(jnp.max(s_c, axis=-1, keepdims=True), jnp.max(s_n, axis=-1, keepdims=True))
    p_c = jnp.exp2(s_c - m)
    p_n = jnp.exp2(s_n - m)
    l = jnp.sum(p_c, axis=-1, keepdims=True) + jnp.sum(p_n, axis=-1, keepdims=True)
    return (_dot(p_c.astype(BF16), vc) + _dot(p_n.astype(BF16), vn)) / l


def _sampattn_kernel(cs_ref, lam4_ref, g_ref, bbias_ref, qa_ref, ckd_ref, cvd_ref, kan_ref, van_ref,
                     qb_ref, ckb_ref, cvb_ref, kbn_ref, vbn_ref, oa_ref, ob_ref, *, n, past, lam_init):
    lam = _lam_value(lam4_ref, lam_init)
    tq_pos = past + lax.broadcasted_iota(jnp.int32, (n, past), 0)
    ts_pos = lax.broadcasted_iota(jnp.int32, (n, past), 1)
    dist_c = jnp.abs(tq_pos - ts_pos).astype(F32)
    vis_c = (ts_pos // CHUNK) <= (tq_pos // CHUNK)
    ii = lax.broadcasted_iota(jnp.int32, (n, n), 0)
    jj = lax.broadcasted_iota(jnp.int32, (n, n), 1)
    dist_n = jnp.abs(ii - jj).astype(F32)
    vis_n = ((past + jj) // CHUNK) <= ((past + ii) // CHUNK)
    for h in range(H_A):
        sl = slice(h * 2 * HEAD_DIM, (h + 1) * 2 * HEAD_DIM)
        cs = cs_ref[h]
        qm = _stack_halves(qa_ref[0, :, sl])
        bias_c = jnp.where(vis_c, -cs * dist_c, NEG)
        bias_n = jnp.where(vis_n, -cs * dist_n, NEG)
        s_c = (_dot_nt(qm, ckd_ref[0, :, sl].astype(BF16)).reshape(2, n, past) + bias_c[None]).reshape(2 * n, past)
        s_n = (_dot_nt(qm, kan_ref[0, :, sl]).reshape(2, n, n) + bias_n[None]).reshape(2 * n, n)
        o = _two_part_softmax(s_c, s_n, cvd_ref[0, :, sl].astype(BF16), van_ref[0, :, sl])
        oa_ref[0, :, sl] = _diff_finish(o[:n], o[n:], lam, g_ref[...], lam_init).astype(oa_ref.dtype)
    lb = ckb_ref.shape[1]
    for hp in range(H_B // 2):
        sl = slice(hp * 2 * HEAD_DIM, (hp + 1) * 2 * HEAD_DIM)
        qm = _stack_halves(qb_ref[0, :, sl])
        bias = bbias_ref[2 * hp:2 * hp + 2]
        s_c = (_dot_nt(qm, ckb_ref[0, :, sl].astype(BF16)).reshape(2, n, lb) + bias[:, :, :lb]).reshape(2 * n, lb)
        s_n = (_dot_nt(qm, kbn_ref[0, :, sl]).reshape(2, n, n) + bias[:, :, lb:]).reshape(2 * n, n)
        o = _two_part_softmax(s_c, s_n, cvb_ref[0, :, sl].astype(BF16), vbn_ref[0, :, sl])
        lo, _ = _half_masks((n, 2 * HEAD_DIM))
        ob_ref[0, :, sl] = jnp.where(lo, o[:n], o[n:]).astype(ob_ref.dtype)


def _sample_band_bias(rel_table, n, past, lb):
    valid = _band_valid(past + np.arange(n)[:, None], (past - lb) + np.arange(lb + n)[None, :])
    return jnp.where(jnp.asarray(valid)[None], _toeplitz_bias(rel_table, n, lb + n, lb), NEG)


def _sampattn(qa, ckd, cvd, kan, van, qb, ckb, cvb, kbn, vbn, cslopes, lam4, subln_g, bbias, lam_init):
    b, n, _ = qa.shape
    past = ckd.shape[1]
    lb = ckb.shape[1]
    kern = functools.partial(_sampattn_kernel, n=n, past=past, lam_init=lam_init)
    full = lambda shape: pl.BlockSpec(shape, lambda bi: (0,) * len(shape))
    per_b = lambda r, w: pl.BlockSpec((1, r, w), lambda bi: (bi, 0, 0))
    return pl.pallas_call(
        kern,
        out_shape=(jax.ShapeDtypeStruct((b, n, D_A), BF16), jax.ShapeDtypeStruct((b, n, D_B), BF16)),
        grid=(b,),
        in_specs=[pl.BlockSpec(memory_space=pltpu.SMEM), full((4, HEAD_DIM)), full((1, 2 * HEAD_DIM)),
                  full((H_B, n, lb + n)),
                  per_b(n, D_A), per_b(past, D_A), per_b(past, D_A), per_b(n, D_A), per_b(n, D_A),
                  per_b(n, D_B), per_b(lb, D_B), per_b(lb, D_B), per_b(n, D_B), per_b(n, D_B)],
        out_specs=(per_b(n, D_A), per_b(n, D_B)),
        compiler_params=_cparams(("parallel",)),
        name="sampattn",
    )(cslopes, lam4, subln_g.reshape(1, 2 * HEAD_DIM), bbias, qa, ckd, cvd, kan, van, qb, ckb, cvb, kbn, vbn)


def _outproj_kernel(oa_ref, ob_ref, sga_ref, sgb_ref, x_ref, g1_ref, sh2_ref, sc2_ref, pa_ref, pb_ref, wo_ref,
                    lng_ref, lnb_ref, x1_ref, h_ref, *, alpha):
    mix = (sga_ref[...].astype(F32) * _dot(oa_ref[...], pa_ref[...])
           + sgb_ref[...].astype(F32) * _dot(ob_ref[...], pb_ref[...]))
    y = _dot(mix.astype(BF16), wo_ref[...])
    x1 = _ln(alpha * x_ref[...] + _rows(g1_ref) * y) * lng_ref[...] + lnb_ref[...]
    x1_ref[...] = x1
    u2 = _ln(x1) * (1.0 + _rows(sc2_ref)) + _rows(sh2_ref)
    h_ref[...] = _pack_pair(u2[:, :PACKED], u2[:, PACKED:])


def _outproj(oa, ob, sga, sgb, x, gate1, shift2, scale2, pa16, pb16, wo16, ln_g, ln_b, *, tm, rows_per_seq,
             per_row, alpha):
    r = x.shape[0]
    row = lambda w: pl.BlockSpec((tm, w), lambda i: (i, 0))
    full = lambda a, b: pl.BlockSpec((a, b), lambda i: (0, 0))
    mod = _mod_spec(per_row, tm, rows_per_seq)
    return pl.pallas_call(
        functools.partial(_outproj_kernel, alpha=alpha),
        out_shape=(jax.ShapeDtypeStruct((r, D_MODEL), F32), jax.ShapeDtypeStruct((r, PACKED), jnp.uint32)),
        grid=(r // tm,),
        in_specs=[row(D_A), row(D_B), row(D_MODEL), row(D_MODEL), row(D_MODEL), mod, mod, mod,
                  full(D_A, D_MODEL), full(D_B, D_MODEL), full(D_MODEL, D_MODEL),
                  full(1, D_MODEL), full(1, D_MODEL)],
        out_specs=(row(D_MODEL), row(PACKED)),
        compiler_params=_cparams(("parallel",)),
        name="outproj",
    )(oa, ob, sga, sgb, x, gate1, shift2, scale2, pa16, pb16, wo16, ln_g.reshape(1, -1), ln_b.reshape(1, -1))


def _first_index_of_max(x, iota, axis, size):
    m = jnp.max(x, axis=axis, keepdims=True)
    idx = jnp.min(jnp.where(x == m, iota, size), axis=axis, keepdims=True)
    return m, idx


def _router_kernel(h_ref, wr_hi_ref, wr_lo_ref, eb_ref, idx_ref, wt_ref, rank_ref, cnt_ref, run_ref, *, tr):
    step = pl.program_id(0)

    @pl.when(step == 0)
    def _():
        run_ref[...] = jnp.zeros_like(run_ref)

    ha, hb = _unpack_pair(h_ref[...])
    wr_hi, wr_lo = wr_hi_ref[...], wr_lo_ref[...]
    logits = (_dot_nt(wr_hi[:, :PACKED], ha) + _dot_nt(wr_hi[:, PACKED:], hb)
              + _dot_nt(wr_lo[:, :PACKED], ha) + _dot_nt(wr_lo[:, PACKED:], hb))
    scores = jax.nn.sigmoid(logits)
    biased = scores + eb_ref[...]

    x3 = biased.reshape(N_GROUP, GROUP_SIZE, tr)
    io3 = lax.broadcasted_iota(jnp.int32, x3.shape, 1)
    m1, i1 = _first_index_of_max(x3, io3, 1, GROUP_SIZE)
    m2 = jnp.max(jnp.where(io3 == i1, -jnp.inf, x3), axis=1, keepdims=True)
    grp = (m1 + m2).reshape(N_GROUP, tr)

    iog = lax.broadcasted_iota(jnp.int32, grp.shape, 0)
    gsel = jnp.zeros(grp.shape, jnp.bool_)
    for _ in range(TOPK_GROUP):
        _, gi = _first_index_of_max(grp, iog, 0, N_GROUP)
        hit = iog == gi
        gsel = gsel | hit
        grp = jnp.where(hit, -jnp.inf, grp)
    emask = jnp.broadcast_to(gsel.reshape(N_GROUP, 1, tr), (N_GROUP, GROUP_SIZE, tr)).reshape(N_EXPERTS, tr)
    cand = jnp.where(emask, biased, -jnp.inf)

    ioe = lax.broadcasted_iota(jnp.int32, cand.shape, 0)
    hits = []
    sel = jnp.zeros(cand.shape, jnp.bool_)
    for _ in range(TOP_K):
        _, ei = _first_index_of_max(cand, ioe, 0, N_EXPERTS)
        hit = ioe == ei
        hits.append((ei, hit))
        sel = sel | hit
        cand = jnp.where(hit, -jnp.inf, cand)
    self32 = sel.astype(F32)
    ra = lax.broadcasted_iota(jnp.int32, (tr, tr), 0)
    rb = lax.broadcasted_iota(jnp.int32, (tr, tr), 1)
    upper = (ra < rb).astype(BF16)
    ranks = run_ref[...] + _dot(self32.astype(BF16), upper)

    ws = [jnp.sum(jnp.where(hit, scores, 0.0), axis=0, keepdims=True) for _, hit in hits]
    wsum = ws[0]
    for w in ws[1:]:
        wsum = wsum + w
    for k, (ei, hit) in enumerate(hits):
        idx_ref[k:k + 1, :] = ei
        wt_ref[k:k + 1, :] = ws[k] / wsum * ROUTED_SCALE
        rank_ref[k:k + 1, :] = jnp.sum(jnp.where(hit, ranks, 0.0), axis=0, keepdims=True).astype(jnp.int32)
    run_new = run_ref[...] + jnp.sum(self32, axis=1, keepdims=True)
    run_ref[...] = run_new
    cnt_ref[...] = jnp.broadcast_to(run_new, cnt_ref.shape).astype(jnp.int32)


def _router(h_all, w_router, e_bias, *, tr):
    t = h_all.shape[0]
    wr_t = w_router.T
    hi32 = lax.bitcast_convert_type(lax.bitcast_convert_type(wr_t, jnp.uint32) & jnp.uint32(0xFFFF0000), F32)
    wr_hi = hi32.astype(BF16)
    wr_lo = (wr_t - hi32).astype(BF16)
    full = lambda a, b: pl.BlockSpec((a, b), lambda i: (0, 0))
    col = lambda: pl.BlockSpec((TOP_K, tr), lambda i: (0, i))
    return pl.pallas_call(
        functools.partial(_router_kernel, tr=tr),
        out_shape=(jax.ShapeDtypeStruct((TOP_K, t), jnp.int32), jax.ShapeDtypeStruct((TOP_K, t), F32),
                   jax.ShapeDtypeStruct((TOP_K, t), jnp.int32), jax.ShapeDtypeStruct((N_EXPERTS, 128), jnp.int32)),
        grid=(t // tr,),
        in_specs=[pl.BlockSpec((tr, PACKED), lambda i: (i, 0)), full(N_EXPERTS, D_MODEL),
                  full(N_EXPERTS, D_MODEL), full(N_EXPERTS, 1)],
        out_specs=(col(), col(), col(), full(N_EXPERTS, 128)),
        scratch_shapes=[pltpu.VMEM((N_EXPERTS, 1), F32)],
        compiler_params=_cparams(("arbitrary",)),
        name="router",
    )(h_all, wr_hi, wr_lo, e_bias.reshape(N_EXPERTS, 1).astype(F32))


def _dest_kernel(ps_ref, idx_ref, rank_ref, dest_ref):
    idx = idx_ref[...]
    base = jnp.zeros(idx.shape, jnp.int32)
    for e in range(N_EXPERTS):
        base = jnp.where(idx == e, ps_ref[e], base)
    dest_ref[...] = base + rank_ref[...]


def _dest(pad_start, idx_t, rank_t, *, td):
    t = idx_t.shape[1]
    col = lambda: pl.BlockSpec((TOP_K, td), lambda i: (0, i))
    return pl.pallas_call(
        _dest_kernel,
        out_shape=jax.ShapeDtypeStruct((TOP_K, t), jnp.int32),
        grid=(t // td,),
        in_specs=[pl.BlockSpec(memory_space=pltpu.SMEM), col(), col()],
        out_specs=col(),
        compiler_params=_cparams(("parallel",)),
        name="dest",
    )(pad_start, idx_t, rank_t)


def _row_copy(src_ref, src_row, dst_ref, dst_row, sem):
    return pltpu.make_async_copy(src_ref.at[pl.ds(src_row, 1)], dst_ref.at[pl.ds(dst_row, 1)], sem)


def _dispatch_kernel(fill_lo_ref, fill_hi_ref, dest_hbm, h_ref, xs_hbm, dest_smem, zero_ref, dsem, rsem, *, tm):
    i = pl.program_id(0)
    n = pl.num_programs(0)
    slot = i % 2

    def dest_copy(tile, s):
        return pltpu.make_async_copy(dest_hbm.at[tile], dest_smem.at[s], dsem.at[s])

    @pl.when(i == 0)
    def _():
        dest_copy(0, 0).start()
        zero_ref[...] = jnp.zeros_like(zero_ref)
        for e in range(N_EXPERTS):
            lo, hi = fill_lo_ref[e], fill_hi_ref[e]

            def fill(r, c):
                _row_copy(zero_ref, 0, xs_hbm, r, rsem).start()
                return c

            def drain(r, c):
                _row_copy(zero_ref, 0, xs_hbm, r, rsem).wait()
                return c

            lax.fori_loop(lo, hi, fill, 0)
            lax.fori_loop(lo, hi, drain, 0)

        first_free = fill_hi_ref[N_EXPERTS - 1] // zero_ref.shape[0]

        def tail_copy(b):
            start = pl.multiple_of(b * zero_ref.shape[0], zero_ref.shape[0])
            return pltpu.make_async_copy(zero_ref, xs_hbm.at[pl.ds(start, zero_ref.shape[0])], rsem)

        def fill_tail(b, c):
            tail_copy(b).start()
            return c

        def drain_tail(b, c):
            tail_copy(b).wait()
            return c

        lax.fori_loop(first_free, xs_hbm.shape[0] // zero_ref.shape[0], fill_tail, 0)
        lax.fori_loop(first_free, xs_hbm.shape[0] // zero_ref.shape[0], drain_tail, 0)

    dest_copy(i, slot).wait()

    @pl.when(i + 1 < n)
    def _():
        dest_copy(i + 1, 1 - slot).start()

    def issue(t, c):
        for k in range(TOP_K):
            _row_copy(h_ref, t, xs_hbm, dest_smem[slot, k, t], rsem).start(priority=k % 2)
        return c

    lax.fori_loop(0, tm, issue, 0)
    for _ in range(TOP_K):
        pltpu.make_async_copy(h_ref, xs_hbm.at[pl.ds(0, tm)], rsem).wait()


def _dispatch(h_all, dest3, fill_lo, fill_hi, n_slots, *, tm):
    t = h_all.shape[0]
    return pl.pallas_call(
        functools.partial(_dispatch_kernel, tm=tm),
        out_shape=jax.ShapeDtypeStruct((n_slots, PACKED), jnp.uint32),
        grid_spec=pltpu.PrefetchScalarGridSpec(
            num_scalar_prefetch=2,
            grid=(t // tm,),
            in_specs=[pl.BlockSpec(memory_space=pl.ANY),
                      pl.BlockSpec((tm, PACKED), lambda i, lo, hi: (i, 0))],
            out_specs=pl.BlockSpec(memory_space=pl.ANY),
            scratch_shapes=[pltpu.SMEM((2, TOP_K, tm), jnp.int32), pltpu.VMEM((EXPERT_BLOCK, PACKED), jnp.uint32),
                            pltpu.SemaphoreType.DMA((2,)), pltpu.SemaphoreType.DMA(())],
        ),
        compiler_params=_cparams(("arbitrary",)),
        name="dispatch",
    )(fill_lo, fill_hi, dest3, h_all)


def _experts_kernel(be_ref, nu_ref, x_ref, w13_ref, w2_ref, y_ref):
    @pl.when(pl.program_id(0) < nu_ref[0])
    def _():
        a = _dot_packed(x_ref[...], w13_ref[0])
        hid = (a[:, :D_EXPERT] * jax.nn.sigmoid(a[:, :D_EXPERT])) * a[:, D_EXPERT:]
        y = _dot(hid.astype(BF16), w2_ref[0])
        y_ref[...] = _pack_pair(y[:, :PACKED], y[:, PACKED:])

    @pl.when(pl.program_id(0) >= nu_ref[0])
    def _():
        y_ref[...] = jnp.zeros_like(y_ref)


def _experts(xs, w13, w2, block_e, n_used):
    n_slots = xs.shape[0]
    nb = n_slots // EXPERT_BLOCK
    last = lambda i, nu: jnp.minimum(i, nu[0] - 1)
    return pl.pallas_call(
        _experts_kernel,
        out_shape=jax.ShapeDtypeStruct((n_slots, PACKED), jnp.uint32),
        grid_spec=pltpu.PrefetchScalarGridSpec(
            num_scalar_prefetch=2,
            grid=(nb,),
            in_specs=[pl.BlockSpec((EXPERT_BLOCK, PACKED), lambda i, be, nu: (last(i, nu), 0)),
                      pl.BlockSpec((1, D_MODEL, 2 * D_EXPERT), lambda i, be, nu: (be[last(i, nu)], 0, 0)),
                      pl.BlockSpec((1, D_EXPERT, D_MODEL), lambda i, be, nu: (be[last(i, nu)], 0, 0))],
            out_specs=pl.BlockSpec((EXPERT_BLOCK, PACKED), lambda i, be, nu: (i, 0)),
        ),
        compiler_params=_cparams(("arbitrary",)),
        name="experts",
    )(block_e, n_used, xs, w13, w2)


def _final_kernel(dest_hbm, yb_hbm, h_ref, x1_ref, wt_ref, g2_ref, ws13_ref, ws2_ref, lng_ref, lnb_ref, o_ref,
                  dest_smem, ybuf, dsem, rsem, *, alpha, tm, off):
    i = pl.program_id(0)
    n = pl.num_programs(0)
    slot = i % 2

    def dest_copy(tile, s):
        return pltpu.make_async_copy(dest_hbm.at[tile + off], dest_smem.at[s], dsem.at[s])

    def gather_tile(s):
        def issue(t, c):
            for k in range(TOP_K):
                _row_copy(yb_hbm, dest_smem[s, k, t], ybuf.at[s, k], t, rsem.at[s]).start(priority=k % 2)
            return c

        lax.fori_loop(0, tm, issue, 0)

    @pl.when(i == 0)
    def _():
        dest_copy(0, 0).start()
        dest_copy(0, 0).wait()
        gather_tile(0)

        @pl.when(n > 1)
        def _():
            dest_copy(1, 1).start()

    @pl.when(i + 1 < n)
    def _():
        dest_copy(i + 1, 1 - slot).wait()
        gather_tile(1 - slot)

        @pl.when(i + 2 < n)
        def _():
            dest_copy(i + 2, slot).start()

    a = _dot_packed(h_ref[...], ws13_ref[...])
    hid = (a[:, :D_EXPERT] * jax.nn.sigmoid(a[:, :D_EXPERT])) * a[:, D_EXPERT:]
    y = _dot(hid.astype(BF16), ws2_ref[...])

    for k in range(TOP_K):
        pltpu.make_async_copy(yb_hbm.at[pl.ds(0, tm)], ybuf.at[slot, k], rsem.at[slot]).wait()
    wt = wt_ref[...]
    ya = jnp.zeros((tm, PACKED), F32)
    yb = jnp.zeros((tm, PACKED), F32)
    for k in range(TOP_K):
        ea, eb = _unpack_pair(ybuf[slot, k])
        ya = ya + wt[:, k:k + 1] * ea.astype(F32)
        yb = yb + wt[:, k:k + 1] * eb.astype(F32)
    y = y + jnp.concatenate([ya, yb], axis=1)
    o_ref[...] = _ln(alpha * x1_ref[...] + _rows(g2_ref) * y) * lng_ref[...] + lnb_ref[...]


def _final(dest3, yb, h_all, x1, wt, gate2, ws13, ws2, ln_g, ln_b, *, tm, row0, rows_per_seq, per_row, alpha):
    r = x1.shape[0]
    off = row0 // tm
    row = lambda w: pl.BlockSpec((tm, w), lambda i: (i, 0))
    row_off = lambda w: pl.BlockSpec((tm, w), lambda i: (i + off, 0))
    full = lambda a, b: pl.BlockSpec((a, b), lambda i: (0, 0))
    any_spec = pl.BlockSpec(memory_space=pl.ANY)
    return pl.pallas_call(
        functools.partial(_final_kernel, alpha=alpha, tm=tm, off=off),
        out_shape=jax.ShapeDtypeStruct((r, D_MODEL), F32),
        grid=(r // tm,),
        in_specs=[any_spec, any_spec, row_off(PACKED), row(D_MODEL), row_off(TOP_K),
                  _mod_spec(per_row, tm, rows_per_seq),
                  full(D_MODEL, 2 * D_EXPERT), full(D_EXPERT, D_MODEL), full(1, D_MODEL), full(1, D_MODEL)],
        out_specs=row(D_MODEL),
        scratch_shapes=[pltpu.SMEM((2, TOP_K, tm), jnp.int32), pltpu.VMEM((2, TOP_K, tm, PACKED), jnp.uint32),
                        pltpu.SemaphoreType.DMA((2,)), pltpu.SemaphoreType.DMA((2,))],
        compiler_params=_cparams(("arbitrary",)),
        name="final",
    )(dest3, yb, h_all, x1, wt, gate2, ws13, ws2, ln_g.reshape(1, -1), ln_b.reshape(1, -1))


def _pick_tile(n, pref):
    t = min(pref, n)
    while n % t:
        t //= 2
    return t


def kernel(x_prompt, x_sample, cache_k_diff, cache_v_diff, cache_k_band, cache_v_band, c_prompt, c_sample,
           w_ada, b_ada, w_in, b_in, lambda_q1, lambda_k1, lambda_q2, lambda_k2, subln_g, rel_bias, p_a, p_b,
           w_out, ln1_g, ln1_b, w_router, e_bias, w1, w3, w2, ws1, ws3, ws2, ln2_g, ln2_b):
    depth = w_in.shape[0]
    alpha = (2 * depth) ** 0.25
    bp, sp, d = x_prompt.shape
    bs, ss, _ = x_sample.shape
    past = cache_k_diff.shape[2]
    lb = cache_k_band.shape[2]
    tp, ts_ = bp * sp, bs * ss
    t_all = tp + ts_
    slopes = (2.0 ** (-8.0 * np.arange(1, H_A + 1) / H_A)).astype(np.float32)
    cslopes = jnp.asarray(np.float32(LOG2E) * slopes)

    yp = x_prompt.reshape(tp, d)
    ys = x_sample.reshape(ts_, d)
    outs = [[] for _ in range(8)]
    tail_p = min(BAND_CHUNKS * CHUNK, sp)
    tm_p = _pick_tile(math.gcd(sp, tail_p), 256)
    tq = _pick_tile(sp, 256)
    ta = _pick_tile(sp, 512)
    ka_aug = _key_bias_columns(slopes, sp)

    for l in range(depth):
        lam_init = 0.8 - 0.6 * math.exp(-0.3 * l)
        lam4 = jnp.stack([lambda_q1[l], lambda_k1[l], lambda_q2[l], lambda_k2[l]]).astype(F32)
        w_in16 = w_in[l].astype(BF16)
        b_in2 = b_in[l].reshape(1, N_IN)
        pa16, pb16, wo16 = p_a[l].astype(BF16), p_b[l].astype(BF16), w_out[l].astype(BF16)

        mod = _ada(jnp.concatenate([c_prompt, c_sample], axis=0), w_ada[l], b_ada[l])
        mod_p = [m.reshape(bp, 1, d) for m in jnp.split(mod[:bp], 6, axis=-1)]
        mod_s = [jnp.repeat(m, ss, axis=0) for m in jnp.split(mod[bp:], 6, axis=-1)]

        (qa, ka32, va32, ka16, va16, qb, kb16, vb16, sga, sgb, kbt, vbt) = _inproj(
            yp, mod_p[0], mod_p[1], w_in16, b_in2, tm=tm_p, rows_per_seq=sp, tail_rows=tail_p, per_row=False)
        r3 = lambda a: a.reshape(bp, sp, a.shape[-1])
        oa = _diffattn(r3(qa), r3(ka16), ka_aug, r3(va16), cslopes, lam4, subln_g[l], lam_init, t=ta)
        ob = _bandattn(r3(qb), r3(kb16), r3(vb16), _band_bias_tiles(rel_bias[l], sp, tq), tq=tq)
        x1p, hp = _outproj(oa.reshape(tp, D_A), ob.reshape(tp, D_B), sga, sgb, yp, mod_p[2], mod_p[3], mod_p[4],
                           pa16, pb16, wo16, ln1_g[l], ln1_b[l], tm=_pick_tile(sp, 512), rows_per_seq=sp,
                           per_row=False, alpha=alpha)
        outs[0].append(ka32.reshape(bp, sp, H_A, 2, HEAD_DIM))
        outs[1].append(va32.reshape(bp, sp, H_A, 2 * HEAD_DIM))
        outs[2].append(kbt.reshape(bp, tail_p, H_B, HEAD_DIM))
        outs[3].append(vbt.reshape(bp, tail_p, H_B, HEAD_DIM))

        (qa_s, ka32_s, va32_s, ka16_s, va16_s, qb_s, kb16_s, vb16_s, sga_s, sgb_s, kbt_s, vbt_s) = _inproj(
            ys, mod_s[0], mod_s[1], w_in16, b_in2, tm=ts_, rows_per_seq=ts_, tail_rows=ts_, per_row=True)
        s3 = lambda a: a.reshape(bs, ss, a.shape[-1])
        oa_s, ob_s = _sampattn(
            s3(qa_s), cache_k_diff[l].reshape(bs, past, D_A), cache_v_diff[l].reshape(bs, past, D_A),
            s3(ka16_s), s3(va16_s), s3(qb_s), cache_k_band[l].reshape(bs, lb, D_B),
            cache_v_band[l].reshape(bs, lb, D_B), s3(kb16_s), s3(vb16_s), cslopes, lam4, subln_g[l],
            _sample_band_bias(rel_bias[l], ss, past, lb), lam_init)
        x1s, hs = _outproj(oa_s.reshape(ts_, D_A), ob_s.reshape(ts_, D_B), sga_s, sgb_s, ys, mod_s[2], mod_s[3],
                           mod_s[4], pa16, pb16, wo16, ln1_g[l], ln1_b[l], tm=ts_, rows_per_seq=ts_, per_row=True,
                           alpha=alpha)
        outs[4].append(ka32_s.reshape(bs, ss, H_A, 2, HEAD_DIM))
        outs[5].append(va32_s.reshape(bs, ss, H_A, 2 * HEAD_DIM))
        outs[6].append(kbt_s.reshape(bs, ss, H_B, HEAD_DIM))
        outs[7].append(vbt_s.reshape(bs, ss, H_B, HEAD_DIM))

        h_all = jnp.concatenate([hp, hs], axis=0)
        tr = _pick_tile(t_all, 512)
        idx_t, wt_t, rank_t, cnt = _router(h_all, w_router[l], e_bias[l], tr=tr)
        counts = cnt[:, 0]
        padded = (counts + EXPERT_BLOCK - 1) // EXPERT_BLOCK * EXPERT_BLOCK
        pad_end = jnp.cumsum(padded)
        pad_start = pad_end - padded
        dest = _dest(pad_start.astype(jnp.int32), idx_t, rank_t, td=_pick_tile(t_all, 2048))
        nb = -(-(t_all * TOP_K) // EXPERT_BLOCK) + N_EXPERTS
        n_slots = nb * EXPERT_BLOCK
        block_starts = jnp.arange(nb, dtype=jnp.int32) * EXPERT_BLOCK
        block_e = jnp.minimum(jnp.sum(pad_end[None, :] <= block_starts[:, None], axis=1),
                              N_EXPERTS - 1).astype(jnp.int32)
        n_used = (pad_end[-1:] // EXPERT_BLOCK).astype(jnp.int32)
        tiled = lambda tm: dest.reshape(TOP_K, t_all // tm, tm).transpose(1, 0, 2)
        tm_d = _pick_tile(math.gcd(sp, t_all), 256)
        xs = _dispatch(h_all, tiled(tm_d), (pad_start + counts).astype(jnp.int32), pad_end.astype(jnp.int32),
                       n_slots, tm=tm_d)
        w13 = jnp.concatenate([w1[l], w3[l]], axis=-1).astype(BF16)
        yb = _experts(xs, w13, w2[l].astype(BF16), block_e, n_used)
        wt = wt_t.T
        ws13 = jnp.concatenate([ws1[l], ws3[l]], axis=-1).astype(BF16)
        ws2b = ws2[l].astype(BF16)
        yp = _final(tiled(tm_d), yb, h_all, x1p, wt, mod_p[5], ws13, ws2b, ln2_g[l], ln2_b[l], tm=tm_d, row0=0,
                    rows_per_seq=sp, per_row=False, alpha=alpha)
        ys = _final(tiled(ts_), yb, h_all, x1s, wt, mod_s[5], ws13, ws2b, ln2_g[l], ln2_b[l], tm=ts_, row0=tp,
                    rows_per_seq=ts_, per_row=True, alpha=alpha)

    return (yp.reshape(bp, sp, d), ys.reshape(bs, ss, d)) + tuple(jnp.stack(o) for o in outs)
```

```python
import functools
import math

import jax
import jax.numpy as jnp
import numpy as np
from jax import lax
from jax.experimental import pallas as pl
from jax.experimental.pallas import tpu as pltpu

F32 = jnp.float32
BF16 = jnp.bfloat16

D_MODEL = 1024
CHUNK = 64
HEAD_DIM = 64
H_A = 8
H_B = 8
D_A = H_A * 2 * HEAD_DIM
D_B = H_B * HEAD_DIM
BAND_CHUNKS = 8
REL_CLIP = 128
N_IN = 3 * D_A + 3 * D_B + 2 * D_MODEL
N_EXPERTS = 64
TOP_K = 8
N_GROUP = 8
TOPK_GROUP = 4
GROUP_SIZE = N_EXPERTS // N_GROUP
D_EXPERT = 256
ROUTED_SCALE = 2.5
EXPERT_BLOCK = 512
ROW_ALIGN = 8
SORT_CHUNK = 256
LN_EPS = 1e-5
NEG = -1e30
LOG2E = math.log2(math.e)
Q_SCALE = HEAD_DIM ** -0.5 * LOG2E

VMEM_LIMIT = 56 * 1024 * 1024


def _cparams(sem):
    return pltpu.CompilerParams(dimension_semantics=sem, vmem_limit_bytes=VMEM_LIMIT)


def _ln(x):
    mu = jnp.mean(x, axis=-1, keepdims=True)
    xc = x - mu
    var = jnp.mean(xc * xc, axis=-1, keepdims=True)
    return xc * lax.rsqrt(var + LN_EPS)


def _rows(ref):
    v = ref[...]
    return v.reshape(v.shape[-2], v.shape[-1])


def _split_bf16(x):
    hi = x.astype(BF16)
    lo = (x - hi.astype(F32)).astype(BF16)
    return hi, lo


def _dot(a, b):
    return jnp.dot(a, b, preferred_element_type=F32)


def _dot_nt(a, b):
    return lax.dot_general(a, b, (((1,), (1,)), ((), ())), preferred_element_type=F32)


PACKED = D_MODEL // 2


def _pack_pair(a, b):
    ua = lax.bitcast_convert_type(a.astype(BF16).astype(F32), jnp.uint32)
    ub = lax.bitcast_convert_type(b.astype(BF16).astype(F32), jnp.uint32)
    return ua | (ub >> 16)


def _unpack_pair(u):
    a = lax.bitcast_convert_type(u & jnp.uint32(0xFFFF0000), F32).astype(BF16)
    b = lax.bitcast_convert_type(u << 16, F32).astype(BF16)
    return a, b


def _dot_packed(u, w):
    a, b = _unpack_pair(u)
    return _dot(a, w[:PACKED]) + _dot(b, w[PACKED:])


def _ada_kernel(c_ref, w_ref, b_ref, o_ref):
    c = c_ref[...]
    a = c * jax.nn.sigmoid(c)
    a_hi, a_lo = _split_bf16(a)
    w_hi, w_lo = _split_bf16(w_ref[...])
    o_ref[...] = _dot(a_hi, w_hi) + _dot(a_lo, w_hi) + _dot(a_hi, w_lo) + b_ref[...]


def _ada(c, w_ada, b_ada):
    n, d = c.shape
    nout = w_ada.shape[1]
    tn = 1024
    return pl.pallas_call(
        _ada_kernel,
        out_shape=jax.ShapeDtypeStruct((n, nout), F32),
        grid=(nout // tn,),
        in_specs=[pl.BlockSpec((n, d), lambda j: (0, 0)),
                  pl.BlockSpec((d, tn), lambda j: (0, j)),
                  pl.BlockSpec((1, tn), lambda j: (0, j))],
        out_specs=pl.BlockSpec((n, tn), lambda j: (0, j)),
        compiler_params=_cparams(("parallel",)),
        name="ada",
    )(c, w_ada, b_ada.reshape(1, nout))


_SEG_QA = (0, D_A)
_SEG_KA = (D_A, 2 * D_A)
_SEG_VA = (2 * D_A, 3 * D_A)
_SEG_QB = (3 * D_A, 3 * D_A + D_B)
_SEG_KB = (3 * D_A + D_B, 3 * D_A + 2 * D_B)
_SEG_VB = (3 * D_A + 2 * D_B, 3 * D_A + 3 * D_B)
_SEG_GA = (3 * D_A + 3 * D_B, 3 * D_A + 3 * D_B + D_MODEL)
_SEG_GB = (3 * D_A + 3 * D_B + D_MODEL, N_IN)


def _inproj_kernel(x_ref, sh_ref, sc_ref, w_ref, b_ref,
                   qa_ref, ka32_ref, va32_ref, ka16_ref, va16_ref,
                   qb_ref, kb16_ref, vb16_ref, sga_ref, sgb_ref, kbt_ref, vbt_ref):
    u = (_ln(x_ref[...]) * (1.0 + _rows(sc_ref)) + _rows(sh_ref)).astype(BF16)

    def seg(lo_hi):
        lo, hi = lo_hi
        return _dot(u, w_ref[:, lo:hi]) + b_ref[:, lo:hi]

    qa_ref[...] = (seg(_SEG_QA) * Q_SCALE).astype(BF16)
    ka = seg(_SEG_KA)
    ka32_ref[...] = ka
    ka16_ref[...] = ka.astype(BF16)
    va = seg(_SEG_VA)
    va32_ref[...] = va
    va16_ref[...] = va.astype(BF16)
    qb_ref[...] = (seg(_SEG_QB) * Q_SCALE).astype(BF16)
    kb = seg(_SEG_KB)
    kbt_ref[...] = kb
    kb16_ref[...] = kb.astype(BF16)
    vb = seg(_SEG_VB)
    vbt_ref[...] = vb
    vb16_ref[...] = vb.astype(BF16)
    sga_ref[...] = jax.nn.sigmoid(seg(_SEG_GA)).astype(BF16)
    sgb_ref[...] = jax.nn.sigmoid(seg(_SEG_GB)).astype(BF16)


def _mod_spec(per_row, tm, rows_per_seq):
    if per_row:
        return pl.BlockSpec((tm, D_MODEL), lambda i: (i, 0))
    tiles_per_seq = rows_per_seq // tm
    return pl.BlockSpec((1, 1, D_MODEL), lambda i: (i // tiles_per_seq, 0, 0))


def _inproj(x, shift, scale, w_in16, b_in, *, tm, rows_per_seq, tail_rows, per_row):
    r = x.shape[0]
    nseq = r // rows_per_seq
    n_i = rows_per_seq // tm
    n_t = tail_rows // tm

    def tail_map(i):
        return ((i // n_i) * n_t + jnp.maximum(i % n_i - (n_i - n_t), 0), 0)

    row = lambda w: pl.BlockSpec((tm, w), lambda i: (i, 0))
    out_shape = (
        jax.ShapeDtypeStruct((r, D_A), BF16),
        jax.ShapeDtypeStruct((r, D_A), F32),
        jax.ShapeDtypeStruct((r, D_A), F32),
        jax.ShapeDtypeStruct((r, D_A), BF16),
        jax.ShapeDtypeStruct((r, D_A), BF16),
        jax.ShapeDtypeStruct((r, D_B), BF16),
        jax.ShapeDtypeStruct((r, D_B), BF16),
        jax.ShapeDtypeStruct((r, D_B), BF16),
        jax.ShapeDtypeStruct((r, D_MODEL), BF16),
        jax.ShapeDtypeStruct((r, D_MODEL), BF16),
        jax.ShapeDtypeStruct((nseq * tail_rows, D_B), F32),
        jax.ShapeDtypeStruct((nseq * tail_rows, D_B), F32),
    )
    out_specs = (row(D_A), row(D_A), row(D_A), row(D_A), row(D_A),
                 row(D_B), row(D_B), row(D_B), row(D_MODEL), row(D_MODEL),
                 pl.BlockSpec((tm, D_B), tail_map), pl.BlockSpec((tm, D_B), tail_map))
    return pl.pallas_call(
        _inproj_kernel,
        out_shape=out_shape,
        grid=(r // tm,),
        in_specs=[row(D_MODEL), _mod_spec(per_row, tm, rows_per_seq), _mod_spec(per_row, tm, rows_per_seq),
                  pl.BlockSpec((D_MODEL, N_IN), lambda i: (0, 0)),
                  pl.BlockSpec((1, N_IN), lambda i: (0, 0))],
        out_specs=out_specs,
        compiler_params=_cparams(("arbitrary",)),
        name="inproj",
    )(x, shift, scale, w_in16, b_in)


N_AUG = 3


def _lam_value(lam4_ref, lam_init):
    v = lam4_ref[...]
    a = jnp.sum(v[0:1, :] * v[1:2, :], axis=-1, keepdims=True)
    b = jnp.sum(v[2:3, :] * v[3:4, :], axis=-1, keepdims=True)
    return jnp.exp(a) - jnp.exp(b) + lam_init


def _half_masks(shape):
    lane = lax.broadcasted_iota(jnp.int32, shape, 1)
    return lane < HEAD_DIM, lane >= HEAD_DIM


def _stack_halves(q):
    lo, hi = _half_masks(q.shape)
    zero = jnp.zeros_like(q)
    return jnp.concatenate([jnp.where(lo, q, zero), jnp.where(hi, q, zero)], axis=0)


def _softmax_step(s, v, m, l, acc):
    m_new = jnp.maximum(m, jnp.max(s, axis=-1, keepdims=True))
    alpha = jnp.exp2(m - m_new)
    p = jnp.exp2(s - m_new)
    l_new = alpha * l + jnp.sum(p, axis=-1, keepdims=True)
    acc_new = alpha * acc + _dot(p.astype(BF16), v)
    return m_new, l_new, acc_new


def _diff_finish(o0, o1, lam, g, lam_init):
    o = o0 - lam * o1
    o = o * lax.rsqrt(jnp.mean(o * o, axis=-1, keepdims=True) + LN_EPS)
    return o * g * (1.0 - lam_init)


def _key_bias_columns(slopes, s):
    x = (np.float32(LOG2E) * slopes.astype(np.float32))[:, None] * np.arange(s, dtype=np.float32)[None, :]
    cols = np.zeros(x.shape + (2 * HEAD_DIM,), np.float32)
    for c in range(N_AUG):
        cols[:, :, c] = x.astype(BF16).astype(np.float32)
        x = x - cols[:, :, c]
    return jnp.asarray(cols, BF16)


def _diffattn_kernel(cs_ref, lam4_ref, g_ref, q_ref, k_ref, ka_ref, v_ref, o_ref, *, t, lam_init):
    h = pl.program_id(1)
    qi = pl.program_id(2)
    cs = cs_ref[h]
    q = q_ref[0]
    lo, hi = _half_masks(q.shape)
    lane = lax.broadcasted_iota(jnp.int32, q.shape, 1)
    zero = jnp.zeros_like(q)
    ones = (lane < N_AUG).astype(BF16)
    q_ext = (jnp.concatenate([jnp.where(lo, q, zero), ones], axis=1),
             jnp.concatenate([jnp.where(hi, q, zero), ones], axis=1))

    def block(j, carry, extra):
        start = pl.multiple_of(j * t, t)
        k_ext = jnp.concatenate([k_ref[0, pl.ds(start, t), :], ka_ref[0, pl.ds(start, t), :]], axis=1)
        vj = v_ref[0, pl.ds(start, t), :]
        out = []
        for half in range(2):
            s = _dot_nt(q_ext[half], k_ext)
            if extra is not None:
                s = s + extra
            out.append(_softmax_step(s, vj, *carry[half]))
        return tuple(out)

    init1 = (jnp.full((t, 1), NEG, F32), jnp.zeros((t, 1), F32), jnp.zeros((t, 2 * HEAD_DIM), F32))
    carry = lax.fori_loop(0, qi, lambda j, c: block(j, c, None), (init1, init1))

    ii = lax.broadcasted_iota(jnp.int32, (t, t), 0)
    jj = lax.broadcasted_iota(jnp.int32, (t, t), 1)
    fix = jnp.where(jj > ii, (2.0 * cs) * (ii - jj).astype(F32), 0.0)
    fix = jnp.where((jj // CHUNK) <= (ii // CHUNK), fix, NEG)
    (m0, l0, a0), (m1, l1, a1) = block(qi, carry, fix)

    lam = _lam_value(lam4_ref, lam_init)
    o_ref[0] = _diff_finish(a0 / l0, a1 / l1, lam, g_ref[...], lam_init).astype(o_ref.dtype)


def _diffattn(qa, ka, ka_aug, va, cslopes, lam4, subln_g, lam_init, *, t):
    b, s, _ = qa.shape
    kern = functools.partial(_diffattn_kernel, t=t, lam_init=lam_init)
    hd2 = 2 * HEAD_DIM
    return pl.pallas_call(
        kern,
        out_shape=jax.ShapeDtypeStruct((b, s, D_A), BF16),
        grid=(b, H_A, s // t),
        in_specs=[pl.BlockSpec(memory_space=pltpu.SMEM),
                  pl.BlockSpec((4, HEAD_DIM), lambda bi, h, i: (0, 0)),
                  pl.BlockSpec((1, hd2), lambda bi, h, i: (0, 0)),
                  pl.BlockSpec((1, t, hd2), lambda bi, h, i: (bi, i, h)),
                  pl.BlockSpec((1, s, hd2), lambda bi, h, i: (bi, 0, h)),
                  pl.BlockSpec((1, s, hd2), lambda bi, h, i: (h, 0, 0)),
                  pl.BlockSpec((1, s, hd2), lambda bi, h, i: (bi, 0, h))],
        out_specs=pl.BlockSpec((1, t, hd2), lambda bi, h, i: (bi, i, h)),
        compiler_params=_cparams(("parallel", "parallel", "arbitrary")),
        name="diffattn",
    )(cslopes, lam4, subln_g.reshape(1, hd2), qa, ka, ka_aug, va)


def _toeplitz_bias(rel_table, rows, cols, delta):
    length = rows + cols - 1
    rel = np.clip(delta + rows - 1 - np.arange(length), -REL_CLIP, REL_CLIP) + REL_CLIP
    u = jnp.take(rel_table.astype(F32) * LOG2E, jnp.asarray(rel, jnp.int32), axis=1)
    u = jnp.pad(u, ((0, 0), (0, 1)))
    r = jnp.tile(u, (1, rows))[:, :rows * length].reshape(-1, rows, length)
    return r[:, :, rows - 1:rows - 1 + cols]


def _band_valid(t_pos, s_pos):
    return (s_pos // CHUNK <= t_pos // CHUNK) & (s_pos // CHUNK >= t_pos // CHUNK - BAND_CHUNKS)


def _band_window(s, tq):
    return min(BAND_CHUNKS * CHUNK + tq, s)


def _band_bias_tiles(rel_table, s, tq):
    win = _band_window(s, tq)
    n_var = min(BAND_CHUNKS * CHUNK // tq + 1, s // tq)
    tiles = []
    for var in range(n_var):
        q0 = var * tq
        k0 = max(q0 - BAND_CHUNKS * CHUNK, 0)
        valid = _band_valid(q0 + np.arange(tq)[:, None], k0 + np.arange(win)[None, :])
        tiles.append(jnp.where(jnp.asarray(valid)[None], _toeplitz_bias(rel_table, tq, win, q0 - k0), NEG))
    return jnp.stack(tiles)


def _pair_softmax_out(qm, kw, vw, bias2, t):
    s = (_dot_nt(qm, kw).reshape(2, t, -1) + bias2).reshape(2 * t, -1)
    m = jnp.max(s, axis=-1, keepdims=True)
    p = jnp.exp2(s - m)
    l = jnp.sum(p, axis=-1, keepdims=True)
    o = _dot(p.astype(BF16), vw) / l
    lo, _ = _half_masks((t, 2 * HEAD_DIM))
    return jnp.where(lo, o[:t], o[t:])


def _bandattn_kernel(bias_ref, q_ref, k_ref, v_ref, o_ref, *, tq, win, n_var):
    qi = pl.program_id(2)
    var = jnp.minimum(qi, n_var - 1)
    k0 = pl.multiple_of(jnp.maximum(qi * tq - BAND_CHUNKS * CHUNK, 0), tq)
    qm = _stack_halves(q_ref[0])
    kw = k_ref[0, pl.ds(k0, win), :]
    vw = v_ref[0, pl.ds(k0, win), :]
    o_ref[0] = _pair_softmax_out(qm, kw, vw, bias_ref[var], tq).astype(o_ref.dtype)


def _bandattn(qb, kb, vb, bias_tiles, *, tq):
    b, s, _ = qb.shape
    n_var, _, _, win = bias_tiles.shape
    kern = functools.partial(_bandattn_kernel, tq=tq, win=win, n_var=n_var)
    return pl.pallas_call(
        kern,
        out_shape=jax.ShapeDtypeStruct((b, s, D_B), BF16),
        grid=(H_B // 2, b, s // tq),
        in_specs=[pl.BlockSpec((n_var, 2, tq, win), lambda hp, bi, i: (0, hp, 0, 0)),
                  pl.BlockSpec((1, tq, 2 * HEAD_DIM), lambda hp, bi, i: (bi, i, hp)),
                  pl.BlockSpec((1, s, 2 * HEAD_DIM), lambda hp, bi, i: (bi, 0, hp)),
                  pl.BlockSpec((1, s, 2 * HEAD_DIM), lambda hp, bi, i: (bi, 0, hp))],
        out_specs=pl.BlockSpec((1, tq, 2 * HEAD_DIM), lambda hp, bi, i: (bi, i, hp)),
        compiler_params=_cparams(("parallel", "parallel", "arbitrary")),
        name="bandattn",
    )(bias_tiles, qb, kb, vb)


def _two_part_softmax(s_c, s_n, vc, vn):
    m = jnp.maximum(jnp.max(s_c, axis=-1, keepdims=True), jnp.max(s_n, axis=-1, keepdims=True))
    p_c = jnp.exp2(s_c - m)
    p_n = jnp.exp2(s_n - m)
    l = jnp.sum(p_c, axis=-1, keepdims=True) + jnp.sum(p_n, axis=-1, keepdims=True)
    return (_dot(p_c.astype(BF16), vc) + _dot(p_n.astype(BF16), vn)) / l


def _sampattn_kernel(cs_ref, lam4_ref, g_ref, bbias_ref, qa_ref, ckd_ref, cvd_ref, kan_ref, van_ref,
                     qb_ref, ckb_ref, cvb_ref, kbn_ref, vbn_ref, oa_ref, ob_ref, *, n, past, lam_init):
    lam = _lam_value(lam4_ref, lam_init)
    tq_pos = past + lax.broadcasted_iota(jnp.int32, (n, past), 0)
    ts_pos = lax.broadcasted_iota(jnp.int32, (n, past), 1)
    dist_c = jnp.abs(tq_pos - ts_pos).astype(F32)
    vis_c = (ts_pos // CHUNK) <= (tq_pos // CHUNK)
    ii = lax.broadcasted_iota(jnp.int32, (n, n), 0)
    jj = lax.broadcasted_iota(jnp.int32, (n, n), 1)
    dist_n = jnp.abs(ii - jj).astype(F32)
    vis_n = ((past + jj) // CHUNK) <= ((past + ii) // CHUNK)
    for h in range(H_A):
        sl = slice(h * 2 * HEAD_DIM, (h + 1) * 2 * HEAD_DIM)
        cs = cs_ref[h]
        qm = _stack_halves(qa_ref[0, :, sl])
        bias_c = jnp.where(vis_c, -cs * dist_c, NEG)
        bias_n = jnp.where(vis_n, -cs * dist_n, NEG)
        s_c = (_dot_nt(qm, ckd_ref[0, :, sl].astype(BF16)).reshape(2, n, past) + bias_c[None]).reshape(2 * n, past)
        s_n = (_dot_nt(qm, kan_ref[0, :, sl]).reshape(2, n, n) + bias_n[None]).reshape(2 * n, n)
        o = _two_part_softmax(s_c, s_n, cvd_ref[0, :, sl].astype(BF16), van_ref[0, :, sl])
        oa_ref[0, :, sl] = _diff_finish(o[:n], o[n:], lam, g_ref[...], lam_init).astype(oa_ref.dtype)
    lb = ckb_ref.shape[1]
    for hp in range(H_B // 2):
        sl = slice(hp * 2 * HEAD_DIM, (hp + 1) * 2 * HEAD_DIM)
        qm = _stack_halves(qb_ref[0, :, sl])
        bias = bbias_ref[2 * hp:2 * hp + 2]
        s_c = (_dot_nt(qm, ckb_ref[0, :, sl].astype(BF16)).reshape(2, n, lb) + bias[:, :, :lb]).reshape(2 * n, lb)
        s_n = (_dot_nt(qm, kbn_ref[0, :, sl]).reshape(2, n, n) + bias[:, :, lb:]).reshape(2 * n, n)
        o = _two_part_softmax(s_c, s_n, cvb_ref[0, :, sl].astype(BF16), vbn_ref[0, :, sl])
        lo, _ = _half_masks((n, 2 * HEAD_DIM))
        ob_ref[0, :, sl] = jnp.where(lo, o[:n], o[n:]).astype(ob_ref.dtype)


def _sample_band_bias(rel_table, n, past, lb):
    valid = _band_valid(past + np.arange(n)[:, None], (past - lb) + np.arange(lb + n)[None, :])
    return jnp.where(jnp.asarray(valid)[None], _toeplitz_bias(rel_table, n, lb + n, lb), NEG)


def _sampattn(qa, ckd, cvd, kan, van, qb, ckb, cvb, kbn, vbn, cslopes, lam4, subln_g, bbias, lam_init):
    b, n, _ = qa.shape
    past = ckd.shape[1]
    lb = ckb.shape[1]
    kern = functools.partial(_sampattn_kernel, n=n, past=past, lam_init=lam_init)
    full = lambda shape: pl.BlockSpec(shape, lambda bi: (0,) * len(shape))
    per_b = lambda r, w: pl.BlockSpec((1, r, w), lambda bi: (bi, 0, 0))
    return pl.pallas_call(
        kern,
        out_shape=(jax.ShapeDtypeStruct((b, n, D_A), BF16), jax.ShapeDtypeStruct((b, n, D_B), BF16)),
        grid=(b,),
        in_specs=[pl.BlockSpec(memory_space=pltpu.SMEM), full((4, HEAD_DIM)), full((1, 2 * HEAD_DIM)),
                  full((H_B, n, lb + n)),
                  per_b(n, D_A), per_b(past, D_A), per_b(past, D_A), per_b(n, D_A), per_b(n, D_A),
                  per_b(n, D_B), per_b(lb, D_B), per_b(lb, D_B), per_b(n, D_B), per_b(n, D_B)],
        out_specs=(per_b(n, D_A), per_b(n, D_B)),
        compiler_params=_cparams(("parallel",)),
        name="sampattn",
    )(cslopes, lam4, subln_g.reshape(1, 2 * HEAD_DIM), bbias, qa, ckd, cvd, kan, van, qb, ckb, cvb, kbn, vbn)


def _outproj_kernel(oa_ref, ob_ref, sga_ref, sgb_ref, x_ref, g1_ref, sh2_ref, sc2_ref, pa_ref, pb_ref, wo_ref,
                    lng_ref, lnb_ref, x1_ref, h_ref, *, alpha):
    mix = (sga_ref[...].astype(F32) * _dot(oa_ref[...], pa_ref[...])
           + sgb_ref[...].astype(F32) * _dot(ob_ref[...], pb_ref[...]))
    y = _dot(mix.astype(BF16), wo_ref[...])
    x1 = _ln(alpha * x_ref[...] + _rows(g1_ref) * y) * lng_ref[...] + lnb_ref[...]
    x1_ref[...] = x1
    u2 = _ln(x1) * (1.0 + _rows(sc2_ref)) + _rows(sh2_ref)
    h_ref[...] = _pack_pair(u2[:, :PACKED], u2[:, PACKED:])


def _outproj(oa, ob, sga, sgb, x, gate1, shift2, scale2, pa16, pb16, wo16, ln_g, ln_b, *, tm, rows_per_seq,
             per_row, alpha):
    r = x.shape[0]
    row = lambda w: pl.BlockSpec((tm, w), lambda i: (i, 0))
    full = lambda a, b: pl.BlockSpec((a, b), lambda i: (0, 0))
    mod = _mod_spec(per_row, tm, rows_per_seq)
    return pl.pallas_call(
        functools.partial(_outproj_kernel, alpha=alpha),
        out_shape=(jax.ShapeDtypeStruct((r, D_MODEL), F32), jax.ShapeDtypeStruct((r, PACKED), jnp.uint32)),
        grid=(r // tm,),
        in_specs=[row(D_A), row(D_B), row(D_MODEL), row(D_MODEL), row(D_MODEL), mod, mod, mod,
                  full(D_A, D_MODEL), full(D_B, D_MODEL), full(D_MODEL, D_MODEL),
                  full(1, D_MODEL), full(1, D_MODEL)],
        out_specs=(row(D_MODEL), row(PACKED)),
        compiler_params=_cparams(("parallel",)),
        name="outproj",
    )(oa, ob, sga, sgb, x, gate1, shift2, scale2, pa16, pb16, wo16, ln_g.reshape(1, -1), ln_b.reshape(1, -1))


def _first_index_of_max(x, iota, axis, size):
    m = jnp.max(x, axis=axis, keepdims=True)
    idx = jnp.min(jnp.where(x == m, iota, size), axis=axis, keepdims=True)
    return m, idx


def _router_kernel(h_ref, wr_hi_ref, wr_lo_ref, eb_ref, wt_ref, pos_ref, tab_ref, cnt_ref, run_ref, *, tr):
    step = pl.program_id(0)

    @pl.when(step == 0)
    def _():
        run_ref[...] = jnp.zeros_like(run_ref)

    ha, hb = _unpack_pair(h_ref[...])
    wr_hi, wr_lo = wr_hi_ref[...], wr_lo_ref[...]
    logits = (_dot_nt(wr_hi[:, :PACKED], ha) + _dot_nt(wr_hi[:, PACKED:], hb)
              + _dot_nt(wr_lo[:, :PACKED], ha) + _dot_nt(wr_lo[:, PACKED:], hb))
    scores = jax.nn.sigmoid(logits)
    biased = scores + eb_ref[...]

    x3 = biased.reshape(N_GROUP, GROUP_SIZE, tr)
    io3 = lax.broadcasted_iota(jnp.int32, x3.shape, 1)
    m1, i1 = _first_index_of_max(x3, io3, 1, GROUP_SIZE)
    m2 = jnp.max(jnp.where(io3 == i1, -jnp.inf, x3), axis=1, keepdims=True)
    grp = (m1 + m2).reshape(N_GROUP, tr)

    iog = lax.broadcasted_iota(jnp.int32, grp.shape, 0)
    gsel = jnp.zeros(grp.shape, jnp.bool_)
    for _ in range(TOPK_GROUP):
        _, gi = _first_index_of_max(grp, iog, 0, N_GROUP)
        hit = iog == gi
        gsel = gsel | hit
        grp = jnp.where(hit, -jnp.inf, grp)
    emask = jnp.broadcast_to(gsel.reshape(N_GROUP, 1, tr), (N_GROUP, GROUP_SIZE, tr)).reshape(N_EXPERTS, tr)
    cand = jnp.where(emask, biased, -jnp.inf)

    ioe = lax.broadcasted_iota(jnp.int32, cand.shape, 0)
    hits = []
    sel = jnp.zeros(cand.shape, jnp.bool_)
    for _ in range(TOP_K):
        _, ei = _first_index_of_max(cand, ioe, 0, N_EXPERTS)
        hit = ioe == ei
        hits.append((ei, hit))
        sel = sel | hit
        cand = jnp.where(hit, -jnp.inf, cand)
    self32 = sel.astype(F32)
    ra = lax.broadcasted_iota(jnp.int32, (tr, tr), 0)
    rb = lax.broadcasted_iota(jnp.int32, (tr, tr), 1)
    upper = (ra < rb).astype(BF16)
    prefix = _dot(self32.astype(BF16), upper)

    count = jnp.sum(self32, axis=1, keepdims=True)
    run_len = jnp.floor((count + (ROW_ALIGN - 1)) * (1.0 / ROW_ALIGN)) * ROW_ALIGN
    ea = lax.broadcasted_iota(jnp.int32, (N_EXPERTS, N_EXPERTS), 0)
    eb = lax.broadcasted_iota(jnp.int32, (N_EXPERTS, N_EXPERTS), 1)
    before = (eb < ea).astype(BF16)
    local_start = _dot(before, jnp.broadcast_to(run_len, (N_EXPERTS, 128)).astype(BF16))[:, :1]
    local_pos = local_start + prefix

    ws = [jnp.sum(jnp.where(hit, scores, 0.0), axis=0, keepdims=True) for _, hit in hits]
    wsum = ws[0]
    for w in ws[1:]:
        wsum = wsum + w
    for k, (_, hit) in enumerate(hits):
        wt_ref[k:k + 1, :] = ws[k] / wsum * ROUTED_SCALE
        pos_ref[k:k + 1, :] = jnp.sum(jnp.where(hit, local_pos, 0.0), axis=0, keepdims=True).astype(jnp.int32)
    run_old = run_ref[...]
    lane = lax.broadcasted_iota(jnp.int32, (N_EXPERTS, 128), 1)
    tab = jnp.where(lane == 0, run_old, jnp.where(lane == 1, local_start, jnp.where(lane == 2, run_len, 0.0)))
    tab_ref[0] = tab.astype(jnp.int32)
    run_new = run_old + run_len
    run_ref[...] = run_new
    cnt_ref[...] = jnp.broadcast_to(run_new, cnt_ref.shape).astype(jnp.int32)


def _router(h_all, w_router, e_bias, *, tr):
    t = h_all.shape[0]
    wr_t = w_router.T
    hi32 = lax.bitcast_convert_type(lax.bitcast_convert_type(wr_t, jnp.uint32) & jnp.uint32(0xFFFF0000), F32)
    wr_hi = hi32.astype(BF16)
    wr_lo = (wr_t - hi32).astype(BF16)
    full = lambda a, b: pl.BlockSpec((a, b), lambda i: (0, 0))
    col = lambda: pl.BlockSpec((TOP_K, tr), lambda i: (0, i))
    return pl.pallas_call(
        functools.partial(_router_kernel, tr=tr),
        out_shape=(jax.ShapeDtypeStruct((TOP_K, t), F32),
                   jax.ShapeDtypeStruct((TOP_K, t), jnp.int32),
                   jax.ShapeDtypeStruct((t // tr, N_EXPERTS, 128), jnp.int32),
                   jax.ShapeDtypeStruct((N_EXPERTS, 128), jnp.int32)),
        grid=(t // tr,),
        in_specs=[pl.BlockSpec((tr, PACKED), lambda i: (i, 0)), full(N_EXPERTS, D_MODEL),
                  full(N_EXPERTS, D_MODEL), full(N_EXPERTS, 1)],
        out_specs=(col(), col(), pl.BlockSpec((1, N_EXPERTS, 128), lambda i: (i, 0, 0)), full(N_EXPERTS, 128)),
        scratch_shapes=[pltpu.VMEM((N_EXPERTS, 1), F32)],
        compiler_params=_cparams(("arbitrary",)),
        name="router",
    )(h_all, wr_hi, wr_lo, e_bias.reshape(N_EXPERTS, 1).astype(F32))


TAB_GLOBAL, TAB_LOCAL, TAB_LEN = 0, 1, 2


def _sorted_rows(tm):
    need = tm * TOP_K + N_EXPERTS * (ROW_ALIGN - 1)
    return -(-need // SORT_CHUNK) * SORT_CHUNK


def _run(tab_smem, s, e):
    return tuple(pl.multiple_of(tab_smem[s, r * N_EXPERTS + e], ROW_ALIGN) for r in (TAB_GLOBAL, TAB_LOCAL, TAB_LEN))


def _tile_rows(tab_smem, s):
    _, local, length = _run(tab_smem, s, N_EXPERTS - 1)
    return pl.multiple_of(local + length, ROW_ALIGN)


def _rows_copy(src_ref, src_row, dst_ref, dst_row, rows, sem):
    return pltpu.make_async_copy(src_ref.at[pl.ds(src_row, rows)], dst_ref.at[pl.ds(dst_row, rows)], sem)


def _dispatch_kernel(fill_lo_ref, fill_hi_ref, tab_hbm, pos_ref, h_ref, xs_hbm, tab_smem, rows_smem, sorted_ref,
                     zero_ref, tsem, csem, zsem, *, tm):
    i = pl.program_id(0)
    n = pl.num_programs(0)
    slot = i % 2

    def tab_copy(tile, s):
        return pltpu.make_async_copy(tab_hbm.at[tile], tab_smem.at[s], tsem.at[s])

    def wait_runs(s):
        rows = rows_smem[s]

        @pl.when(rows > 0)
        def _():
            _rows_copy(sorted_ref.at[s], 0, xs_hbm, 0, pl.multiple_of(rows, ROW_ALIGN), csem.at[s]).wait()

    @pl.when(i == 0)
    def _():
        tab_copy(0, 0).start()
        zero_ref[...] = jnp.zeros_like(zero_ref)
        blk = zero_ref.shape[0]
        for e in range(N_EXPERTS):
            lo = pl.multiple_of(fill_lo_ref[e], ROW_ALIGN)
            rows = pl.multiple_of(fill_hi_ref[e] - lo, ROW_ALIGN)

            @pl.when(rows > 0)
            def _():
                cp = _rows_copy(zero_ref, 0, xs_hbm, lo, rows, zsem)
                cp.start()
                cp.wait()

        def tail(b, c):
            cp = _rows_copy(zero_ref, 0, xs_hbm, pl.multiple_of(b * blk, blk), blk, zsem)
            cp.start()
            cp.wait()
            return c

        lax.fori_loop(fill_hi_ref[N_EXPERTS - 1] // blk, xs_hbm.shape[0] // blk, tail, 0)

    tab_copy(i, slot).wait()

    @pl.when(i + 1 < n)
    def _():
        tab_copy(i + 1, 1 - slot).start()

    ha, hb = _unpack_pair(h_ref[...])
    pos = pos_ref[...]
    rows = _tile_rows(tab_smem, slot)

    def sort_chunk(c, carry):
        r0 = pl.multiple_of(c * SORT_CHUNK, SORT_CHUNK)
        row = r0 + lax.broadcasted_iota(jnp.int32, (SORT_CHUNK, tm), 0)
        pick = jnp.zeros((SORT_CHUNK, tm), F32)
        for k in range(TOP_K):
            pick = jnp.where(row == pos[k:k + 1, :], 1.0, pick)
        pick = pick.astype(BF16)
        sorted_ref[slot, pl.ds(r0, SORT_CHUNK), :] = _pack_pair(_dot(pick, ha), _dot(pick, hb))
        return carry

    lax.fori_loop(0, (rows + SORT_CHUNK - 1) // SORT_CHUNK, sort_chunk, 0)

    for e in range(N_EXPERTS):
        dst, src, length = _run(tab_smem, slot, e)

        @pl.when(length > 0)
        def _():
            _rows_copy(sorted_ref.at[slot], src, xs_hbm, dst, length, csem.at[slot]).start()

    rows_smem[slot] = rows

    @pl.when(i > 0)
    def _():
        wait_runs(1 - slot)

    @pl.when(i == n - 1)
    def _():
        wait_runs(slot)


def _dispatch(h_all, pos_t, tab, fill_lo, fill_hi, n_slots, *, tm):
    t = h_all.shape[0]
    return pl.pallas_call(
        functools.partial(_dispatch_kernel, tm=tm),
        out_shape=jax.ShapeDtypeStruct((n_slots, PACKED), jnp.uint32),
        grid_spec=pltpu.PrefetchScalarGridSpec(
            num_scalar_prefetch=2,
            grid=(t // tm,),
            in_specs=[pl.BlockSpec(memory_space=pl.ANY),
                      pl.BlockSpec((TOP_K, tm), lambda i, lo, hi: (0, i)),
                      pl.BlockSpec((tm, PACKED), lambda i, lo, hi: (i, 0))],
            out_specs=pl.BlockSpec(memory_space=pl.ANY),
            scratch_shapes=[pltpu.SMEM((2, 3 * N_EXPERTS), jnp.int32), pltpu.SMEM((2,), jnp.int32),
                            pltpu.VMEM((2, _sorted_rows(tm), PACKED), jnp.uint32),
                            pltpu.VMEM((EXPERT_BLOCK, PACKED), jnp.uint32),
                            pltpu.SemaphoreType.DMA((2,)), pltpu.SemaphoreType.DMA((2,)),
                            pltpu.SemaphoreType.DMA(())],
        ),
        compiler_params=_cparams(("arbitrary",)),
        name="dispatch",
    )(fill_lo, fill_hi, tab, pos_t, h_all)


def _experts_kernel(be_ref, nu_ref, x_ref, w13_ref, w2_ref, y_ref):
    @pl.when(pl.program_id(0) < nu_ref[0])
    def _():
        a = _dot_packed(x_ref[...], w13_ref[0])
        hid = (a[:, :D_EXPERT] * jax.nn.sigmoid(a[:, :D_EXPERT])) * a[:, D_EXPERT:]
        y = _dot(hid.astype(BF16), w2_ref[0])
        y_ref[...] = _pack_pair(y[:, :PACKED], y[:, PACKED:])

    @pl.when(pl.program_id(0) >= nu_ref[0])
    def _():
        y_ref[...] = jnp.zeros_like(y_ref)


def _experts(xs, w13, w2, block_e, n_used):
    n_slots = xs.shape[0]
    nb = n_slots // EXPERT_BLOCK
    last = lambda i, nu: jnp.minimum(i, nu[0] - 1)
    return pl.pallas_call(
        _experts_kernel,
        out_shape=jax.ShapeDtypeStruct((n_slots, PACKED), jnp.uint32),
        grid_spec=pltpu.PrefetchScalarGridSpec(
            num_scalar_prefetch=2,
            grid=(nb,),
            in_specs=[pl.BlockSpec((EXPERT_BLOCK, PACKED), lambda i, be, nu: (last(i, nu), 0)),
                      pl.BlockSpec((1, D_MODEL, 2 * D_EXPERT), lambda i, be, nu: (be[last(i, nu)], 0, 0)),
                      pl.BlockSpec((1, D_EXPERT, D_MODEL), lambda i, be, nu: (be[last(i, nu)], 0, 0))],
            out_specs=pl.BlockSpec((EXPERT_BLOCK, PACKED), lambda i, be, nu: (i, 0)),
        ),
        compiler_params=_cparams(("arbitrary",)),
        name="experts",
    )(block_e, n_used, xs, w13, w2)


def _final_kernel(tab_hbm, yb_hbm, h_ref, x1_ref, pos_ref, wt_ref, g2_ref, ws13_ref, ws2_ref, lng_ref, lnb_ref,
                  o_ref, tab_smem, rows_smem, ybuf, tsem, gsem, *, alpha, tm, off):
    i = pl.program_id(0)
    n = pl.num_programs(0)
    slot = i % 2

    def tab_copy(tile, s):
        return pltpu.make_async_copy(tab_hbm.at[tile + off], tab_smem.at[s], tsem.at[s])

    def fetch_runs(s):
        for e in range(N_EXPERTS):
            src, dst, length = _run(tab_smem, s, e)

            @pl.when(length > 0)
            def _():
                _rows_copy(yb_hbm, src, ybuf.at[s], dst, length, gsem.at[s]).start()

        rows_smem[s] = _tile_rows(tab_smem, s)

    @pl.when(i == 0)
    def _():
        ybuf[...] = jnp.zeros_like(ybuf)
        tab_copy(0, 0).start()
        tab_copy(0, 0).wait()
        fetch_runs(0)

        @pl.when(n > 1)
        def _():
            tab_copy(1, 1).start()

    @pl.when(i + 1 < n)
    def _():
        tab_copy(i + 1, 1 - slot).wait()
        fetch_runs(1 - slot)

        @pl.when(i + 2 < n)
        def _():
            tab_copy(i + 2, slot).start()

    a = _dot_packed(h_ref[...], ws13_ref[...])
    hid = (a[:, :D_EXPERT] * jax.nn.sigmoid(a[:, :D_EXPERT])) * a[:, D_EXPERT:]
    y = _dot(hid.astype(BF16), ws2_ref[...])

    rows = rows_smem[slot]

    @pl.when(rows > 0)
    def _():
        _rows_copy(yb_hbm, 0, ybuf.at[slot], 0, pl.multiple_of(rows, ROW_ALIGN), gsem.at[slot]).wait()

    pos = pos_ref[...]
    wt = wt_ref[...]
    lanes = 128
    pos_b = [jnp.broadcast_to(pos[:, k:k + 1], (tm, lanes)) for k in range(TOP_K)]
    wt_b = [jnp.broadcast_to(wt[:, k:k + 1], (tm, lanes)) for k in range(TOP_K)]
    lane_id = lax.broadcasted_iota(jnp.int32, (tm, lanes), 1)

    def combine_chunk(c, acc):
        r0 = pl.multiple_of(c * SORT_CHUNK, SORT_CHUNK)
        parts = []
        for part in range(SORT_CHUNK // lanes):
            col = lane_id + (r0 + part * lanes)
            w = jnp.zeros((tm, lanes), F32)
            for k in range(TOP_K):
                w = jnp.where(col == pos_b[k], wt_b[k], w)
            parts.append(w)
        w_hi, w_lo = _split_bf16(jnp.concatenate(parts, axis=1))
        w2 = jnp.concatenate([w_hi, w_lo], axis=0)
        ea, eb = _unpack_pair(ybuf[slot, pl.ds(r0, SORT_CHUNK), :])
        ra, rb = _dot(w2, ea), _dot(w2, eb)
        return acc[0] + (ra[:tm] + ra[tm:]), acc[1] + (rb[:tm] + rb[tm:])

    zero = jnp.zeros((tm, PACKED), F32)
    ya, yb = lax.fori_loop(0, (rows + SORT_CHUNK - 1) // SORT_CHUNK, combine_chunk, (zero, zero))
    y = y + jnp.concatenate([ya, yb], axis=1)
    o_ref[...] = _ln(alpha * x1_ref[...] + _rows(g2_ref) * y) * lng_ref[...] + lnb_ref[...]


def _final(tab, yb, h_all, x1, pos, wt, gate2, ws13, ws2, ln_g, ln_b, *, tm, row0, rows_per_seq, per_row, alpha):
    r = x1.shape[0]
    off = row0 // tm
    row = lambda w: pl.BlockSpec((tm, w), lambda i: (i, 0))
    row_off = lambda w: pl.BlockSpec((tm, w), lambda i: (i + off, 0))
    full = lambda a, b: pl.BlockSpec((a, b), lambda i: (0, 0))
    any_spec = pl.BlockSpec(memory_space=pl.ANY)
    return pl.pallas_call(
        functools.partial(_final_kernel, alpha=alpha, tm=tm, off=off),
        out_shape=jax.ShapeDtypeStruct((r, D_MODEL), F32),
        grid=(r // tm,),
        in_specs=[any_spec, any_spec, row_off(PACKED), row(D_MODEL), row_off(TOP_K), row_off(TOP_K),
                  _mod_spec(per_row, tm, rows_per_seq),
                  full(D_MODEL, 2 * D_EXPERT), full(D_EXPERT, D_MODEL), full(1, D_MODEL), full(1, D_MODEL)],
        out_specs=row(D_MODEL),
        scratch_shapes=[pltpu.SMEM((2, 3 * N_EXPERTS), jnp.int32), pltpu.SMEM((2,), jnp.int32),
                        pltpu.VMEM((2, _sorted_rows(tm), PACKED), jnp.uint32),
                        pltpu.SemaphoreType.DMA((2,)), pltpu.SemaphoreType.DMA((2,))],
        compiler_params=_cparams(("arbitrary",)),
        name="final",
    )(tab, yb, h_all, x1, pos, wt, gate2, ws13, ws2, ln_g.reshape(1, -1), ln_b.reshape(1, -1))


def _pick_tile(n, pref):
    t = min(pref, n)
    while n % t:
        t //= 2
    return t


def kernel(x_prompt, x_sample, cache_k_diff, cache_v_diff, cache_k_band, cache_v_band, c_prompt, c_sample,
           w_ada, b_ada, w_in, b_in, lambda_q1, lambda_k1, lambda_q2, lambda_k2, subln_g, rel_bias, p_a, p_b,
           w_out, ln1_g, ln1_b, w_router, e_bias, w1, w3, w2, ws1, ws3, ws2, ln2_g, ln2_b):
    depth = w_in.shape[0]
    alpha = (2 * depth) ** 0.25
    bp, sp, d = x_prompt.shape
    bs, ss, _ = x_sample.shape
    past = cache_k_diff.shape[2]
    lb = cache_k_band.shape[2]
    tp, ts_ = bp * sp, bs * ss
    t_all = tp + ts_
    slopes = (2.0 ** (-8.0 * np.arange(1, H_A + 1) / H_A)).astype(np.float32)
    cslopes = jnp.asarray(np.float32(LOG2E) * slopes)

    yp = x_prompt.reshape(tp, d)
    ys = x_sample.reshape(ts_, d)
    outs = [[] for _ in range(8)]
    tail_p = min(BAND_CHUNKS * CHUNK, sp)
    tm_p = _pick_tile(math.gcd(sp, tail_p), 256)
    tq = _pick_tile(sp, 256)
    ta = _pick_tile(sp, 512)
    ka_aug = _key_bias_columns(slopes, sp)

    for l in range(depth):
        lam_init = 0.8 - 0.6 * math.exp(-0.3 * l)
        lam4 = jnp.stack([lambda_q1[l], lambda_k1[l], lambda_q2[l], lambda_k2[l]]).astype(F32)
        w_in16 = w_in[l].astype(BF16)
        b_in2 = b_in[l].reshape(1, N_IN)
        pa16, pb16, wo16 = p_a[l].astype(BF16), p_b[l].astype(BF16), w_out[l].astype(BF16)

        mod = _ada(jnp.concatenate([c_prompt, c_sample], axis=0), w_ada[l], b_ada[l])
        mod_p = [m.reshape(bp, 1, d) for m in jnp.split(mod[:bp], 6, axis=-1)]
        mod_s = [jnp.repeat(m, ss, axis=0) for m in jnp.split(mod[bp:], 6, axis=-1)]

        (qa, ka32, va32, ka16, va16, qb, kb16, vb16, sga, sgb, kbt, vbt) = _inproj(
            yp, mod_p[0], mod_p[1], w_in16, b_in2, tm=tm_p, rows_per_seq=sp, tail_rows=tail_p, per_row=False)
        r3 = lambda a: a.reshape(bp, sp, a.shape[-1])
        oa = _diffattn(r3(qa), r3(ka16), ka_aug, r3(va16), cslopes, lam4, subln_g[l], lam_init, t=ta)
        ob = _bandattn(r3(qb), r3(kb16), r3(vb16), _band_bias_tiles(rel_bias[l], sp, tq), tq=tq)
        x1p, hp = _outproj(oa.reshape(tp, D_A), ob.reshape(tp, D_B), sga, sgb, yp, mod_p[2], mod_p[3], mod_p[4],
                           pa16, pb16, wo16, ln1_g[l], ln1_b[l], tm=_pick_tile(sp, 512), rows_per_seq=sp,
                           per_row=False, alpha=alpha)
        outs[0].append(ka32.reshape(bp, sp, H_A, 2, HEAD_DIM))
        outs[1].append(va32.reshape(bp, sp, H_A, 2 * HEAD_DIM))
        outs[2].append(kbt.reshape(bp, tail_p, H_B, HEAD_DIM))
        outs[3].append(vbt.reshape(bp, tail_p, H_B, HEAD_DIM))

        (qa_s, ka32_s, va32_s, ka16_s, va16_s, qb_s, kb16_s, vb16_s, sga_s, sgb_s, kbt_s, vbt_s) = _inproj(
            ys, mod_s[0], mod_s[1], w_in16, b_in2, tm=ts_, rows_per_seq=ts_, tail_rows=ts_, per_row=True)
        s3 = lambda a: a.reshape(bs, ss, a.shape[-1])
        oa_s, ob_s = _sampattn(
            s3(qa_s), cache_k_diff[l].reshape(bs, past, D_A), cache_v_diff[l].reshape(bs, past, D_A),
            s3(ka16_s), s3(va16_s), s3(qb_s), cache_k_band[l].reshape(bs, lb, D_B),
            cache_v_band[l].reshape(bs, lb, D_B), s3(kb16_s), s3(vb16_s), cslopes, lam4, subln_g[l],
            _sample_band_bias(rel_bias[l], ss, past, lb), lam_init)
        x1s, hs = _outproj(oa_s.reshape(ts_, D_A), ob_s.reshape(ts_, D_B), sga_s, sgb_s, ys, mod_s[2], mod_s[3],
                           mod_s[4], pa16, pb16, wo16, ln1_g[l], ln1_b[l], tm=ts_, rows_per_seq=ts_, per_row=True,
                           alpha=alpha)
        outs[4].append(ka32_s.reshape(bs, ss, H_A, 2, HEAD_DIM))
        outs[5].append(va32_s.reshape(bs, ss, H_A, 2 * HEAD_DIM))
        outs[6].append(kbt_s.reshape(bs, ss, H_B, HEAD_DIM))
        outs[7].append(vbt_s.reshape(bs, ss, H_B, HEAD_DIM))

        h_all = jnp.concatenate([hp, hs], axis=0)
        tm_r = _pick_tile(math.gcd(sp, ts_), 256)
        n_tiles = t_all // tm_r
        wt_t, pos_t, tab_raw, cnt = _router(h_all, w_router[l], e_bias[l], tr=tm_r)
        counts = cnt[:, 0]
        padded = (counts + EXPERT_BLOCK - 1) // EXPERT_BLOCK * EXPERT_BLOCK
        pad_end = jnp.cumsum(padded)
        pad_start = pad_end - padded
        max_slots = t_all * TOP_K + n_tiles * N_EXPERTS * (ROW_ALIGN - 1)
        nb = -(-max_slots // EXPERT_BLOCK) + N_EXPERTS
        n_slots = nb * EXPERT_BLOCK
        block_starts = jnp.arange(nb, dtype=jnp.int32) * EXPERT_BLOCK
        block_e = jnp.minimum(jnp.sum(pad_end[None, :] <= block_starts[:, None], axis=1),
                              N_EXPERTS - 1).astype(jnp.int32)
        n_used = (pad_end[-1:] // EXPERT_BLOCK).astype(jnp.int32)
        tab = jnp.stack([tab_raw[:, :, 0] + pad_start[None, :], tab_raw[:, :, 1], tab_raw[:, :, 2]], axis=1)
        tab = tab.reshape(n_tiles, 3 * N_EXPERTS).astype(jnp.int32)
        xs = _dispatch(h_all, pos_t, tab, (pad_start + counts).astype(jnp.int32), pad_end.astype(jnp.int32),
                       n_slots, tm=tm_r)
        w13 = jnp.concatenate([w1[l], w3[l]], axis=-1).astype(BF16)
        yb = _experts(xs, w13, w2[l].astype(BF16), block_e, n_used)
        wt, pos = wt_t.T, pos_t.T
        ws13 = jnp.concatenate([ws1[l], ws3[l]], axis=-1).astype(BF16)
        ws2b = ws2[l].astype(BF16)
        yp = _final(tab, yb, h_all, x1p, pos, wt, mod_p[5], ws13, ws2b, ln2_g[l], ln2_b[l], tm=tm_r, row0=0,
                    rows_per_seq=sp, per_row=False, alpha=alpha)
        ys = _final(tab, yb, h_all, x1s, pos, wt, mod_s[5], ws13, ws2b, ln2_g[l], ln2_b[l], tm=tm_r, row0=tp,
                    rows_per_seq=ts_, per_row=True, alpha=alpha)

    return (yp.reshape(bp, sp, d), ys.reshape(bs, ss, d)) + tuple(jnp.stack(o) for o in outs)
```

```python
import functools
import math

import jax
import jax.numpy as jnp
import numpy as np
from jax import lax
from jax.experimental import pallas as pl
from jax.experimental.pallas import tpu as pltpu

F32 = jnp.float32
BF16 = jnp.bfloat16

D_MODEL = 1024
CHUNK = 64
HEAD_DIM = 64
H_A = 8
H_B = 8
D_A = H_A * 2 * HEAD_DIM
D_B = H_B * HEAD_DIM
BAND_CHUNKS = 8
REL_CLIP = 128
N_IN = 3 * D_A + 3 * D_B + 2 * D_MODEL
N_EXPERTS = 64
TOP_K = 8
N_GROUP = 8
TOPK_GROUP = 4
GROUP_SIZE = N_EXPERTS // N_GROUP
D_EXPERT = 256
ROUTED_SCALE = 2.5
EXPERT_BLOCK = 512
ROW_ALIGN = 8
SORT_CHUNK = 256
LN_EPS = 1e-5
NEG = -1e30
LOG2E = math.log2(math.e)
Q_SCALE = HEAD_DIM ** -0.5 * LOG2E

VMEM_LIMIT = 56 * 1024 * 1024


def _cparams(sem):
    return pltpu.CompilerParams(dimension_semantics=sem, vmem_limit_bytes=VMEM_LIMIT)


def _ln(x):
    mu = jnp.mean(x, axis=-1, keepdims=True)
    xc = x - mu
    var = jnp.mean(xc * xc, axis=-1, keepdims=True)
    return xc * lax.rsqrt(var + LN_EPS)


def _rows(ref):
    v = ref[...]
    return v.reshape(v.shape[-2], v.shape[-1])


def _split_bf16(x):
    hi = x.astype(BF16)
    lo = (x - hi.astype(F32)).astype(BF16)
    return hi, lo


def _dot(a, b):
    return jnp.dot(a, b, preferred_element_type=F32)


def _dot_nt(a, b):
    return lax.dot_general(a, b, (((1,), (1,)), ((), ())), preferred_element_type=F32)


PACKED = D_MODEL // 2


def _pack_pair(a, b):
    ua = lax.bitcast_convert_type(a.astype(BF16).astype(F32), jnp.uint32)
    ub = lax.bitcast_convert_type(b.astype(BF16).astype(F32), jnp.uint32)
    return ua | (ub >> 16)


def _unpack_pair(u):
    a = lax.bitcast_convert_type(u & jnp.uint32(0xFFFF0000), F32).astype(BF16)
    b = lax.bitcast_convert_type(u << 16, F32).astype(BF16)
    return a, b


def _dot_packed(u, w):
    a, b = _unpack_pair(u)
    return _dot(a, w[:PACKED]) + _dot(b, w[PACKED:])


def _ada_kernel(c_ref, w_ref, b_ref, o_ref):
    c = c_ref[...]
    a = c * jax.nn.sigmoid(c)
    a_hi, a_lo = _split_bf16(a)
    w_hi, w_lo = _split_bf16(w_ref[...])
    o_ref[...] = _dot(a_hi, w_hi) + _dot(a_lo, w_hi) + _dot(a_hi, w_lo) + b_ref[...]


def _ada(c, w_ada, b_ada):
    n, d = c.shape
    nout = w_ada.shape[1]
    tn = 1024
    return pl.pallas_call(
        _ada_kernel,
        out_shape=jax.ShapeDtypeStruct((n, nout), F32),
        grid=(nout // tn,),
        in_specs=[pl.BlockSpec((n, d), lambda j: (0, 0)),
                  pl.BlockSpec((d, tn), lambda j: (0, j)),
                  pl.BlockSpec((1, tn), lambda j: (0, j))],
        out_specs=pl.BlockSpec((n, tn), lambda j: (0, j)),
        compiler_params=_cparams(("parallel",)),
        name="ada",
    )(c, w_ada, b_ada.reshape(1, nout))


_SEG_QA = (0, D_A)
_SEG_KA = (D_A, 2 * D_A)
_SEG_VA = (2 * D_A, 3 * D_A)
_SEG_QB = (3 * D_A, 3 * D_A + D_B)
_SEG_KB = (3 * D_A + D_B, 3 * D_A + 2 * D_B)
_SEG_VB = (3 * D_A + 2 * D_B, 3 * D_A + 3 * D_B)
_SEG_GA = (3 * D_A + 3 * D_B, 3 * D_A + 3 * D_B + D_MODEL)
_SEG_GB = (3 * D_A + 3 * D_B + D_MODEL, N_IN)


def _inproj_kernel(x_ref, sh_ref, sc_ref, w_ref, b_ref,
                   qa_ref, ka32_ref, va32_ref, ka16_ref, va16_ref,
                   qb_ref, kb16_ref, vb16_ref, sga_ref, sgb_ref, kbt_ref, vbt_ref, *, tiles_per_seq, tail_tiles):
    u = (_ln(x_ref[...]) * (1.0 + _rows(sc_ref)) + _rows(sh_ref)).astype(BF16)

    def seg(lo_hi):
        lo, hi = lo_hi
        return _dot(u, w_ref[:, lo:hi]) + b_ref[:, lo:hi]

    qa_ref[...] = (seg(_SEG_QA) * Q_SCALE).astype(BF16)
    ka = seg(_SEG_KA)
    for hj in range(2 * H_A):
        ka32_ref[:, hj // 2, hj % 2, :] = ka[:, hj * HEAD_DIM:(hj + 1) * HEAD_DIM]
    ka16_ref[...] = ka.astype(BF16)
    va = seg(_SEG_VA)
    va32_ref[...] = va
    va16_ref[...] = va.astype(BF16)
    qb_ref[...] = (seg(_SEG_QB) * Q_SCALE).astype(BF16)
    kb = seg(_SEG_KB)
    kb16_ref[...] = kb.astype(BF16)
    vb = seg(_SEG_VB)
    vb16_ref[...] = vb.astype(BF16)
    @pl.when(pl.program_id(0) % tiles_per_seq >= tiles_per_seq - tail_tiles)
    def _():
        for h in range(H_B):
            kbt_ref[:, h, :] = kb[:, h * HEAD_DIM:(h + 1) * HEAD_DIM]
            vbt_ref[:, h, :] = vb[:, h * HEAD_DIM:(h + 1) * HEAD_DIM]
    sga_ref[...] = jax.nn.sigmoid(seg(_SEG_GA)).astype(BF16)
    sgb_ref[...] = jax.nn.sigmoid(seg(_SEG_GB)).astype(BF16)


def _mod_spec(per_row, tm, rows_per_seq):
    if per_row:
        return pl.BlockSpec((tm, D_MODEL), lambda i: (i, 0))
    tiles_per_seq = rows_per_seq // tm
    return pl.BlockSpec((1, 1, D_MODEL), lambda i: (i // tiles_per_seq, 0, 0))


def _inproj(x, shift, scale, w_in16, b_in, *, tm, rows_per_seq, tail_rows, per_row):
    r = x.shape[0]
    nseq = r // rows_per_seq
    n_i = rows_per_seq // tm
    n_t = tail_rows // tm

    def tail_map(i):
        return ((i // n_i) * n_t + jnp.maximum(i % n_i - (n_i - n_t), 0), 0, 0)

    row = lambda w: pl.BlockSpec((tm, w), lambda i: (i, 0))
    out_shape = (
        jax.ShapeDtypeStruct((r, D_A), BF16),
        jax.ShapeDtypeStruct((r, H_A, 2, HEAD_DIM), F32),
        jax.ShapeDtypeStruct((r, D_A), F32),
        jax.ShapeDtypeStruct((r, D_A), BF16),
        jax.ShapeDtypeStruct((r, D_A), BF16),
        jax.ShapeDtypeStruct((r, D_B), BF16),
        jax.ShapeDtypeStruct((r, D_B), BF16),
        jax.ShapeDtypeStruct((r, D_B), BF16),
        jax.ShapeDtypeStruct((r, D_MODEL), BF16),
        jax.ShapeDtypeStruct((r, D_MODEL), BF16),
        jax.ShapeDtypeStruct((nseq * tail_rows, H_B, HEAD_DIM), F32),
        jax.ShapeDtypeStruct((nseq * tail_rows, H_B, HEAD_DIM), F32),
    )
    out_specs = (row(D_A), pl.BlockSpec((tm, H_A, 2, HEAD_DIM), lambda i: (i, 0, 0, 0)), row(D_A), row(D_A), row(D_A),
                 row(D_B), row(D_B), row(D_B), row(D_MODEL), row(D_MODEL),
                 pl.BlockSpec((tm, H_B, HEAD_DIM), tail_map), pl.BlockSpec((tm, H_B, HEAD_DIM), tail_map))
    return pl.pallas_call(
        functools.partial(_inproj_kernel, tiles_per_seq=n_i, tail_tiles=n_t),
        out_shape=out_shape,
        grid=(r // tm,),
        in_specs=[row(D_MODEL), _mod_spec(per_row, tm, rows_per_seq), _mod_spec(per_row, tm, rows_per_seq),
                  pl.BlockSpec((D_MODEL, N_IN), lambda i: (0, 0)),
                  pl.BlockSpec((1, N_IN), lambda i: (0, 0))],
        out_specs=out_specs,
        compiler_params=_cparams(("arbitrary",)),
        name="inproj",
    )(x, shift, scale, w_in16, b_in)


N_AUG = 3


def _lam_value(lam4_ref, lam_init):
    v = lam4_ref[...]
    a = jnp.sum(v[0:1, :] * v[1:2, :], axis=-1, keepdims=True)
    b = jnp.sum(v[2:3, :] * v[3:4, :], axis=-1, keepdims=True)
    return jnp.exp(a) - jnp.exp(b) + lam_init


def _half_masks(shape):
    lane = lax.broadcasted_iota(jnp.int32, shape, 1)
    return lane < HEAD_DIM, lane >= HEAD_DIM


def _stack_halves(q):
    lo, hi = _half_masks(q.shape)
    zero = jnp.zeros_like(q)
    return jnp.concatenate([jnp.where(lo, q, zero), jnp.where(hi, q, zero)], axis=0)


def _softmax_step(s, v, m, l, acc):
    m_new = jnp.maximum(m, jnp.max(s, axis=-1, keepdims=True))
    alpha = jnp.exp2(m - m_new)
    p = jnp.exp2(s - m_new)
    l_new = alpha * l + jnp.sum(p, axis=-1, keepdims=True)
    acc_new = alpha * acc + _dot(p.astype(BF16), v)
    return m_new, l_new, acc_new


def _diff_finish(o0, o1, lam, g, lam_init):
    o = o0 - lam * o1
    o = o * lax.rsqrt(jnp.mean(o * o, axis=-1, keepdims=True) + LN_EPS)
    return o * g * (1.0 - lam_init)


def _key_bias_columns(slopes, s):
    x = (np.float32(LOG2E) * slopes.astype(np.float32))[:, None] * np.arange(s, dtype=np.float32)[None, :]
    cols = np.zeros(x.shape + (2 * HEAD_DIM,), np.float32)
    for c in range(N_AUG):
        cols[:, :, c] = x.astype(BF16).astype(np.float32)
        x = x - cols[:, :, c]
    return jnp.asarray(cols, BF16)


def _diffattn_kernel(cs_ref, lam4_ref, g_ref, q_ref, k_ref, ka_ref, v_ref, o_ref, *, t, lam_init):
    h = pl.program_id(1)
    qi = pl.program_id(2)
    cs = cs_ref[h]
    q = q_ref[0]
    lo, hi = _half_masks(q.shape)
    lane = lax.broadcasted_iota(jnp.int32, q.shape, 1)
    zero = jnp.zeros_like(q)
    ones = (lane < N_AUG).astype(BF16)
    q_ext = (jnp.concatenate([jnp.where(lo, q, zero), ones], axis=1),
             jnp.concatenate([jnp.where(hi, q, zero), ones], axis=1))

    def block(j, carry, extra):
        start = pl.multiple_of(j * t, t)
        k_ext = jnp.concatenate([k_ref[0, pl.ds(start, t), :], ka_ref[0, pl.ds(start, t), :]], axis=1)
        vj = v_ref[0, pl.ds(start, t), :]
        out = []
        for half in range(2):
            s = _dot_nt(q_ext[half], k_ext)
            if extra is not None:
                s = s + extra
            out.append(_softmax_step(s, vj, *carry[half]))
        return tuple(out)

    init1 = (jnp.full((t, 1), NEG, F32), jnp.zeros((t, 1), F32), jnp.zeros((t, 2 * HEAD_DIM), F32))
    carry = lax.fori_loop(0, qi, lambda j, c: block(j, c, None), (init1, init1))

    ii = lax.broadcasted_iota(jnp.int32, (t, t), 0)
    jj = lax.broadcasted_iota(jnp.int32, (t, t), 1)
    fix = jnp.where(jj > ii, (2.0 * cs) * (ii - jj).astype(F32), 0.0)
    fix = jnp.where((jj // CHUNK) <= (ii // CHUNK), fix, NEG)
    (m0, l0, a0), (m1, l1, a1) = block(qi, carry, fix)

    lam = _lam_value(lam4_ref, lam_init)
    o_ref[0] = _diff_finish(a0 / l0, a1 / l1, lam, g_ref[...], lam_init).astype(o_ref.dtype)


def _diffattn(qa, ka, ka_aug, va, cslopes, lam4, subln_g, lam_init, *, t):
    b, s, _ = qa.shape
    kern = functools.partial(_diffattn_kernel, t=t, lam_init=lam_init)
    hd2 = 2 * HEAD_DIM
    return pl.pallas_call(
        kern,
        out_shape=jax.ShapeDtypeStruct((b, s, D_A), BF16),
        grid=(b, H_A, s // t),
        in_specs=[pl.BlockSpec(memory_space=pltpu.SMEM),
                  pl.BlockSpec((4, HEAD_DIM), lambda bi, h, i: (0, 0)),
                  pl.BlockSpec((1, hd2), lambda bi, h, i: (0, 0)),
                  pl.BlockSpec((1, t, hd2), lambda bi, h, i: (bi, i, h)),
                  pl.BlockSpec((1, s, hd2), lambda bi, h, i: (bi, 0, h)),
                  pl.BlockSpec((1, s, hd2), lambda bi, h, i: (h, 0, 0)),
                  pl.BlockSpec((1, s, hd2), lambda bi, h, i: (bi, 0, h))],
        out_specs=pl.BlockSpec((1, t, hd2), lambda bi, h, i: (bi, i, h)),
        compiler_params=_cparams(("parallel", "parallel", "arbitrary")),
        name="diffattn",
    )(cslopes, lam4, subln_g.reshape(1, hd2), qa, ka, ka_aug, va)


def _toeplitz_bias(rel_table, rows, cols, delta):
    length = rows + cols - 1
    rel = np.clip(delta + rows - 1 - np.arange(length), -REL_CLIP, REL_CLIP) + REL_CLIP
    u = jnp.take(rel_table.astype(F32) * LOG2E, jnp.asarray(rel, jnp.int32), axis=1)
    u = jnp.pad(u, ((0, 0), (0, 1)))
    r = jnp.tile(u, (1, rows))[:, :rows * length].reshape(-1, rows, length)
    return r[:, :, rows - 1:rows - 1 + cols]


def _band_valid(t_pos, s_pos):
    return (s_pos // CHUNK <= t_pos // CHUNK) & (s_pos // CHUNK >= t_pos // CHUNK - BAND_CHUNKS)


def _band_window(s, tq):
    return min(BAND_CHUNKS * CHUNK + tq, s)


def _band_bias_tiles(rel_table, s, tq):
    win = _band_window(s, tq)
    n_var = min(BAND_CHUNKS * CHUNK // tq + 1, s // tq)
    tiles = []
    for var in range(n_var):
        q0 = var * tq
        k0 = max(q0 - BAND_CHUNKS * CHUNK, 0)
        valid = _band_valid(q0 + np.arange(tq)[:, None], k0 + np.arange(win)[None, :])
        tiles.append(jnp.where(jnp.asarray(valid)[None], _toeplitz_bias(rel_table, tq, win, q0 - k0), NEG))
    return jnp.stack(tiles)


def _pair_softmax_out(qm, kw, vw, bias2, t):
    s = (_dot_nt(qm, kw).reshape(2, t, -1) + bias2).reshape(2 * t, -1)
    m = jnp.max(s, axis=-1, keepdims=True)
    p = jnp.exp2(s - m)
    l = jnp.sum(p, axis=-1, keepdims=True)
    o = _dot(p.astype(BF16), vw) / l
    lo, _ = _half_masks((t, 2 * HEAD_DIM))
    return jnp.where(lo, o[:t], o[t:])


def _bandattn_kernel(bias_ref, q_ref, k_ref, v_ref, o_ref, *, tq, win, n_var):
    qi = pl.program_id(2)
    var = jnp.minimum(qi, n_var - 1)
    k0 = pl.multiple_of(jnp.maximum(qi * tq - BAND_CHUNKS * CHUNK, 0), tq)
    qm = _stack_halves(q_ref[0])
    kw = k_ref[0, pl.ds(k0, win), :]
    vw = v_ref[0, pl.ds(k0, win), :]
    o_ref[0] = _pair_softmax_out(qm, kw, vw, bias_ref[var], tq).astype(o_ref.dtype)


def _bandattn(qb, kb, vb, bias_tiles, *, tq):
    b, s, _ = qb.shape
    n_var, _, _, win = bias_tiles.shape
    kern = functools.partial(_bandattn_kernel, tq=tq, win=win, n_var=n_var)
    return pl.pallas_call(
        kern,
        out_shape=jax.ShapeDtypeStruct((b, s, D_B), BF16),
        grid=(H_B // 2, b, s // tq),
        in_specs=[pl.BlockSpec((n_var, 2, tq, win), lambda hp, bi, i: (0, hp, 0, 0)),
                  pl.BlockSpec((1, tq, 2 * HEAD_DIM), lambda hp, bi, i: (bi, i, hp)),
                  pl.BlockSpec((1, s, 2 * HEAD_DIM), lambda hp, bi, i: (bi, 0, hp)),
                  pl.BlockSpec((1, s, 2 * HEAD_DIM), lambda hp, bi, i: (bi, 0, hp))],
        out_specs=pl.BlockSpec((1, tq, 2 * HEAD_DIM), lambda hp, bi, i: (bi, i, hp)),
        compiler_params=_cparams(("parallel", "parallel", "arbitrary")),
        name="bandattn",
    )(bias_tiles, qb, kb, vb)


def _two_part_softmax(s_c, s_n, vc, vn):
    m = jnp.maximum(jnp.max(s_c, axis=-1, keepdims=True), jnp.max(s_n, axis=-1, keepdims=True))
    p_c = jnp.exp2(s_c - m)
    p_n = jnp.exp2(s_n - m)
    l = jnp.sum(p_c, axis=-1, keepdims=True) + jnp.sum(p_n, axis=-1, keepdims=True)
    return (_dot(p_c.astype(BF16), vc) + _dot(p_n.astype(BF16), vn)) / l


def _sampattn_kernel(cs_ref, lam4_ref, g_ref, bbias_ref, qa_ref, ckd_ref, cvd_ref, kan_ref, van_ref,
                     qb_ref, ckb_ref, cvb_ref, kbn_ref, vbn_ref, oa_ref, ob_ref, *, n, past, lam_init):
    lam = _lam_value(lam4_ref, lam_init)
    tq_pos = past + lax.broadcasted_iota(jnp.int32, (n, past), 0)
    ts_pos = lax.broadcasted_iota(jnp.int32, (n, past), 1)
    dist_c = jnp.abs(tq_pos - ts_pos).astype(F32)
    vis_c = (ts_pos // CHUNK) <= (tq_pos // CHUNK)
    ii = lax.broadcasted_iota(jnp.int32, (n, n), 0)
    jj = lax.broadcasted_iota(jnp.int32, (n, n), 1)
    dist_n = jnp.abs(ii - jj).astype(F32)
    vis_n = ((past + jj) // CHUNK) <= ((past + ii) // CHUNK)
    for h in range(H_A):
        sl = slice(h * 2 * HEAD_DIM, (h + 1) * 2 * HEAD_DIM)
        cs = cs_ref[h]
        qm = _stack_halves(qa_ref[0, :, sl])
        bias_c = jnp.where(vis_c, -cs * dist_c, NEG)
        bias_n = jnp.where(vis_n, -cs * dist_n, NEG)
        s_c = (_dot_nt(qm, ckd_ref[0, :, sl].astype(BF16)).reshape(2, n, past) + bias_c[None]).reshape(2 * n, past)
        s_n = (_dot_nt(qm, kan_ref[0, :, sl]).reshape(2, n, n) + bias_n[None]).reshape(2 * n, n)
        o = _two_part_softmax(s_c, s_n, cvd_ref[0, :, sl].astype(BF16), van_ref[0, :, sl])
        oa_ref[0, :, sl] = _diff_finish(o[:n], o[n:], lam, g_ref[...], lam_init).astype(oa_ref.dtype)
    lb = ckb_ref.shape[1]
    for hp in range(H_B // 2):
        sl = slice(hp * 2 * HEAD_DIM, (hp + 1) * 2 * HEAD_DIM)
        qm = _stack_halves(qb_ref[0, :, sl])
        bias = bbias_ref[2 * hp:2 * hp + 2]
        s_c = (_dot_nt(qm, ckb_ref[0, :, sl].astype(BF16)).reshape(2, n, lb) + bias[:, :, :lb]).reshape(2 * n, lb)
        s_n = (_dot_nt(qm, kbn_ref[0, :, sl]).reshape(2, n, n) + bias[:, :, lb:]).reshape(2 * n, n)
        o = _two_part_softmax(s_c, s_n, cvb_ref[0, :, sl].astype(BF16), vbn_ref[0, :, sl])
        lo, _ = _half_masks((n, 2 * HEAD_DIM))
        ob_ref[0, :, sl] = jnp.where(lo, o[:n], o[n:]).astype(ob_ref.dtype)


def _sample_band_bias(rel_table, n, past, lb):
    valid = _band_valid(past + np.arange(n)[:, None], (past - lb) + np.arange(lb + n)[None, :])
    return jnp.where(jnp.asarray(valid)[None], _toeplitz_bias(rel_table, n, lb + n, lb), NEG)


def _sampattn(qa, ckd, cvd, kan, van, qb, ckb, cvb, kbn, vbn, cslopes, lam4, subln_g, bbias, lam_init):
    b, n, _ = qa.shape
    past = ckd.shape[1]
    lb = ckb.shape[1]
    kern = functools.partial(_sampattn_kernel, n=n, past=past, lam_init=lam_init)
    full = lambda shape: pl.BlockSpec(shape, lambda bi: (0,) * len(shape))
    per_b = lambda r, w: pl.BlockSpec((1, r, w), lambda bi: (bi, 0, 0))
    return pl.pallas_call(
        kern,
        out_shape=(jax.ShapeDtypeStruct((b, n, D_A), BF16), jax.ShapeDtypeStruct((b, n, D_B), BF16)),
        grid=(b,),
        in_specs=[pl.BlockSpec(memory_space=pltpu.SMEM), full((4, HEAD_DIM)), full((1, 2 * HEAD_DIM)),
                  full((H_B, n, lb + n)),
                  per_b(n, D_A), per_b(past, D_A), per_b(past, D_A), per_b(n, D_A), per_b(n, D_A),
                  per_b(n, D_B), per_b(lb, D_B), per_b(lb, D_B), per_b(n, D_B), per_b(n, D_B)],
        out_specs=(per_b(n, D_A), per_b(n, D_B)),
        compiler_params=_cparams(("parallel",)),
        name="sampattn",
    )(cslopes, lam4, subln_g.reshape(1, 2 * HEAD_DIM), bbias, qa, ckd, cvd, kan, van, qb, ckb, cvb, kbn, vbn)


def _outproj_kernel(oa_ref, ob_ref, sga_ref, sgb_ref, x_ref, g1_ref, sh2_ref, sc2_ref, pa_ref, pb_ref, wo_ref,
                    lng_ref, lnb_ref, x1_ref, h_ref, *, alpha):
    mix = (sga_ref[...].astype(F32) * _dot(oa_ref[...], pa_ref[...])
           + sgb_ref[...].astype(F32) * _dot(ob_ref[...], pb_ref[...]))
    y = _dot(mix.astype(BF16), wo_ref[...])
    x1 = _ln(alpha * x_ref[...] + _rows(g1_ref) * y) * lng_ref[...] + lnb_ref[...]
    x1_ref[...] = x1
    u2 = _ln(x1) * (1.0 + _rows(sc2_ref)) + _rows(sh2_ref)
    h_ref[...] = _pack_pair(u2[:, :PACKED], u2[:, PACKED:])


def _outproj(oa, ob, sga, sgb, x, gate1, shift2, scale2, pa16, pb16, wo16, ln_g, ln_b, *, tm, rows_per_seq,
             per_row, alpha):
    r = x.shape[0]
    row = lambda w: pl.BlockSpec((tm, w), lambda i: (i, 0))
    full = lambda a, b: pl.BlockSpec((a, b), lambda i: (0, 0))
    mod = _mod_spec(per_row, tm, rows_per_seq)
    return pl.pallas_call(
        functools.partial(_outproj_kernel, alpha=alpha),
        out_shape=(jax.ShapeDtypeStruct((r, D_MODEL), F32), jax.ShapeDtypeStruct((r, PACKED), jnp.uint32)),
        grid=(r // tm,),
        in_specs=[row(D_A), row(D_B), row(D_MODEL), row(D_MODEL), row(D_MODEL), mod, mod, mod,
                  full(D_A, D_MODEL), full(D_B, D_MODEL), full(D_MODEL, D_MODEL),
                  full(1, D_MODEL), full(1, D_MODEL)],
        out_specs=(row(D_MODEL), row(PACKED)),
        compiler_params=_cparams(("parallel",)),
        name="outproj",
    )(oa, ob, sga, sgb, x, gate1, shift2, scale2, pa16, pb16, wo16, ln_g.reshape(1, -1), ln_b.reshape(1, -1))


def _first_index_of_max(x, iota, axis, size):
    m = jnp.max(x, axis=axis, keepdims=True)
    idx = jnp.min(jnp.where(x == m, iota, size), axis=axis, keepdims=True)
    return m, idx


def _router_kernel(h_ref, wr_hi_ref, wr_lo_ref, eb_ref, wt_ref, pos_ref, tab_ref, cnt_ref, run_ref, *, tr):
    step = pl.program_id(0)

    @pl.when(step == 0)
    def _():
        run_ref[...] = jnp.zeros_like(run_ref)

    ha, hb = _unpack_pair(h_ref[...])
    wr_hi, wr_lo = wr_hi_ref[...], wr_lo_ref[...]
    logits = (_dot_nt(wr_hi[:, :PACKED], ha) + _dot_nt(wr_hi[:, PACKED:], hb)
              + _dot_nt(wr_lo[:, :PACKED], ha) + _dot_nt(wr_lo[:, PACKED:], hb))
    scores = jax.nn.sigmoid(logits)
    biased = scores + eb_ref[...]

    x3 = biased.reshape(N_GROUP, GROUP_SIZE, tr)
    io3 = lax.broadcasted_iota(jnp.int32, x3.shape, 1)
    m1, i1 = _first_index_of_max(x3, io3, 1, GROUP_SIZE)
    m2 = jnp.max(jnp.where(io3 == i1, -jnp.inf, x3), axis=1, keepdims=True)
    grp = (m1 + m2).reshape(N_GROUP, tr)

    iog = lax.broadcasted_iota(jnp.int32, grp.shape, 0)
    gsel = jnp.zeros(grp.shape, jnp.bool_)
    for _ in range(TOPK_GROUP):
        _, gi = _first_index_of_max(grp, iog, 0, N_GROUP)
        hit = iog == gi
        gsel = gsel | hit
        grp = jnp.where(hit, -jnp.inf, grp)
    emask = jnp.broadcast_to(gsel.reshape(N_GROUP, 1, tr), (N_GROUP, GROUP_SIZE, tr)).reshape(N_EXPERTS, tr)
    cand = jnp.where(emask, biased, -jnp.inf)

    ioe = lax.broadcasted_iota(jnp.int32, cand.shape, 0)
    hits = []
    sel = jnp.zeros(cand.shape, jnp.bool_)
    for _ in range(TOP_K):
        _, ei = _first_index_of_max(cand, ioe, 0, N_EXPERTS)
        hit = ioe == ei
        hits.append((ei, hit))
        sel = sel | hit
        cand = jnp.where(hit, -jnp.inf, cand)
    self32 = sel.astype(F32)
    ra = lax.broadcasted_iota(jnp.int32, (tr, tr), 0)
    rb = lax.broadcasted_iota(jnp.int32, (tr, tr), 1)
    upper = (ra < rb).astype(BF16)
    prefix = _dot(self32.astype(BF16), upper)

    count = jnp.sum(self32, axis=1, keepdims=True)
    run_len = jnp.floor((count + (ROW_ALIGN - 1)) * (1.0 / ROW_ALIGN)) * ROW_ALIGN
    ea = lax.broadcasted_iota(jnp.int32, (N_EXPERTS, N_EXPERTS), 0)
    eb = lax.broadcasted_iota(jnp.int32, (N_EXPERTS, N_EXPERTS), 1)
    before = (eb < ea).astype(BF16)
    local_start = _dot(before, jnp.broadcast_to(run_len, (N_EXPERTS, 128)).astype(BF16))[:, :1]
    local_pos = local_start + prefix

    ws = [jnp.sum(jnp.where(hit, scores, 0.0), axis=0, keepdims=True) for _, hit in hits]
    wsum = ws[0]
    for w in ws[1:]:
        wsum = wsum + w
    for k, (_, hit) in enumerate(hits):
        wt_ref[k:k + 1, :] = ws[k] / wsum * ROUTED_SCALE
        pos_ref[k:k + 1, :] = jnp.sum(jnp.where(hit, local_pos, 0.0), axis=0, keepdims=True).astype(jnp.int32)
    run_old = run_ref[...]
    lane = lax.broadcasted_iota(jnp.int32, (N_EXPERTS, 128), 1)
    tab = jnp.where(lane == 0, run_old, jnp.where(lane == 1, local_start, jnp.where(lane == 2, run_len, 0.0)))
    tab_ref[0] = tab.astype(jnp.int32)
    run_new = run_old + run_len
    run_ref[...] = run_new
    cnt_ref[...] = jnp.broadcast_to(run_new, cnt_ref.shape).astype(jnp.int32)


def _router(h_all, w_router, e_bias, *, tr):
    t = h_all.shape[0]
    wr_t = w_router.T
    hi32 = lax.bitcast_convert_type(lax.bitcast_convert_type(wr_t, jnp.uint32) & jnp.uint32(0xFFFF0000), F32)
    wr_hi = hi32.astype(BF16)
    wr_lo = (wr_t - hi32).astype(BF16)
    full = lambda a, b: pl.BlockSpec((a, b), lambda i: (0, 0))
    col = lambda: pl.BlockSpec((TOP_K, tr), lambda i: (0, i))
    return pl.pallas_call(
        functools.partial(_router_kernel, tr=tr),
        out_shape=(jax.ShapeDtypeStruct((TOP_K, t), F32),
                   jax.ShapeDtypeStruct((TOP_K, t), jnp.int32),
                   jax.ShapeDtypeStruct((t // tr, N_EXPERTS, 128), jnp.int32),
                   jax.ShapeDtypeStruct((N_EXPERTS, 128), jnp.int32)),
        grid=(t // tr,),
        in_specs=[pl.BlockSpec((tr, PACKED), lambda i: (i, 0)), full(N_EXPERTS, D_MODEL),
                  full(N_EXPERTS, D_MODEL), full(N_EXPERTS, 1)],
        out_specs=(col(), col(), pl.BlockSpec((1, N_EXPERTS, 128), lambda i: (i, 0, 0)), full(N_EXPERTS, 128)),
        scratch_shapes=[pltpu.VMEM((N_EXPERTS, 1), F32)],
        compiler_params=_cparams(("arbitrary",)),
        name="router",
    )(h_all, wr_hi, wr_lo, e_bias.reshape(N_EXPERTS, 1).astype(F32))


TAB_GLOBAL, TAB_LOCAL, TAB_LEN = 0, 1, 2


def _sorted_rows(tm):
    need = tm * TOP_K + N_EXPERTS * (ROW_ALIGN - 1)
    return -(-need // SORT_CHUNK) * SORT_CHUNK


def _run(tab_smem, s, e):
    return tuple(pl.multiple_of(tab_smem[s, r * N_EXPERTS + e], ROW_ALIGN) for r in (TAB_GLOBAL, TAB_LOCAL, TAB_LEN))


def _tile_rows(tab_smem, s):
    _, local, length = _run(tab_smem, s, N_EXPERTS - 1)
    return pl.multiple_of(local + length, ROW_ALIGN)


def _rows_copy(src_ref, src_row, dst_ref, dst_row, rows, sem):
    return pltpu.make_async_copy(src_ref.at[pl.ds(src_row, rows)], dst_ref.at[pl.ds(dst_row, rows)], sem)


def _dispatch_kernel(fill_lo_ref, fill_hi_ref, tab_hbm, pos_ref, h_ref, xs_hbm, tab_smem, rows_smem, sorted_ref,
                     zero_ref, tsem, csem, zsem, *, tm):
    i = pl.program_id(0)
    n = pl.num_programs(0)
    slot = i % 2

    def tab_copy(tile, s):
        return pltpu.make_async_copy(tab_hbm.at[tile], tab_smem.at[s], tsem.at[s])

    def wait_runs(s):
        rows = rows_smem[s]

        @pl.when(rows > 0)
        def _():
            _rows_copy(sorted_ref.at[s], 0, xs_hbm, 0, pl.multiple_of(rows, ROW_ALIGN), csem.at[s]).wait()

    @pl.when(i == 0)
    def _():
        tab_copy(0, 0).start()
        zero_ref[...] = jnp.zeros_like(zero_ref)
        blk = zero_ref.shape[0]
        for e in range(N_EXPERTS):
            lo = pl.multiple_of(fill_lo_ref[e], ROW_ALIGN)
            rows = pl.multiple_of(fill_hi_ref[e] - lo, ROW_ALIGN)

            @pl.when(rows > 0)
            def _():
                cp = _rows_copy(zero_ref, 0, xs_hbm, lo, rows, zsem)
                cp.start()
                cp.wait()

        def tail(b, c):
            cp = _rows_copy(zero_ref, 0, xs_hbm, pl.multiple_of(b * blk, blk), blk, zsem)
            cp.start()
            cp.wait()
            return c

        lax.fori_loop(fill_hi_ref[N_EXPERTS - 1] // blk, xs_hbm.shape[0] // blk, tail, 0)

    tab_copy(i, slot).wait()

    @pl.when(i + 1 < n)
    def _():
        tab_copy(i + 1, 1 - slot).start()

    ha, hb = _unpack_pair(h_ref[...])
    pos = pos_ref[...]
    rows = _tile_rows(tab_smem, slot)

    def sort_chunk(c, carry):
        r0 = pl.multiple_of(c * SORT_CHUNK, SORT_CHUNK)
        row = r0 + lax.broadcasted_iota(jnp.int32, (SORT_CHUNK, tm), 0)
        pick = jnp.zeros((SORT_CHUNK, tm), F32)
        for k in range(TOP_K):
            pick = jnp.where(row == pos[k:k + 1, :], 1.0, pick)
        pick = pick.astype(BF16)
        sorted_ref[slot, pl.ds(r0, SORT_CHUNK), :] = _pack_pair(_dot(pick, ha), _dot(pick, hb))
        return carry

    lax.fori_loop(0, (rows + SORT_CHUNK - 1) // SORT_CHUNK, sort_chunk, 0)

    for e in range(N_EXPERTS):
        dst, src, length = _run(tab_smem, slot, e)

        @pl.when(length > 0)
        def _():
            _rows_copy(sorted_ref.at[slot], src, xs_hbm, dst, length, csem.at[slot]).start()

    rows_smem[slot] = rows

    @pl.when(i > 0)
    def _():
        wait_runs(1 - slot)

    @pl.when(i == n - 1)
    def _():
        wait_runs(slot)


def _dispatch(h_all, pos_t, tab, fill_lo, fill_hi, n_slots, *, tm):
    t = h_all.shape[0]
    return pl.pallas_call(
        functools.partial(_dispatch_kernel, tm=tm),
        out_shape=jax.ShapeDtypeStruct((n_slots, PACKED), jnp.uint32),
        grid_spec=pltpu.PrefetchScalarGridSpec(
            num_scalar_prefetch=2,
            grid=(t // tm,),
            in_specs=[pl.BlockSpec(memory_space=pl.ANY),
                      pl.BlockSpec((TOP_K, tm), lambda i, lo, hi: (0, i)),
                      pl.BlockSpec((tm, PACKED), lambda i, lo, hi: (i, 0))],
            out_specs=pl.BlockSpec(memory_space=pl.ANY),
            scratch_shapes=[pltpu.SMEM((2, 3 * N_EXPERTS), jnp.int32), pltpu.SMEM((2,), jnp.int32),
                            pltpu.VMEM((2, _sorted_rows(tm), PACKED), jnp.uint32),
                            pltpu.VMEM((EXPERT_BLOCK, PACKED), jnp.uint32),
                            pltpu.SemaphoreType.DMA((2,)), pltpu.SemaphoreType.DMA((2,)),
                            pltpu.SemaphoreType.DMA(())],
        ),
        compiler_params=_cparams(("arbitrary",)),
        name="dispatch",
    )(fill_lo, fill_hi, tab, pos_t, h_all)


def _experts_kernel(be_ref, nu_ref, x_ref, w13_ref, w2_ref, y_ref):
    @pl.when(pl.program_id(0) < nu_ref[0])
    def _():
        a = _dot_packed(x_ref[...], w13_ref[0])
        hid = (a[:, :D_EXPERT] * jax.nn.sigmoid(a[:, :D_EXPERT])) * a[:, D_EXPERT:]
        y = _dot(hid.astype(BF16), w2_ref[0])
        y_ref[...] = _pack_pair(y[:, :PACKED], y[:, PACKED:])

    @pl.when(pl.program_id(0) >= nu_ref[0])
    def _():
        y_ref[...] = jnp.zeros_like(y_ref)


def _experts(xs, w13, w2, block_e, n_used):
    n_slots = xs.shape[0]
    nb = n_slots // EXPERT_BLOCK
    last = lambda i, nu: jnp.minimum(i, nu[0] - 1)
    return pl.pallas_call(
        _experts_kernel,
        out_shape=jax.ShapeDtypeStruct((n_slots, PACKED), jnp.uint32),
        grid_spec=pltpu.PrefetchScalarGridSpec(
            num_scalar_prefetch=2,
            grid=(nb,),
            in_specs=[pl.BlockSpec((EXPERT_BLOCK, PACKED), lambda i, be, nu: (last(i, nu), 0)),
                      pl.BlockSpec((1, D_MODEL, 2 * D_EXPERT), lambda i, be, nu: (be[last(i, nu)], 0, 0)),
                      pl.BlockSpec((1, D_EXPERT, D_MODEL), lambda i, be, nu: (be[last(i, nu)], 0, 0))],
            out_specs=pl.BlockSpec((EXPERT_BLOCK, PACKED), lambda i, be, nu: (i, 0)),
        ),
        compiler_params=_cparams(("arbitrary",)),
        name="experts",
    )(block_e, n_used, xs, w13, w2)


def _final_kernel(tab_hbm, yb_hbm, h_ref, x1_ref, pos_ref, wt_ref, g2_ref, ws13_ref, ws2_ref, lng_ref, lnb_ref,
                  o_ref, tab_smem, rows_smem, ybuf, tsem, gsem, *, alpha, tm, off):
    i = pl.program_id(0)
    n = pl.num_programs(0)
    slot = i % 2

    def tab_copy(tile, s):
        return pltpu.make_async_copy(tab_hbm.at[tile + off], tab_smem.at[s], tsem.at[s])

    def fetch_runs(s):
        for e in range(N_EXPERTS):
            src, dst, length = _run(tab_smem, s, e)

            @pl.when(length > 0)
            def _():
                _rows_copy(yb_hbm, src, ybuf.at[s], dst, length, gsem.at[s]).start()

        rows_smem[s] = _tile_rows(tab_smem, s)

    @pl.when(i == 0)
    def _():
        ybuf[...] = jnp.zeros_like(ybuf)
        tab_copy(0, 0).start()
        tab_copy(0, 0).wait()
        fetch_runs(0)

        @pl.when(n > 1)
        def _():
            tab_copy(1, 1).start()

    @pl.when(i + 1 < n)
    def _():
        tab_copy(i + 1, 1 - slot).wait()
        fetch_runs(1 - slot)

        @pl.when(i + 2 < n)
        def _():
            tab_copy(i + 2, slot).start()

    a = _dot_packed(h_ref[...], ws13_ref[...])
    hid = (a[:, :D_EXPERT] * jax.nn.sigmoid(a[:, :D_EXPERT])) * a[:, D_EXPERT:]
    y = _dot(hid.astype(BF16), ws2_ref[...])

    rows = rows_smem[slot]

    @pl.when(rows > 0)
    def _():
        _rows_copy(yb_hbm, 0, ybuf.at[slot], 0, pl.multiple_of(rows, ROW_ALIGN), gsem.at[slot]).wait()

    pos = pos_ref[...]
    wt = wt_ref[...]
    lanes = 128
    pos_b = [jnp.broadcast_to(pos[:, k:k + 1], (tm, lanes)) for k in range(TOP_K)]
    wt_b = [jnp.broadcast_to(wt[:, k:k + 1], (tm, lanes)) for k in range(TOP_K)]
    lane_id = lax.broadcasted_iota(jnp.int32, (tm, lanes), 1)

    def combine_chunk(c, acc):
        r0 = pl.multiple_of(c * SORT_CHUNK, SORT_CHUNK)
        parts = []
        for part in range(SORT_CHUNK // lanes):
            col = lane_id + (r0 + part * lanes)
            w = jnp.zeros((tm, lanes), F32)
            for k in range(TOP_K):
                w = jnp.where(col == pos_b[k], wt_b[k], w)
            parts.append(w)
        w_hi, w_lo = _split_bf16(jnp.concatenate(parts, axis=1))
        w2 = jnp.concatenate([w_hi, w_lo], axis=0)
        ea, eb = _unpack_pair(ybuf[slot, pl.ds(r0, SORT_CHUNK), :])
        ra, rb = _dot(w2, ea), _dot(w2, eb)
        return acc[0] + (ra[:tm] + ra[tm:]), acc[1] + (rb[:tm] + rb[tm:])

    zero = jnp.zeros((tm, PACKED), F32)
    ya, yb = lax.fori_loop(0, (rows + SORT_CHUNK - 1) // SORT_CHUNK, combine_chunk, (zero, zero))
    y = y + jnp.concatenate([ya, yb], axis=1)
    o_ref[...] = _ln(alpha * x1_ref[...] + _rows(g2_ref) * y) * lng_ref[...] + lnb_ref[...]


def _final(tab, yb, h_all, x1, pos, wt, gate2, ws13, ws2, ln_g, ln_b, *, tm, row0, rows_per_seq, per_row, alpha):
    r = x1.shape[0]
    off = row0 // tm
    row = lambda w: pl.BlockSpec((tm, w), lambda i: (i, 0))
    row_off = lambda w: pl.BlockSpec((tm, w), lambda i: (i + off, 0))
    full = lambda a, b: pl.BlockSpec((a, b), lambda i: (0, 0))
    any_spec = pl.BlockSpec(memory_space=pl.ANY)
    return pl.pallas_call(
        functools.partial(_final_kernel, alpha=alpha, tm=tm, off=off),
        out_shape=jax.ShapeDtypeStruct((r, D_MODEL), F32),
        grid=(r // tm,),
        in_specs=[any_spec, any_spec, row_off(PACKED), row(D_MODEL), row_off(TOP_K), row_off(TOP_K),
                  _mod_spec(per_row, tm, rows_per_seq),
                  full(D_MODEL, 2 * D_EXPERT), full(D_EXPERT, D_MODEL), full(1, D_MODEL), full(1, D_MODEL)],
        out_specs=row(D_MODEL),
        scratch_shapes=[pltpu.SMEM((2, 3 * N_EXPERTS), jnp.int32), pltpu.SMEM((2,), jnp.int32),
                        pltpu.VMEM((2, _sorted_rows(tm), PACKED), jnp.uint32),
                        pltpu.SemaphoreType.DMA((2,)), pltpu.SemaphoreType.DMA((2,))],
        compiler_params=_cparams(("arbitrary",)),
        name="final",
    )(tab, yb, h_all, x1, pos, wt, gate2, ws13, ws2, ln_g.reshape(1, -1), ln_b.reshape(1, -1))


def _pick_tile(n, pref):
    t = min(pref, n)
    while n % t:
        t //= 2
    return t


def kernel(x_prompt, x_sample, cache_k_diff, cache_v_diff, cache_k_band, cache_v_band, c_prompt, c_sample,
           w_ada, b_ada, w_in, b_in, lambda_q1, lambda_k1, lambda_q2, lambda_k2, subln_g, rel_bias, p_a, p_b,
           w_out, ln1_g, ln1_b, w_router, e_bias, w1, w3, w2, ws1, ws3, ws2, ln2_g, ln2_b):
    depth = w_in.shape[0]
    alpha = (2 * depth) ** 0.25
    bp, sp, d = x_prompt.shape
    bs, ss, _ = x_sample.shape
    past = cache_k_diff.shape[2]
    lb = cache_k_band.shape[2]
    tp, ts_ = bp * sp, bs * ss
    t_all = tp + ts_
    slopes = (2.0 ** (-8.0 * np.arange(1, H_A + 1) / H_A)).astype(np.float32)
    cslopes = jnp.asarray(np.float32(LOG2E) * slopes)

    yp = x_prompt.reshape(tp, d)
    ys = x_sample.reshape(ts_, d)
    outs = [[] for _ in range(8)]
    tail_p = min(BAND_CHUNKS * CHUNK, sp)
    tm_p = _pick_tile(math.gcd(sp, tail_p), 256)
    tq = _pick_tile(sp, 256)
    ta = _pick_tile(sp, 512)
    ka_aug = _key_bias_columns(slopes, sp)

    for l in range(depth):
        lam_init = 0.8 - 0.6 * math.exp(-0.3 * l)
        lam4 = jnp.stack([lambda_q1[l], lambda_k1[l], lambda_q2[l], lambda_k2[l]]).astype(F32)
        w_in16 = w_in[l].astype(BF16)
        b_in2 = b_in[l].reshape(1, N_IN)
        pa16, pb16, wo16 = p_a[l].astype(BF16), p_b[l].astype(BF16), w_out[l].astype(BF16)

        mod = _ada(jnp.concatenate([c_prompt, c_sample], axis=0), w_ada[l], b_ada[l])
        mod_p = [m.reshape(bp, 1, d) for m in jnp.split(mod[:bp], 6, axis=-1)]
        mod_s = [jnp.repeat(m, ss, axis=0) for m in jnp.split(mod[bp:], 6, axis=-1)]

        (qa, ka32, va32, ka16, va16, qb, kb16, vb16, sga, sgb, kbt, vbt) = _inproj(
            yp, mod_p[0], mod_p[1], w_in16, b_in2, tm=tm_p, rows_per_seq=sp, tail_rows=tail_p, per_row=False)
        r3 = lambda a: a.reshape(bp, sp, a.shape[-1])
        oa = _diffattn(r3(qa), r3(ka16), ka_aug, r3(va16), cslopes, lam4, subln_g[l], lam_init, t=ta)
        ob = _bandattn(r3(qb), r3(kb16), r3(vb16), _band_bias_tiles(rel_bias[l], sp, tq), tq=tq)
        x1p, hp = _outproj(oa.reshape(tp, D_A), ob.reshape(tp, D_B), sga, sgb, yp, mod_p[2], mod_p[3], mod_p[4],
                           pa16, pb16, wo16, ln1_g[l], ln1_b[l], tm=_pick_tile(sp, 512), rows_per_seq=sp,
                           per_row=False, alpha=alpha)
        outs[0].append(ka32.reshape(bp, sp, H_A, 2, HEAD_DIM))
        outs[1].append(va32.reshape(bp, sp, H_A, 2 * HEAD_DIM))
        outs[2].append(kbt.reshape(bp, tail_p, H_B, HEAD_DIM))
        outs[3].append(vbt.reshape(bp, tail_p, H_B, HEAD_DIM))

        (qa_s, ka32_s, va32_s, ka16_s, va16_s, qb_s, kb16_s, vb16_s, sga_s, sgb_s, kbt_s, vbt_s) = _inproj(
            ys, mod_s[0], mod_s[1], w_in16, b_in2, tm=ts_, rows_per_seq=ts_, tail_rows=ts_, per_row=True)
        s3 = lambda a: a.reshape(bs, ss, a.shape[-1])
        oa_s, ob_s = _sampattn(
            s3(qa_s), cache_k_diff[l].reshape(bs, past, D_A), cache_v_diff[l].reshape(bs, past, D_A),
            s3(ka16_s), s3(va16_s), s3(qb_s), cache_k_band[l].reshape(bs, lb, D_B),
            cache_v_band[l].reshape(bs, lb, D_B), s3(kb16_s), s3(vb16_s), cslopes, lam4, subln_g[l],
            _sample_band_bias(rel_bias[l], ss, past, lb), lam_init)
        x1s, hs = _outproj(oa_s.reshape(ts_, D_A), ob_s.reshape(ts_, D_B), sga_s, sgb_s, ys, mod_s[2], mod_s[3],
                           mod_s[4], pa16, pb16, wo16, ln1_g[l], ln1_b[l], tm=ts_, rows_per_seq=ts_, per_row=True,
                           alpha=alpha)
        outs[4].append(ka32_s.reshape(bs, ss, H_A, 2, HEAD_DIM))
        outs[5].append(va32_s.reshape(bs, ss, H_A, 2 * HEAD_DIM))
        outs[6].append(kbt_s.reshape(bs, ss, H_B, HEAD_DIM))
        outs[7].append(vbt_s.reshape(bs, ss, H_B, HEAD_DIM))

        h_all = jnp.concatenate([hp, hs], axis=0)
        tm_r = _pick_tile(math.gcd(sp, ts_), 256)
        n_tiles = t_all // tm_r
        wt_t, pos_t, tab_raw, cnt = _router(h_all, w_router[l], e_bias[l], tr=tm_r)
        counts = cnt[:, 0]
        padded = (counts + EXPERT_BLOCK - 1) // EXPERT_BLOCK * EXPERT_BLOCK
        pad_end = jnp.cumsum(padded)
        pad_start = pad_end - padded
        max_slots = t_all * TOP_K + n_tiles * N_EXPERTS * (ROW_ALIGN - 1)
        nb = -(-max_slots // EXPERT_BLOCK) + N_EXPERTS
        n_slots = nb * EXPERT_BLOCK
        block_starts = jnp.arange(nb, dtype=jnp.int32) * EXPERT_BLOCK
        block_e = jnp.minimum(jnp.sum(pad_end[None, :] <= block_starts[:, None], axis=1),
                              N_EXPERTS - 1).astype(jnp.int32)
        n_used = (pad_end[-1:] // EXPERT_BLOCK).astype(jnp.int32)
        tab = jnp.stack([tab_raw[:, :, 0] + pad_start[None, :], tab_raw[:, :, 1], tab_raw[:, :, 2]], axis=1)
        tab = tab.reshape(n_tiles, 3 * N_EXPERTS).astype(jnp.int32)
        xs = _dispatch(h_all, pos_t, tab, (pad_start + counts).astype(jnp.int32), pad_end.astype(jnp.int32),
                       n_slots, tm=tm_r)
        w13 = jnp.concatenate([w1[l], w3[l]], axis=-1).astype(BF16)
        yb = _experts(xs, w13, w2[l].astype(BF16), block_e, n_used)
        wt, pos = wt_t.T, pos_t.T
        ws13 = jnp.concatenate([ws1[l], ws3[l]], axis=-1).astype(BF16)
        ws2b = ws2[l].astype(BF16)
        yp = _final(tab, yb, h_all, x1p, pos, wt, mod_p[5], ws13, ws2b, ln2_g[l], ln2_b[l], tm=tm_r, row0=0,
                    rows_per_seq=sp, per_row=False, alpha=alpha)
        ys = _final(tab, yb, h_all, x1s, pos, wt, mod_s[5], ws13, ws2b, ln2_g[l], ln2_b[l], tm=tm_r, row0=tp,
                    rows_per_seq=ts_, per_row=True, alpha=alpha)

    return (yp.reshape(bp, sp, d), ys.reshape(bs, ss, d)) + tuple(jnp.stack(o) for o in outs)
```

```python
import functools
import math

import jax
import jax.numpy as jnp
import numpy as np
from jax import lax
from jax.experimental import pallas as pl
from jax.experimental.pallas import tpu as pltpu

F32 = jnp.float32
BF16 = jnp.bfloat16

D_MODEL = 1024
CHUNK = 64
HEAD_DIM = 64
H_A = 8
H_B = 8
D_A = H_A * 2 * HEAD_DIM
D_B = H_B * HEAD_DIM
BAND_CHUNKS = 8
REL_CLIP = 128
N_IN = 3 * D_A + 3 * D_B + 2 * D_MODEL
N_EXPERTS = 64
TOP_K = 8
N_GROUP = 8
TOPK_GROUP = 4
GROUP_SIZE = N_EXPERTS // N_GROUP
D_EXPERT = 256
ROUTED_SCALE = 2.5
EXPERT_BLOCK = 512
ROW_ALIGN = 8
SORT_CHUNK = 256
LN_EPS = 1e-5
NEG = -1e30
LOG2E = math.log2(math.e)
Q_SCALE = HEAD_DIM ** -0.5 * LOG2E

VMEM_LIMIT = 56 * 1024 * 1024


def _cparams(sem):
    return pltpu.CompilerParams(dimension_semantics=sem, vmem_limit_bytes=VMEM_LIMIT)


def _ln(x):
    mu = jnp.mean(x, axis=-1, keepdims=True)
    xc = x - mu
    var = jnp.mean(xc * xc, axis=-1, keepdims=True)
    return xc * lax.rsqrt(var + LN_EPS)


def _rows(ref):
    v = ref[...]
    return v.reshape(v.shape[-2], v.shape[-1])


def _split_bf16(x):
    hi = x.astype(BF16)
    lo = (x - hi.astype(F32)).astype(BF16)
    return hi, lo


def _dot(a, b):
    return jnp.dot(a, b, preferred_element_type=F32)


def _dot_nt(a, b):
    return lax.dot_general(a, b, (((1,), (1,)), ((), ())), preferred_element_type=F32)


PACKED = D_MODEL // 2


def _pack_pair(a, b):
    ua = lax.bitcast_convert_type(a.astype(BF16).astype(F32), jnp.uint32)
    ub = lax.bitcast_convert_type(b.astype(BF16).astype(F32), jnp.uint32)
    return ua | (ub >> 16)


def _unpack_pair(u):
    a = lax.bitcast_convert_type(u & jnp.uint32(0xFFFF0000), F32).astype(BF16)
    b = lax.bitcast_convert_type(u << 16, F32).astype(BF16)
    return a, b


def _two_group_specs(tm, n_first):
    return (pl.BlockSpec((tm, PACKED), lambda i, *_: (jnp.minimum(i, n_first - 1), 0)),
            pl.BlockSpec((tm, PACKED), lambda i, *_: (jnp.maximum(i - n_first, 0), 0)))


def _two_group_rows(first_ref, second_ref, n_first):
    return jnp.where(pl.program_id(0) < n_first, first_ref[...], second_ref[...])


def _dot_packed(u, w):
    a, b = _unpack_pair(u)
    return _dot(a, w[:PACKED]) + _dot(b, w[PACKED:])


def _ada_kernel(c_ref, w_ref, b_ref, o_ref):
    c = c_ref[...]
    a = c * jax.nn.sigmoid(c)
    a_hi, a_lo = _split_bf16(a)
    w_hi, w_lo = _split_bf16(w_ref[...])
    o_ref[...] = _dot(a_hi, w_hi) + _dot(a_lo, w_hi) + _dot(a_hi, w_lo) + b_ref[...]


def _ada(c, w_ada, b_ada):
    n, d = c.shape
    nout = w_ada.shape[1]
    tn = 1024
    return pl.pallas_call(
        _ada_kernel,
        out_shape=jax.ShapeDtypeStruct((n, nout), F32),
        grid=(nout // tn,),
        in_specs=[pl.BlockSpec((n, d), lambda j: (0, 0)),
                  pl.BlockSpec((d, tn), lambda j: (0, j)),
                  pl.BlockSpec((1, tn), lambda j: (0, j))],
        out_specs=pl.BlockSpec((n, tn), lambda j: (0, j)),
        compiler_params=_cparams(("parallel",)),
        name="ada",
    )(c, w_ada, b_ada.reshape(1, nout))


_SEG_QA = (0, D_A)
_SEG_KA = (D_A, 2 * D_A)
_SEG_VA = (2 * D_A, 3 * D_A)
_SEG_QB = (3 * D_A, 3 * D_A + D_B)
_SEG_KB = (3 * D_A + D_B, 3 * D_A + 2 * D_B)
_SEG_VB = (3 * D_A + 2 * D_B, 3 * D_A + 3 * D_B)
_SEG_GA = (3 * D_A + 3 * D_B, 3 * D_A + 3 * D_B + D_MODEL)
_SEG_GB = (3 * D_A + 3 * D_B + D_MODEL, N_IN)


def _inproj_kernel(x_ref, sh_ref, sc_ref, w_ref, b_ref,
                   qa_ref, ka32_ref, va32_ref, ka16_ref, va16_ref,
                   qb_ref, kb16_ref, vb16_ref, sga_ref, sgb_ref, kbt_ref, vbt_ref, *, tiles_per_seq, tail_tiles):
    u = (_ln(x_ref[...]) * (1.0 + _rows(sc_ref)) + _rows(sh_ref)).astype(BF16)

    def seg(lo_hi):
        lo, hi = lo_hi
        return _dot(u, w_ref[:, lo:hi]) + b_ref[:, lo:hi]

    qa_ref[...] = (seg(_SEG_QA) * Q_SCALE).astype(BF16)
    ka = seg(_SEG_KA)
    for hj in range(2 * H_A):
        ka32_ref[:, hj // 2, hj % 2, :] = ka[:, hj * HEAD_DIM:(hj + 1) * HEAD_DIM]
    ka16_ref[...] = ka.astype(BF16)
    va = seg(_SEG_VA)
    va32_ref[...] = va
    va16_ref[...] = va.astype(BF16)
    qb_ref[...] = (seg(_SEG_QB) * Q_SCALE).astype(BF16)
    kb = seg(_SEG_KB)
    kb16_ref[...] = kb.astype(BF16)
    vb = seg(_SEG_VB)
    vb16_ref[...] = vb.astype(BF16)
    @pl.when(pl.program_id(0) % tiles_per_seq >= tiles_per_seq - tail_tiles)
    def _():
        for h in range(H_B):
            kbt_ref[:, h, :] = kb[:, h * HEAD_DIM:(h + 1) * HEAD_DIM]
            vbt_ref[:, h, :] = vb[:, h * HEAD_DIM:(h + 1) * HEAD_DIM]
    sga_ref[...] = jax.nn.sigmoid(seg(_SEG_GA)).astype(BF16)
    sgb_ref[...] = jax.nn.sigmoid(seg(_SEG_GB)).astype(BF16)


def _mod_spec(per_row, tm, rows_per_seq):
    if per_row:
        return pl.BlockSpec((tm, D_MODEL), lambda i: (i, 0))
    tiles_per_seq = rows_per_seq // tm
    return pl.BlockSpec((1, 1, D_MODEL), lambda i: (i // tiles_per_seq, 0, 0))


def _inproj(x, shift, scale, w_in16, b_in, *, tm, rows_per_seq, tail_rows, per_row):
    r = x.shape[0]
    nseq = r // rows_per_seq
    n_i = rows_per_seq // tm
    n_t = tail_rows // tm

    def tail_map(i):
        return ((i // n_i) * n_t + jnp.maximum(i % n_i - (n_i - n_t), 0), 0, 0)

    row = lambda w: pl.BlockSpec((tm, w), lambda i: (i, 0))
    out_shape = (
        jax.ShapeDtypeStruct((r, D_A), BF16),
        jax.ShapeDtypeStruct((r, H_A, 2, HEAD_DIM), F32),
        jax.ShapeDtypeStruct((r, D_A), F32),
        jax.ShapeDtypeStruct((r, D_A), BF16),
        jax.ShapeDtypeStruct((r, D_A), BF16),
        jax.ShapeDtypeStruct((r, D_B), BF16),
        jax.ShapeDtypeStruct((r, D_B), BF16),
        jax.ShapeDtypeStruct((r, D_B), BF16),
        jax.ShapeDtypeStruct((r, D_MODEL), BF16),
        jax.ShapeDtypeStruct((r, D_MODEL), BF16),
        jax.ShapeDtypeStruct((nseq * tail_rows, H_B, HEAD_DIM), F32),
        jax.ShapeDtypeStruct((nseq * tail_rows, H_B, HEAD_DIM), F32),
    )
    out_specs = (row(D_A), pl.BlockSpec((tm, H_A, 2, HEAD_DIM), lambda i: (i, 0, 0, 0)), row(D_A), row(D_A), row(D_A),
                 row(D_B), row(D_B), row(D_B), row(D_MODEL), row(D_MODEL),
                 pl.BlockSpec((tm, H_B, HEAD_DIM), tail_map), pl.BlockSpec((tm, H_B, HEAD_DIM), tail_map))
    return pl.pallas_call(
        functools.partial(_inproj_kernel, tiles_per_seq=n_i, tail_tiles=n_t),
        out_shape=out_shape,
        grid=(r // tm,),
        in_specs=[row(D_MODEL), _mod_spec(per_row, tm, rows_per_seq), _mod_spec(per_row, tm, rows_per_seq),
                  pl.BlockSpec((D_MODEL, N_IN), lambda i: (0, 0)),
                  pl.BlockSpec((1, N_IN), lambda i: (0, 0))],
        out_specs=out_specs,
        compiler_params=_cparams(("arbitrary",)),
        name="inproj",
    )(x, shift, scale, w_in16, b_in)


N_AUG = 3


def _lam_value(lam4_ref, lam_init):
    v = lam4_ref[...]
    a = jnp.sum(v[0:1, :] * v[1:2, :], axis=-1, keepdims=True)
    b = jnp.sum(v[2:3, :] * v[3:4, :], axis=-1, keepdims=True)
    return jnp.exp(a) - jnp.exp(b) + lam_init


def _half_masks(shape):
    lane = lax.broadcasted_iota(jnp.int32, shape, 1)
    return lane < HEAD_DIM, lane >= HEAD_DIM


def _stack_halves(q):
    lo, hi = _half_masks(q.shape)
    zero = jnp.zeros_like(q)
    return jnp.concatenate([jnp.where(lo, q, zero), jnp.where(hi, q, zero)], axis=0)


def _softmax_step(s, v, m, l, acc):
    m_new = jnp.maximum(m, jnp.max(s, axis=-1, keepdims=True))
    alpha = jnp.exp2(m - m_new)
    p = jnp.exp2(s - m_new)
    l_new = alpha * l + jnp.sum(p, axis=-1, keepdims=True)
    acc_new = alpha * acc + _dot(p.astype(BF16), v)
    return m_new, l_new, acc_new


def _diff_finish(o0, o1, lam, g, lam_init):
    o = o0 - lam * o1
    o = o * lax.rsqrt(jnp.mean(o * o, axis=-1, keepdims=True) + LN_EPS)
    return o * g * (1.0 - lam_init)


def _key_bias_columns(slopes, s):
    x = (np.float32(LOG2E) * slopes.astype(np.float32))[:, None] * np.arange(s, dtype=np.float32)[None, :]
    cols = np.zeros(x.shape + (2 * HEAD_DIM,), np.float32)
    for c in range(N_AUG):
        cols[:, :, c] = x.astype(BF16).astype(np.float32)
        x = x - cols[:, :, c]
    return jnp.asarray(cols, BF16)


def _own_block_fix(cslopes, t):
    ii = lax.broadcasted_iota(jnp.int32, (t, t), 0)
    jj = lax.broadcasted_iota(jnp.int32, (t, t), 1)
    ahead = jnp.where(jj > ii, 2.0 * (ii - jj).astype(F32), 0.0)
    return jnp.where(((jj // CHUNK) <= (ii // CHUNK))[None], cslopes[:, None, None] * ahead[None], NEG)


def _diffattn_kernel(lam4_ref, g_ref, fix_ref, q_ref, k_ref, ka_ref, v_ref, o_ref, *, t, lam_init):
    qi = pl.program_id(2)
    q = q_ref[0]
    lo, hi = _half_masks(q.shape)
    lane = lax.broadcasted_iota(jnp.int32, q.shape, 1)
    zero = jnp.zeros_like(q)
    ones = (lane < N_AUG).astype(BF16)
    q_ext = (jnp.concatenate([jnp.where(lo, q, zero), ones], axis=1),
             jnp.concatenate([jnp.where(hi, q, zero), ones], axis=1))

    def block(j, carry, extra):
        start = pl.multiple_of(j * t, t)
        k_ext = jnp.concatenate([k_ref[0, pl.ds(start, t), :], ka_ref[0, pl.ds(start, t), :]], axis=1)
        vj = v_ref[0, pl.ds(start, t), :]
        out = []
        for half in range(2):
            s = _dot_nt(q_ext[half], k_ext)
            if extra is not None:
                s = s + extra
            out.append(_softmax_step(s, vj, *carry[half]))
        return tuple(out)

    init1 = (jnp.full((t, 1), NEG, F32), jnp.zeros((t, 1), F32), jnp.zeros((t, 2 * HEAD_DIM), F32))
    carry = lax.fori_loop(0, qi, lambda j, c: block(j, c, None), (init1, init1))

    (m0, l0, a0), (m1, l1, a1) = block(qi, carry, fix_ref[0])

    lam = _lam_value(lam4_ref, lam_init)
    o_ref[0] = _diff_finish(a0 / l0, a1 / l1, lam, g_ref[...], lam_init).astype(o_ref.dtype)


def _diffattn(qa, ka, ka_aug, va, cslopes, lam4, subln_g, lam_init, *, t):
    b, s, _ = qa.shape
    kern = functools.partial(_diffattn_kernel, t=t, lam_init=lam_init)
    hd2 = 2 * HEAD_DIM
    return pl.pallas_call(
        kern,
        out_shape=jax.ShapeDtypeStruct((b, s, D_A), BF16),
        grid=(b, H_A, s // t),
        in_specs=[pl.BlockSpec((4, HEAD_DIM), lambda bi, h, i: (0, 0)),
                  pl.BlockSpec((1, hd2), lambda bi, h, i: (0, 0)),
                  pl.BlockSpec((1, t, t), lambda bi, h, i: (h, 0, 0)),
                  pl.BlockSpec((1, t, hd2), lambda bi, h, i: (bi, i, h)),
                  pl.BlockSpec((1, s, hd2), lambda bi, h, i: (bi, 0, h)),
                  pl.BlockSpec((1, s, hd2), lambda bi, h, i: (h, 0, 0)),
                  pl.BlockSpec((1, s, hd2), lambda bi, h, i: (bi, 0, h))],
        out_specs=pl.BlockSpec((1, t, hd2), lambda bi, h, i: (bi, i, h)),
        compiler_params=_cparams(("parallel", "parallel", "arbitrary")),
        name="diffattn",
    )(lam4, subln_g.reshape(1, hd2), _own_block_fix(cslopes, t), qa, ka, ka_aug, va)


def _toeplitz_bias(rel_table, rows, cols, delta):
    length = rows + cols - 1
    rel = np.clip(delta + rows - 1 - np.arange(length), -REL_CLIP, REL_CLIP) + REL_CLIP
    u = jnp.take(rel_table.astype(F32) * LOG2E, jnp.asarray(rel, jnp.int32), axis=1)
    u = jnp.pad(u, ((0, 0), (0, 1)))
    r = jnp.tile(u, (1, rows))[:, :rows * length].reshape(-1, rows, length)
    return r[:, :, rows - 1:rows - 1 + cols]


def _band_valid(t_pos, s_pos):
    return (s_pos // CHUNK <= t_pos // CHUNK) & (s_pos // CHUNK >= t_pos // CHUNK - BAND_CHUNKS)


def _band_window(s, tq):
    return min(BAND_CHUNKS * CHUNK + tq, s)


def _band_bias_tiles(rel_table, s, tq):
    win = _band_window(s, tq)
    n_var = min(BAND_CHUNKS * CHUNK // tq + 1, s // tq)
    tiles = []
    for var in range(n_var):
        q0 = var * tq
        k0 = max(q0 - BAND_CHUNKS * CHUNK, 0)
        valid = _band_valid(q0 + np.arange(tq)[:, None], k0 + np.arange(win)[None, :])
        tiles.append(jnp.where(jnp.asarray(valid)[None], _toeplitz_bias(rel_table, tq, win, q0 - k0), NEG))
    return jnp.stack(tiles)


def _pair_softmax_out(qm, kw, vw, bias2, t):
    s = (_dot_nt(qm, kw).reshape(2, t, -1) + bias2).reshape(2 * t, -1)
    m = jnp.max(s, axis=-1, keepdims=True)
    p = jnp.exp2(s - m)
    l = jnp.sum(p, axis=-1, keepdims=True)
    o = _dot(p.astype(BF16), vw) / l
    lo, _ = _half_masks((t, 2 * HEAD_DIM))
    return jnp.where(lo, o[:t], o[t:])


def _bandattn_kernel(bias_ref, q_ref, k_ref, v_ref, o_ref, *, tq, win, n_var):
    qi = pl.program_id(2)
    var = jnp.minimum(qi, n_var - 1)
    k0 = pl.multiple_of(jnp.maximum(qi * tq - BAND_CHUNKS * CHUNK, 0), tq)
    qm = _stack_halves(q_ref[0])
    kw = k_ref[0, pl.ds(k0, win), :]
    vw = v_ref[0, pl.ds(k0, win), :]
    o_ref[0] = _pair_softmax_out(qm, kw, vw, bias_ref[var], tq).astype(o_ref.dtype)


def _bandattn(qb, kb, vb, bias_tiles, *, tq):
    b, s, _ = qb.shape
    n_var, _, _, win = bias_tiles.shape
    kern = functools.partial(_bandattn_kernel, tq=tq, win=win, n_var=n_var)
    return pl.pallas_call(
        kern,
        out_shape=jax.ShapeDtypeStruct((b, s, D_B), BF16),
        grid=(H_B // 2, b, s // tq),
        in_specs=[pl.BlockSpec((n_var, 2, tq, win), lambda hp, bi, i: (0, hp, 0, 0)),
                  pl.BlockSpec((1, tq, 2 * HEAD_DIM), lambda hp, bi, i: (bi, i, hp)),
                  pl.BlockSpec((1, s, 2 * HEAD_DIM), lambda hp, bi, i: (bi, 0, hp)),
                  pl.BlockSpec((1, s, 2 * HEAD_DIM), lambda hp, bi, i: (bi, 0, hp))],
        out_specs=pl.BlockSpec((1, tq, 2 * HEAD_DIM), lambda hp, bi, i: (bi, i, hp)),
        compiler_params=_cparams(("parallel", "parallel", "arbitrary")),
        name="bandattn",
    )(bias_tiles, qb, kb, vb)


def _two_part_softmax(s_c, s_n, vc, vn):
    m = jnp.maximum(jnp.max(s_c, axis=-1, keepdims=True), jnp.max(s_n, axis=-1, keepdims=True))
    p_c = jnp.exp2(s_c - m)
    p_n = jnp.exp2(s_n - m)
    l = jnp.sum(p_c, axis=-1, keepdims=True) + jnp.sum(p_n, axis=-1, keepdims=True)
    return (_dot(p_c.astype(BF16), vc) + _dot(p_n.astype(BF16), vn)) / l


def _sampattn_kernel(cs_ref, lam4_ref, g_ref, bbias_ref, qa_ref, ckd_ref, cvd_ref, kan_ref, van_ref,
                     qb_ref, ckb_ref, cvb_ref, kbn_ref, vbn_ref, oa_ref, ob_ref, *, n, past, lam_init):
    lam = _lam_value(lam4_ref, lam_init)
    tq_pos = past + lax.broadcasted_iota(jnp.int32, (n, past), 0)
    ts_pos = lax.broadcasted_iota(jnp.int32, (n, past), 1)
    dist_c = jnp.abs(tq_pos - ts_pos).astype(F32)
    vis_c = (ts_pos // CHUNK) <= (tq_pos // CHUNK)
    ii = lax.broadcasted_iota(jnp.int32, (n, n), 0)
    jj = lax.broadcasted_iota(jnp.int32, (n, n), 1)
    dist_n = jnp.abs(ii - jj).astype(F32)
    vis_n = ((past + jj) // CHUNK) <= ((past + ii) // CHUNK)
    for h in range(H_A):
        sl = slice(h * 2 * HEAD_DIM, (h + 1) * 2 * HEAD_DIM)
        cs = cs_ref[h]
        qm = _stack_halves(qa_ref[0, :, sl])
        bias_c = jnp.where(vis_c, -cs * dist_c, NEG)
        bias_n = jnp.where(vis_n, -cs * dist_n, NEG)
        s_c = (_dot_nt(qm, ckd_ref[0, :, sl].astype(BF16)).reshape(2, n, past) + bias_c[None]).reshape(2 * n, past)
        s_n = (_dot_nt(qm, kan_ref[0, :, sl]).reshape(2, n, n) + bias_n[None]).reshape(2 * n, n)
        o = _two_part_softmax(s_c, s_n, cvd_ref[0, :, sl].astype(BF16), van_ref[0, :, sl])
        oa_ref[0, :, sl] = _diff_finish(o[:n], o[n:], lam, g_ref[...], lam_init).astype(oa_ref.dtype)
    lb = ckb_ref.shape[1]
    for hp in range(H_B // 2):
        sl = slice(hp * 2 * HEAD_DIM, (hp + 1) * 2 * HEAD_DIM)
        qm = _stack_halves(qb_ref[0, :, sl])
        bias = bbias_ref[2 * hp:2 * hp + 2]
        s_c = (_dot_nt(qm, ckb_ref[0, :, sl].astype(BF16)).reshape(2, n, lb) + bias[:, :, :lb]).reshape(2 * n, lb)
        s_n = (_dot_nt(qm, kbn_ref[0, :, sl]).reshape(2, n, n) + bias[:, :, lb:]).reshape(2 * n, n)
        o = _two_part_softmax(s_c, s_n, cvb_ref[0, :, sl].astype(BF16), vbn_ref[0, :, sl])
        lo, _ = _half_masks((n, 2 * HEAD_DIM))
        ob_ref[0, :, sl] = jnp.where(lo, o[:n], o[n:]).astype(ob_ref.dtype)


def _sample_band_bias(rel_table, n, past, lb):
    valid = _band_valid(past + np.arange(n)[:, None], (past - lb) + np.arange(lb + n)[None, :])
    return jnp.where(jnp.asarray(valid)[None], _toeplitz_bias(rel_table, n, lb + n, lb), NEG)


def _sampattn(qa, ckd, cvd, kan, van, qb, ckb, cvb, kbn, vbn, cslopes, lam4, subln_g, bbias, lam_init):
    b, n, _ = qa.shape
    past = ckd.shape[1]
    lb = ckb.shape[1]
    kern = functools.partial(_sampattn_kernel, n=n, past=past, lam_init=lam_init)
    full = lambda shape: pl.BlockSpec(shape, lambda bi: (0,) * len(shape))
    per_b = lambda r, w: pl.BlockSpec((1, r, w), lambda bi: (bi, 0, 0))
    return pl.pallas_call(
        kern,
        out_shape=(jax.ShapeDtypeStruct((b, n, D_A), BF16), jax.ShapeDtypeStruct((b, n, D_B), BF16)),
        grid=(b,),
        in_specs=[pl.BlockSpec(memory_space=pltpu.SMEM), full((4, HEAD_DIM)), full((1, 2 * HEAD_DIM)),
                  full((H_B, n, lb + n)),
                  per_b(n, D_A), per_b(past, D_A), per_b(past, D_A), per_b(n, D_A), per_b(n, D_A),
                  per_b(n, D_B), per_b(lb, D_B), per_b(lb, D_B), per_b(n, D_B), per_b(n, D_B)],
        out_specs=(per_b(n, D_A), per_b(n, D_B)),
        compiler_params=_cparams(("parallel",)),
        name="sampattn",
    )(cslopes, lam4, subln_g.reshape(1, 2 * HEAD_DIM), bbias, qa, ckd, cvd, kan, van, qb, ckb, cvb, kbn, vbn)


def _outproj_kernel(oa_ref, ob_ref, sga_ref, sgb_ref, x_ref, g1_ref, sh2_ref, sc2_ref, pa_ref, pb_ref, wo_ref,
                    lng_ref, lnb_ref, x1_ref, h_ref, *, alpha):
    mix = (sga_ref[...].astype(F32) * _dot(oa_ref[...], pa_ref[...])
           + sgb_ref[...].astype(F32) * _dot(ob_ref[...], pb_ref[...]))
    y = _dot(mix.astype(BF16), wo_ref[...])
    x1 = _ln(alpha * x_ref[...] + _rows(g1_ref) * y) * lng_ref[...] + lnb_ref[...]
    x1_ref[...] = x1
    u2 = _ln(x1) * (1.0 + _rows(sc2_ref)) + _rows(sh2_ref)
    h_ref[...] = _pack_pair(u2[:, :PACKED], u2[:, PACKED:])


def _outproj(oa, ob, sga, sgb, x, gate1, shift2, scale2, pa16, pb16, wo16, ln_g, ln_b, *, tm, rows_per_seq,
             per_row, alpha):
    r = x.shape[0]
    row = lambda w: pl.BlockSpec((tm, w), lambda i: (i, 0))
    full = lambda a, b: pl.BlockSpec((a, b), lambda i: (0, 0))
    mod = _mod_spec(per_row, tm, rows_per_seq)
    return pl.pallas_call(
        functools.partial(_outproj_kernel, alpha=alpha),
        out_shape=(jax.ShapeDtypeStruct((r, D_MODEL), F32), jax.ShapeDtypeStruct((r, PACKED), jnp.uint32)),
        grid=(r // tm,),
        in_specs=[row(D_A), row(D_B), row(D_MODEL), row(D_MODEL), row(D_MODEL), mod, mod, mod,
                  full(D_A, D_MODEL), full(D_B, D_MODEL), full(D_MODEL, D_MODEL),
                  full(1, D_MODEL), full(1, D_MODEL)],
        out_specs=(row(D_MODEL), row(PACKED)),
        compiler_params=_cparams(("parallel",)),
        name="outproj",
    )(oa, ob, sga, sgb, x, gate1, shift2, scale2, pa16, pb16, wo16, ln_g.reshape(1, -1), ln_b.reshape(1, -1))


def _first_index_of_max(x, iota, axis, size):
    m = jnp.max(x, axis=axis, keepdims=True)
    idx = jnp.min(jnp.where(x == m, iota, size), axis=axis, keepdims=True)
    return m, idx


def _router_kernel(hp_ref, hs_ref, wr_hi_ref, wr_lo_ref, eb_ref, wt_ref, pos_ref, tab_ref, cnt_ref, run_ref, *, tr,
                   n_first):
    step = pl.program_id(0)

    @pl.when(step == 0)
    def _():
        run_ref[...] = jnp.zeros_like(run_ref)

    ha, hb = _unpack_pair(_two_group_rows(hp_ref, hs_ref, n_first))
    wr_hi, wr_lo = wr_hi_ref[...], wr_lo_ref[...]
    logits = (_dot_nt(wr_hi[:, :PACKED], ha) + _dot_nt(wr_hi[:, PACKED:], hb)
              + _dot_nt(wr_lo[:, :PACKED], ha) + _dot_nt(wr_lo[:, PACKED:], hb))
    scores = jax.nn.sigmoid(logits)
    biased = scores + eb_ref[...]

    x3 = biased.reshape(N_GROUP, GROUP_SIZE, tr)
    io3 = lax.broadcasted_iota(jnp.int32, x3.shape, 1)
    m1, i1 = _first_index_of_max(x3, io3, 1, GROUP_SIZE)
    m2 = jnp.max(jnp.where(io3 == i1, -jnp.inf, x3), axis=1, keepdims=True)
    grp = (m1 + m2).reshape(N_GROUP, tr)

    iog = lax.broadcasted_iota(jnp.int32, grp.shape, 0)
    gsel = jnp.zeros(grp.shape, jnp.bool_)
    for _ in range(TOPK_GROUP):
        _, gi = _first_index_of_max(grp, iog, 0, N_GROUP)
        hit = iog == gi
        gsel = gsel | hit
        grp = jnp.where(hit, -jnp.inf, grp)
    emask = jnp.broadcast_to(gsel.reshape(N_GROUP, 1, tr), (N_GROUP, GROUP_SIZE, tr)).reshape(N_EXPERTS, tr)
    cand = jnp.where(emask, biased, -jnp.inf)

    ioe = lax.broadcasted_iota(jnp.int32, cand.shape, 0)
    hits = []
    sel = jnp.zeros(cand.shape, jnp.bool_)
    for _ in range(TOP_K):
        _, ei = _first_index_of_max(cand, ioe, 0, N_EXPERTS)
        hit = ioe == ei
        hits.append((ei, hit))
        sel = sel | hit
        cand = jnp.where(hit, -jnp.inf, cand)
    self32 = sel.astype(F32)
    ra = lax.broadcasted_iota(jnp.int32, (tr, tr), 0)
    rb = lax.broadcasted_iota(jnp.int32, (tr, tr), 1)
    upper = (ra < rb).astype(BF16)
    prefix = _dot(self32.astype(BF16), upper)

    count = jnp.sum(self32, axis=1, keepdims=True)
    run_len = jnp.floor((count + (ROW_ALIGN - 1)) * (1.0 / ROW_ALIGN)) * ROW_ALIGN
    ea = lax.broadcasted_iota(jnp.int32, (N_EXPERTS, N_EXPERTS), 0)
    eb = lax.broadcasted_iota(jnp.int32, (N_EXPERTS, N_EXPERTS), 1)
    before = (eb < ea).astype(BF16)
    local_start = _dot(before, jnp.broadcast_to(run_len, (N_EXPERTS, 128)).astype(BF16))[:, :1]
    local_pos = local_start + prefix

    ws = [jnp.sum(jnp.where(hit, scores, 0.0), axis=0, keepdims=True) for _, hit in hits]
    wsum = ws[0]
    for w in ws[1:]:
        wsum = wsum + w
    for k, (_, hit) in enumerate(hits):
        wt_ref[k:k + 1, :] = ws[k] / wsum * ROUTED_SCALE
        pos_ref[k:k + 1, :] = jnp.sum(jnp.where(hit, local_pos, 0.0), axis=0, keepdims=True).astype(jnp.int32)
    run_old = run_ref[...]
    lane = lax.broadcasted_iota(jnp.int32, (N_EXPERTS, 128), 1)
    tab = jnp.where(lane == 0, run_old, jnp.where(lane == 1, local_start, jnp.where(lane == 2, run_len, 0.0)))
    tab_ref[0] = tab.astype(jnp.int32)
    run_new = run_old + run_len
    run_ref[...] = run_new
    cnt_ref[...] = jnp.broadcast_to(run_new, cnt_ref.shape).astype(jnp.int32)


def _router(hp, hs, w_router, e_bias, *, tr):
    t = hp.shape[0] + hs.shape[0]
    n_first = hp.shape[0] // tr
    wr_t = w_router.T
    hi32 = lax.bitcast_convert_type(lax.bitcast_convert_type(wr_t, jnp.uint32) & jnp.uint32(0xFFFF0000), F32)
    wr_hi = hi32.astype(BF16)
    wr_lo = (wr_t - hi32).astype(BF16)
    full = lambda a, b: pl.BlockSpec((a, b), lambda i: (0, 0))
    col = lambda: pl.BlockSpec((TOP_K, tr), lambda i: (0, i))
    return pl.pallas_call(
        functools.partial(_router_kernel, tr=tr, n_first=n_first),
        out_shape=(jax.ShapeDtypeStruct((TOP_K, t), F32),
                   jax.ShapeDtypeStruct((TOP_K, t), jnp.int32),
                   jax.ShapeDtypeStruct((t // tr, N_EXPERTS, 128), jnp.int32),
                   jax.ShapeDtypeStruct((N_EXPERTS, 128), jnp.int32)),
        grid=(t // tr,),
        in_specs=[*_two_group_specs(tr, n_first), full(N_EXPERTS, D_MODEL),
                  full(N_EXPERTS, D_MODEL), full(N_EXPERTS, 1)],
        out_specs=(col(), col(), pl.BlockSpec((1, N_EXPERTS, 128), lambda i: (i, 0, 0)), full(N_EXPERTS, 128)),
        scratch_shapes=[pltpu.VMEM((N_EXPERTS, 1), F32)],
        compiler_params=_cparams(("arbitrary",)),
        name="router",
    )(hp, hs, wr_hi, wr_lo, e_bias.reshape(N_EXPERTS, 1).astype(F32))


TAB_GLOBAL, TAB_LOCAL, TAB_LEN = 0, 1, 2


def _sorted_rows(tm):
    need = tm * TOP_K + N_EXPERTS * (ROW_ALIGN - 1)
    return -(-need // SORT_CHUNK) * SORT_CHUNK


def _run(tab_smem, s, e):
    return tuple(pl.multiple_of(tab_smem[s, r * N_EXPERTS + e], ROW_ALIGN) for r in (TAB_GLOBAL, TAB_LOCAL, TAB_LEN))


def _tile_rows(tab_smem, s):
    _, local, length = _run(tab_smem, s, N_EXPERTS - 1)
    return pl.multiple_of(local + length, ROW_ALIGN)


def _rows_copy(src_ref, src_row, dst_ref, dst_row, rows, sem):
    return pltpu.make_async_copy(src_ref.at[pl.ds(src_row, rows)], dst_ref.at[pl.ds(dst_row, rows)], sem)


def _dispatch_kernel(fill_lo_ref, fill_hi_ref, tab_hbm, pos_ref, hp_ref, hs_ref, xs_hbm, tab_smem, rows_smem,
                     sorted_ref, zero_ref, tsem, csem, zsem, *, tm, n_first):
    i = pl.program_id(0)
    n = pl.num_programs(0)
    slot = i % 2

    def tab_copy(tile, s):
        return pltpu.make_async_copy(tab_hbm.at[tile], tab_smem.at[s], tsem.at[s])

    def wait_runs(s):
        rows = rows_smem[s]

        @pl.when(rows > 0)
        def _():
            _rows_copy(sorted_ref.at[s], 0, xs_hbm, 0, pl.multiple_of(rows, ROW_ALIGN), csem.at[s]).wait()

    @pl.when(i == 0)
    def _():
        tab_copy(0, 0).start()
        zero_ref[...] = jnp.zeros_like(zero_ref)
        blk = zero_ref.shape[0]
        for e in range(N_EXPERTS):
            lo = pl.multiple_of(fill_lo_ref[e], ROW_ALIGN)
            rows = pl.multiple_of(fill_hi_ref[e] - lo, ROW_ALIGN)

            @pl.when(rows > 0)
            def _():
                cp = _rows_copy(zero_ref, 0, xs_hbm, lo, rows, zsem)
                cp.start()
                cp.wait()

        def tail(b, c):
            cp = _rows_copy(zero_ref, 0, xs_hbm, pl.multiple_of(b * blk, blk), blk, zsem)
            cp.start()
            cp.wait()
            return c

        lax.fori_loop(fill_hi_ref[N_EXPERTS - 1] // blk, xs_hbm.shape[0] // blk, tail, 0)

    tab_copy(i, slot).wait()

    @pl.when(i + 1 < n)
    def _():
        tab_copy(i + 1, 1 - slot).start()

    ha, hb = _unpack_pair(_two_group_rows(hp_ref, hs_ref, n_first))
    pos = pos_ref[...]
    rows = _tile_rows(tab_smem, slot)

    def sort_chunk(c, carry):
        r0 = pl.multiple_of(c * SORT_CHUNK, SORT_CHUNK)
        row = r0 + lax.broadcasted_iota(jnp.int32, (SORT_CHUNK, tm), 0)
        pick = jnp.zeros((SORT_CHUNK, tm), F32)
        for k in range(TOP_K):
            pick = jnp.where(row == pos[k:k + 1, :], 1.0, pick)
        pick = pick.astype(BF16)
        sorted_ref[slot, pl.ds(r0, SORT_CHUNK), :] = _pack_pair(_dot(pick, ha), _dot(pick, hb))
        return carry

    lax.fori_loop(0, (rows + SORT_CHUNK - 1) // SORT_CHUNK, sort_chunk, 0)

    for e in range(N_EXPERTS):
        dst, src, length = _run(tab_smem, slot, e)

        @pl.when(length > 0)
        def _():
            _rows_copy(sorted_ref.at[slot], src, xs_hbm, dst, length, csem.at[slot]).start()

    rows_smem[slot] = rows

    @pl.when(i > 0)
    def _():
        wait_runs(1 - slot)

    @pl.when(i == n - 1)
    def _():
        wait_runs(slot)


def _dispatch(hp, hs, pos_t, tab, fill_lo, fill_hi, n_slots, *, tm):
    t = hp.shape[0] + hs.shape[0]
    n_first = hp.shape[0] // tm
    return pl.pallas_call(
        functools.partial(_dispatch_kernel, tm=tm, n_first=n_first),
        out_shape=jax.ShapeDtypeStruct((n_slots, PACKED), jnp.uint32),
        grid_spec=pltpu.PrefetchScalarGridSpec(
            num_scalar_prefetch=2,
            grid=(t // tm,),
            in_specs=[pl.BlockSpec(memory_space=pl.ANY),
                      pl.BlockSpec((TOP_K, tm), lambda i, lo, hi: (0, i)),
                      *_two_group_specs(tm, n_first)],
            out_specs=pl.BlockSpec(memory_space=pl.ANY),
            scratch_shapes=[pltpu.SMEM((2, 3 * N_EXPERTS), jnp.int32), pltpu.SMEM((2,), jnp.int32),
                            pltpu.VMEM((2, _sorted_rows(tm), PACKED), jnp.uint32),
                            pltpu.VMEM((EXPERT_BLOCK, PACKED), jnp.uint32),
                            pltpu.SemaphoreType.DMA((2,)), pltpu.SemaphoreType.DMA((2,)),
                            pltpu.SemaphoreType.DMA(())],
        ),
        compiler_params=_cparams(("arbitrary",)),
        name="dispatch",
    )(fill_lo, fill_hi, tab, pos_t, hp, hs)


def _experts_kernel(be_ref, nu_ref, x_ref, w13_ref, w2_ref, y_ref):
    @pl.when(pl.program_id(0) < nu_ref[0])
    def _():
        a = _dot_packed(x_ref[...], w13_ref[0])
        hid = (a[:, :D_EXPERT] * jax.nn.sigmoid(a[:, :D_EXPERT])) * a[:, D_EXPERT:]
        y = _dot(hid.astype(BF16), w2_ref[0])
        y_ref[...] = _pack_pair(y[:, :PACKED], y[:, PACKED:])

    @pl.when(pl.program_id(0) >= nu_ref[0])
    def _():
        y_ref[...] = jnp.zeros_like(y_ref)


def _experts(xs, w13, w2, block_e, n_used):
    n_slots = xs.shape[0]
    nb = n_slots // EXPERT_BLOCK
    last = lambda i, nu: jnp.minimum(i, nu[0] - 1)
    return pl.pallas_call(
        _experts_kernel,
        out_shape=jax.ShapeDtypeStruct((n_slots, PACKED), jnp.uint32),
        grid_spec=pltpu.PrefetchScalarGridSpec(
            num_scalar_prefetch=2,
            grid=(nb,),
            in_specs=[pl.BlockSpec((EXPERT_BLOCK, PACKED), lambda i, be, nu: (last(i, nu), 0)),
                      pl.BlockSpec((1, D_MODEL, 2 * D_EXPERT), lambda i, be, nu: (be[last(i, nu)], 0, 0)),
                      pl.BlockSpec((1, D_EXPERT, D_MODEL), lambda i, be, nu: (be[last(i, nu)], 0, 0))],
            out_specs=pl.BlockSpec((EXPERT_BLOCK, PACKED), lambda i, be, nu: (i, 0)),
        ),
        compiler_params=_cparams(("arbitrary",)),
        name="experts",
    )(block_e, n_used, xs, w13, w2)


def _final_kernel(tab_hbm, yb_hbm, h_ref, x1_ref, pos_ref, wt_ref, g2_ref, ws13_ref, ws2_ref, lng_ref, lnb_ref,
                  o_ref, tab_smem, rows_smem, ybuf, tsem, gsem, *, alpha, tm, off):
    i = pl.program_id(0)
    n = pl.num_programs(0)
    slot = i % 2

    def tab_copy(tile, s):
        return pltpu.make_async_copy(tab_hbm.at[tile + off], tab_smem.at[s], tsem.at[s])

    def fetch_runs(s):
        for e in range(N_EXPERTS):
            src, dst, length = _run(tab_smem, s, e)

            @pl.when(length > 0)
            def _():
                _rows_copy(yb_hbm, src, ybuf.at[s], dst, length, gsem.at[s]).start()

        rows_smem[s] = _tile_rows(tab_smem, s)

    @pl.when(i == 0)
    def _():
        ybuf[...] = jnp.zeros_like(ybuf)
        tab_copy(0, 0).start()
        tab_copy(0, 0).wait()
        fetch_runs(0)

        @pl.when(n > 1)
        def _():
            tab_copy(1, 1).start()

    @pl.when(i + 1 < n)
    def _():
        tab_copy(i + 1, 1 - slot).wait()
        fetch_runs(1 - slot)

        @pl.when(i + 2 < n)
        def _():
            tab_copy(i + 2, slot).start()

    a = _dot_packed(h_ref[...], ws13_ref[...])
    hid = (a[:, :D_EXPERT] * jax.nn.sigmoid(a[:, :D_EXPERT])) * a[:, D_EXPERT:]
    y = _dot(hid.astype(BF16), ws2_ref[...])

    rows = rows_smem[slot]

    @pl.when(rows > 0)
    def _():
        _rows_copy(yb_hbm, 0, ybuf.at[slot], 0, pl.multiple_of(rows, ROW_ALIGN), gsem.at[slot]).wait()

    pos = pos_ref[...]
    wt = wt_ref[...]
    lanes = 128
    pos_b = [jnp.broadcast_to(pos[:, k:k + 1], (tm, lanes)) for k in range(TOP_K)]
    wt_b = [jnp.broadcast_to(wt[:, k:k + 1], (tm, lanes)) for k in range(TOP_K)]
    lane_id = lax.broadcasted_iota(jnp.int32, (tm, lanes), 1)

    def combine_chunk(c, acc):
        r0 = pl.multiple_of(c * SORT_CHUNK, SORT_CHUNK)
        parts = []
        for part in range(SORT_CHUNK // lanes):
            col = lane_id + (r0 + part * lanes)
            w = jnp.zeros((tm, lanes), F32)
            for k in range(TOP_K):
                w = jnp.where(col == pos_b[k], wt_b[k], w)
            parts.append(w)
        w16 = jnp.concatenate(parts, axis=1).astype(BF16)
        ea, eb = _unpack_pair(ybuf[slot, pl.ds(r0, SORT_CHUNK), :])
        return acc[0] + _dot(w16, ea), acc[1] + _dot(w16, eb)

    zero = jnp.zeros((tm, PACKED), F32)
    ya, yb = lax.fori_loop(0, (rows + SORT_CHUNK - 1) // SORT_CHUNK, combine_chunk, (zero, zero))
    y = y + jnp.concatenate([ya, yb], axis=1)
    o_ref[...] = _ln(alpha * x1_ref[...] + _rows(g2_ref) * y) * lng_ref[...] + lnb_ref[...]


def _final(tab, yb, h, x1, pos, wt, gate2, ws13, ws2, ln_g, ln_b, *, tm, row0, rows_per_seq, per_row, alpha):
    r = x1.shape[0]
    off = row0 // tm
    row = lambda w: pl.BlockSpec((tm, w), lambda i: (i, 0))
    row_off = lambda w: pl.BlockSpec((tm, w), lambda i: (i + off, 0))
    full = lambda a, b: pl.BlockSpec((a, b), lambda i: (0, 0))
    any_spec = pl.BlockSpec(memory_space=pl.ANY)
    return pl.pallas_call(
        functools.partial(_final_kernel, alpha=alpha, tm=tm, off=off),
        out_shape=jax.ShapeDtypeStruct((r, D_MODEL), F32),
        grid=(r // tm,),
        in_specs=[any_spec, any_spec, row(PACKED), row(D_MODEL), row_off(TOP_K), row_off(TOP_K),
                  _mod_spec(per_row, tm, rows_per_seq),
                  full(D_MODEL, 2 * D_EXPERT), full(D_EXPERT, D_MODEL), full(1, D_MODEL), full(1, D_MODEL)],
        out_specs=row(D_MODEL),
        scratch_shapes=[pltpu.SMEM((2, 3 * N_EXPERTS), jnp.int32), pltpu.SMEM((2,), jnp.int32),
                        pltpu.VMEM((2, _sorted_rows(tm), PACKED), jnp.uint32),
                        pltpu.SemaphoreType.DMA((2,)), pltpu.SemaphoreType.DMA((2,))],
        compiler_params=_cparams(("arbitrary",)),
        name="final",
    )(tab, yb, h, x1, pos, wt, gate2, ws13, ws2, ln_g.reshape(1, -1), ln_b.reshape(1, -1))


def _pick_tile(n, pref):
    t = min(pref, n)
    while n % t:
        t //= 2
    return t


def kernel(x_prompt, x_sample, cache_k_diff, cache_v_diff, cache_k_band, cache_v_band, c_prompt, c_sample,
           w_ada, b_ada, w_in, b_in, lambda_q1, lambda_k1, lambda_q2, lambda_k2, subln_g, rel_bias, p_a, p_b,
           w_out, ln1_g, ln1_b, w_router, e_bias, w1, w3, w2, ws1, ws3, ws2, ln2_g, ln2_b):
    depth = w_in.shape[0]
    alpha = (2 * depth) ** 0.25
    bp, sp, d = x_prompt.shape
    bs, ss, _ = x_sample.shape
    past = cache_k_diff.shape[2]
    lb = cache_k_band.shape[2]
    tp, ts_ = bp * sp, bs * ss
    t_all = tp + ts_
    slopes = (2.0 ** (-8.0 * np.arange(1, H_A + 1) / H_A)).astype(np.float32)
    cslopes = jnp.asarray(np.float32(LOG2E) * slopes)

    yp = x_prompt.reshape(tp, d)
    ys = x_sample.reshape(ts_, d)
    outs = [[] for _ in range(8)]
    tail_p = min(BAND_CHUNKS * CHUNK, sp)
    tm_p = _pick_tile(math.gcd(sp, tail_p), 256)
    tq = _pick_tile(sp, 256)
    ta = _pick_tile(sp, 512)
    ka_aug = _key_bias_columns(slopes, sp)

    for l in range(depth):
        lam_init = 0.8 - 0.6 * math.exp(-0.3 * l)
        lam4 = jnp.stack([lambda_q1[l], lambda_k1[l], lambda_q2[l], lambda_k2[l]]).astype(F32)
        w_in16 = w_in[l].astype(BF16)
        b_in2 = b_in[l].reshape(1, N_IN)
        pa16, pb16, wo16 = p_a[l].astype(BF16), p_b[l].astype(BF16), w_out[l].astype(BF16)

        mod = _ada(jnp.concatenate([c_prompt, c_sample], axis=0), w_ada[l], b_ada[l])
        mod_p = [m.reshape(bp, 1, d) for m in jnp.split(mod[:bp], 6, axis=-1)]
        mod_s = [jnp.repeat(m, ss, axis=0) for m in jnp.split(mod[bp:], 6, axis=-1)]

        (qa, ka32, va32, ka16, va16, qb, kb16, vb16, sga, sgb, kbt, vbt) = _inproj(
            yp, mod_p[0], mod_p[1], w_in16, b_in2, tm=tm_p, rows_per_seq=sp, tail_rows=tail_p, per_row=False)
        r3 = lambda a: a.reshape(bp, sp, a.shape[-1])
        oa = _diffattn(r3(qa), r3(ka16), ka_aug, r3(va16), cslopes, lam4, subln_g[l], lam_init, t=ta)
        ob = _bandattn(r3(qb), r3(kb16), r3(vb16), _band_bias_tiles(rel_bias[l], sp, tq), tq=tq)
        x1p, hp = _outproj(oa.reshape(tp, D_A), ob.reshape(tp, D_B), sga, sgb, yp, mod_p[2], mod_p[3], mod_p[4],
                           pa16, pb16, wo16, ln1_g[l], ln1_b[l], tm=_pick_tile(sp, 512), rows_per_seq=sp,
                           per_row=False, alpha=alpha)
        outs[0].append(ka32.reshape(bp, sp, H_A, 2, HEAD_DIM))
        outs[1].append(va32.reshape(bp, sp, H_A, 2 * HEAD_DIM))
        outs[2].append(kbt.reshape(bp, tail_p, H_B, HEAD_DIM))
        outs[3].append(vbt.reshape(bp, tail_p, H_B, HEAD_DIM))

        (qa_s, ka32_s, va32_s, ka16_s, va16_s, qb_s, kb16_s, vb16_s, sga_s, sgb_s, kbt_s, vbt_s) = _inproj(
            ys, mod_s[0], mod_s[1], w_in16, b_in2, tm=ts_, rows_per_seq=ts_, tail_rows=ts_, per_row=True)
        s3 = lambda a: a.reshape(bs, ss, a.shape[-1])
        oa_s, ob_s = _sampattn(
            s3(qa_s), cache_k_diff[l].reshape(bs, past, D_A), cache_v_diff[l].reshape(bs, past, D_A),
            s3(ka16_s), s3(va16_s), s3(qb_s), cache_k_band[l].reshape(bs, lb, D_B),
            cache_v_band[l].reshape(bs, lb, D_B), s3(kb16_s), s3(vb16_s), cslopes, lam4, subln_g[l],
            _sample_band_bias(rel_bias[l], ss, past, lb), lam_init)
        x1s, hs = _outproj(oa_s.reshape(ts_, D_A), ob_s.reshape(ts_, D_B), sga_s, sgb_s, ys, mod_s[2], mod_s[3],
                           mod_s[4], pa16, pb16, wo16, ln1_g[l], ln1_b[l], tm=ts_, rows_per_seq=ts_, per_row=True,
                           alpha=alpha)
        outs[4].append(ka32_s.reshape(bs, ss, H_A, 2, HEAD_DIM))
        outs[5].append(va32_s.reshape(bs, ss, H_A, 2 * HEAD_DIM))
        outs[6].append(kbt_s.reshape(bs, ss, H_B, HEAD_DIM))
        outs[7].append(vbt_s.reshape(bs, ss, H_B, HEAD_DIM))

        tm_r = _pick_tile(math.gcd(sp, ts_), 256)
        n_tiles = t_all // tm_r
        wt_t, pos_t, tab_raw, cnt = _router(hp, hs, w_router[l], e_bias[l], tr=tm_r)
        counts = cnt[:, 0]
        padded = (counts + EXPERT_BLOCK - 1) // EXPERT_BLOCK * EXPERT_BLOCK
        pad_end = jnp.cumsum(padded)
        pad_start = pad_end - padded
        max_slots = t_all * TOP_K + n_tiles * N_EXPERTS * (ROW_ALIGN - 1)
        nb = -(-max_slots // EXPERT_BLOCK) + N_EXPERTS
        n_slots = nb * EXPERT_BLOCK
        block_starts = jnp.arange(nb, dtype=jnp.int32) * EXPERT_BLOCK
        block_e = jnp.minimum(jnp.sum(pad_end[None, :] <= block_starts[:, None], axis=1),
                              N_EXPERTS - 1).astype(jnp.int32)
        n_used = (pad_end[-1:] // EXPERT_BLOCK).astype(jnp.int32)
        tab = jnp.stack([tab_raw[:, :, 0] + pad_start[None, :], tab_raw[:, :, 1], tab_raw[:, :, 2]], axis=1)
        tab = tab.reshape(n_tiles, 3 * N_EXPERTS).astype(jnp.int32)
        xs = _dispatch(hp, hs, pos_t, tab, (pad_start + counts).astype(jnp.int32), pad_end.astype(jnp.int32),
                       n_slots, tm=tm_r)
        w13 = jnp.concatenate([w1[l], w3[l]], axis=-1).astype(BF16)
        yb = _experts(xs, w13, w2[l].astype(BF16), block_e, n_used)
        wt, pos = wt_t.T, pos_t.T
        ws13 = jnp.concatenate([ws1[l], ws3[l]], axis=-1).astype(BF16)
        ws2b = ws2[l].astype(BF16)
        yp = _final(tab, yb, hp, x1p, pos, wt, mod_p[5], ws13, ws2b, ln2_g[l], ln2_b[l], tm=tm_r, row0=0,
                    rows_per_seq=sp, per_row=False, alpha=alpha)
        ys = _final(tab, yb, hs, x1s, pos, wt, mod_s[5], ws13, ws2b, ln2_g[l], ln2_b[l], tm=tm_r, row0=tp,
                    rows_per_seq=ts_, per_row=True, alpha=alpha)

    return (yp.reshape(bp, sp, d), ys.reshape(bs, ss, d)) + tuple(jnp.stack(o) for o in outs)
```

```python
import functools
import math

import jax
import jax.numpy as jnp
import numpy as np
from jax import lax
from jax.experimental import pallas as pl
from jax.experimental.pallas import tpu as pltpu

F32 = jnp.float32
BF16 = jnp.bfloat16

D_MODEL = 1024
CHUNK = 64
HEAD_DIM = 64
H_A = 8
H_B = 8
D_A = H_A * 2 * HEAD_DIM
D_B = H_B * HEAD_DIM
BAND_CHUNKS = 8
REL_CLIP = 128
N_IN = 3 * D_A + 3 * D_B + 2 * D_MODEL
N_EXPERTS = 64
TOP_K = 8
N_GROUP = 8
TOPK_GROUP = 4
GROUP_SIZE = N_EXPERTS // N_GROUP
D_EXPERT = 256
ROUTED_SCALE = 2.5
EXPERT_BLOCK = 1024
ROW_ALIGN = 8
SORT_CHUNK = 256
LN_EPS = 1e-5
NEG = -1e30
LOG2E = math.log2(math.e)
Q_SCALE = HEAD_DIM ** -0.5 * LOG2E

VMEM_LIMIT = 56 * 1024 * 1024


def _cparams(sem):
    return pltpu.CompilerParams(dimension_semantics=sem, vmem_limit_bytes=VMEM_LIMIT)


def _ln(x):
    mu = jnp.mean(x, axis=-1, keepdims=True)
    xc = x - mu
    var = jnp.mean(xc * xc, axis=-1, keepdims=True)
    return xc * lax.rsqrt(var + LN_EPS)


def _rows(ref):
    v = ref[...]
    return v.reshape(v.shape[-2], v.shape[-1])


def _split_bf16(x):
    hi = x.astype(BF16)
    lo = (x - hi.astype(F32)).astype(BF16)
    return hi, lo


def _dot(a, b):
    return jnp.dot(a, b, preferred_element_type=F32)


def _dot_nt(a, b):
    return lax.dot_general(a, b, (((1,), (1,)), ((), ())), preferred_element_type=F32)


PACKED = D_MODEL // 2


def _pack_pair(a, b):
    ua = lax.bitcast_convert_type(a.astype(BF16).astype(F32), jnp.uint32)
    ub = lax.bitcast_convert_type(b.astype(BF16).astype(F32), jnp.uint32)
    return ua | (ub >> 16)


def _unpack_pair(u):
    a = lax.bitcast_convert_type(u & jnp.uint32(0xFFFF0000), F32).astype(BF16)
    b = lax.bitcast_convert_type(u << 16, F32).astype(BF16)
    return a, b


def _two_group_specs(tm, n_first):
    return (pl.BlockSpec((tm, PACKED), lambda i, *_: (jnp.minimum(i, n_first - 1), 0)),
            pl.BlockSpec((tm, PACKED), lambda i, *_: (jnp.maximum(i - n_first, 0), 0)))


def _two_group_rows(first_ref, second_ref, n_first):
    return jnp.where(pl.program_id(0) < n_first, first_ref[...], second_ref[...])


def _dot_packed(u, w):
    a, b = _unpack_pair(u)
    return _dot(a, w[:PACKED]) + _dot(b, w[PACKED:])


def _ada_kernel(c_ref, w_ref, b_ref, o_ref):
    c = c_ref[...]
    a = c * jax.nn.sigmoid(c)
    a_hi, a_lo = _split_bf16(a)
    w_hi, w_lo = _split_bf16(w_ref[...])
    o_ref[...] = _dot(a_hi, w_hi) + _dot(a_lo, w_hi) + _dot(a_hi, w_lo) + b_ref[...]


def _ada(c, w_ada, b_ada):
    n, d = c.shape
    nout = w_ada.shape[1]
    tn = 1024
    return pl.pallas_call(
        _ada_kernel,
        out_shape=jax.ShapeDtypeStruct((n, nout), F32),
        grid=(nout // tn,),
        in_specs=[pl.BlockSpec((n, d), lambda j: (0, 0)),
                  pl.BlockSpec((d, tn), lambda j: (0, j)),
                  pl.BlockSpec((1, tn), lambda j: (0, j))],
        out_specs=pl.BlockSpec((n, tn), lambda j: (0, j)),
        compiler_params=_cparams(("parallel",)),
        name="ada",
    )(c, w_ada, b_ada.reshape(1, nout))


_SEG_QA = (0, D_A)
_SEG_KA = (D_A, 2 * D_A)
_SEG_VA = (2 * D_A, 3 * D_A)
_SEG_QB = (3 * D_A, 3 * D_A + D_B)
_SEG_KB = (3 * D_A + D_B, 3 * D_A + 2 * D_B)
_SEG_VB = (3 * D_A + 2 * D_B, 3 * D_A + 3 * D_B)
_SEG_GA = (3 * D_A + 3 * D_B, 3 * D_A + 3 * D_B + D_MODEL)
_SEG_GB = (3 * D_A + 3 * D_B + D_MODEL, N_IN)


def _inproj_kernel(x_ref, sh_ref, sc_ref, w_ref, b_ref,
                   qa_ref, ka32_ref, va32_ref, ka16_ref, va16_ref,
                   qb_ref, kb16_ref, vb16_ref, sga_ref, sgb_ref, kbt_ref, vbt_ref, *, tiles_per_seq, tail_tiles):
    u = (_ln(x_ref[...]) * (1.0 + _rows(sc_ref)) + _rows(sh_ref)).astype(BF16)

    def seg(lo_hi):
        lo, hi = lo_hi
        return _dot(u, w_ref[:, lo:hi]) + b_ref[:, lo:hi]

    qa_ref[...] = (seg(_SEG_QA) * Q_SCALE).astype(BF16)
    ka = seg(_SEG_KA)
    for hj in range(2 * H_A):
        ka32_ref[:, hj // 2, hj % 2, :] = ka[:, hj * HEAD_DIM:(hj + 1) * HEAD_DIM]
    ka16_ref[...] = ka.astype(BF16)
    va = seg(_SEG_VA)
    va32_ref[...] = va
    va16_ref[...] = va.astype(BF16)
    qb_ref[...] = (seg(_SEG_QB) * Q_SCALE).astype(BF16)
    kb = seg(_SEG_KB)
    kb16_ref[...] = kb.astype(BF16)
    vb = seg(_SEG_VB)
    vb16_ref[...] = vb.astype(BF16)
    @pl.when(pl.program_id(0) % tiles_per_seq >= tiles_per_seq - tail_tiles)
    def _():
        for h in range(H_B):
            kbt_ref[:, h, :] = kb[:, h * HEAD_DIM:(h + 1) * HEAD_DIM]
            vbt_ref[:, h, :] = vb[:, h * HEAD_DIM:(h + 1) * HEAD_DIM]
    sga_ref[...] = jax.nn.sigmoid(seg(_SEG_GA)).astype(BF16)
    sgb_ref[...] = jax.nn.sigmoid(seg(_SEG_GB)).astype(BF16)


def _mod_spec(per_row, tm, rows_per_seq):
    if per_row:
        return pl.BlockSpec((tm, D_MODEL), lambda i: (i, 0))
    tiles_per_seq = rows_per_seq // tm
    return pl.BlockSpec((1, 1, D_MODEL), lambda i: (i // tiles_per_seq, 0, 0))


def _inproj(x, shift, scale, w_in16, b_in, *, tm, rows_per_seq, tail_rows, per_row):
    r = x.shape[0]
    nseq = r // rows_per_seq
    n_i = rows_per_seq // tm
    n_t = tail_rows // tm

    def tail_map(i):
        return ((i // n_i) * n_t + jnp.maximum(i % n_i - (n_i - n_t), 0), 0, 0)

    row = lambda w: pl.BlockSpec((tm, w), lambda i: (i, 0))
    out_shape = (
        jax.ShapeDtypeStruct((r, D_A), BF16),
        jax.ShapeDtypeStruct((r, H_A, 2, HEAD_DIM), F32),
        jax.ShapeDtypeStruct((r, D_A), F32),
        jax.ShapeDtypeStruct((r, D_A), BF16),
        jax.ShapeDtypeStruct((r, D_A), BF16),
        jax.ShapeDtypeStruct((r, D_B), BF16),
        jax.ShapeDtypeStruct((r, D_B), BF16),
        jax.ShapeDtypeStruct((r, D_B), BF16),
        jax.ShapeDtypeStruct((r, D_MODEL), BF16),
        jax.ShapeDtypeStruct((r, D_MODEL), BF16),
        jax.ShapeDtypeStruct((nseq * tail_rows, H_B, HEAD_DIM), F32),
        jax.ShapeDtypeStruct((nseq * tail_rows, H_B, HEAD_DIM), F32),
    )
    out_specs = (row(D_A), pl.BlockSpec((tm, H_A, 2, HEAD_DIM), lambda i: (i, 0, 0, 0)), row(D_A), row(D_A), row(D_A),
                 row(D_B), row(D_B), row(D_B), row(D_MODEL), row(D_MODEL),
                 pl.BlockSpec((tm, H_B, HEAD_DIM), tail_map), pl.BlockSpec((tm, H_B, HEAD_DIM), tail_map))
    return pl.pallas_call(
        functools.partial(_inproj_kernel, tiles_per_seq=n_i, tail_tiles=n_t),
        out_shape=out_shape,
        grid=(r // tm,),
        in_specs=[row(D_MODEL), _mod_spec(per_row, tm, rows_per_seq), _mod_spec(per_row, tm, rows_per_seq),
                  pl.BlockSpec((D_MODEL, N_IN), lambda i: (0, 0)),
                  pl.BlockSpec((1, N_IN), lambda i: (0, 0))],
        out_specs=out_specs,
        compiler_params=_cparams(("arbitrary",)),
        name="inproj",
    )(x, shift, scale, w_in16, b_in)


N_AUG = 3


def _lam_value(lam4_ref, lam_init):
    v = lam4_ref[...]
    a = jnp.sum(v[0:1, :] * v[1:2, :], axis=-1, keepdims=True)
    b = jnp.sum(v[2:3, :] * v[3:4, :], axis=-1, keepdims=True)
    return jnp.exp(a) - jnp.exp(b) + lam_init


def _half_masks(shape):
    lane = lax.broadcasted_iota(jnp.int32, shape, 1)
    return lane < HEAD_DIM, lane >= HEAD_DIM


def _stack_halves(q):
    lo, hi = _half_masks(q.shape)
    zero = jnp.zeros_like(q)
    return jnp.concatenate([jnp.where(lo, q, zero), jnp.where(hi, q, zero)], axis=0)


def _softmax_step(s, v, m, l, acc):
    m_new = jnp.maximum(m, jnp.max(s, axis=-1, keepdims=True))
    alpha = jnp.exp2(m - m_new)
    p = jnp.exp2(s - m_new)
    l_new = alpha * l + jnp.sum(p, axis=-1, keepdims=True)
    acc_new = alpha * acc + _dot(p.astype(BF16), v)
    return m_new, l_new, acc_new


def _diff_finish(o0, o1, lam, g, lam_init):
    o = o0 - lam * o1
    o = o * lax.rsqrt(jnp.mean(o * o, axis=-1, keepdims=True) + LN_EPS)
    return o * g * (1.0 - lam_init)


def _key_bias_columns(slopes, s):
    x = (np.float32(LOG2E) * slopes.astype(np.float32))[:, None] * np.arange(s, dtype=np.float32)[None, :]
    cols = np.zeros(x.shape + (2 * HEAD_DIM,), np.float32)
    for c in range(N_AUG):
        cols[:, :, c] = x.astype(BF16).astype(np.float32)
        x = x - cols[:, :, c]
    return jnp.asarray(cols, BF16)


def _own_block_fix(cslopes, t):
    ii = lax.broadcasted_iota(jnp.int32, (t, t), 0)
    jj = lax.broadcasted_iota(jnp.int32, (t, t), 1)
    ahead = jnp.where(jj > ii, 2.0 * (ii - jj).astype(F32), 0.0)
    return jnp.where(((jj // CHUNK) <= (ii // CHUNK))[None], cslopes[:, None, None] * ahead[None], NEG)


def _diffattn_kernel(lam4_ref, g_ref, fix_ref, q_ref, k_ref, ka_ref, v_ref, o_ref, *, t, lam_init):
    lo, hi = _half_masks((t, 2 * HEAD_DIM))
    lane = lax.broadcasted_iota(jnp.int32, (t, 2 * HEAD_DIM), 1)
    ones = (lane < N_AUG).astype(BF16)
    lam = _lam_value(lam4_ref, lam_init)

    def q_tile(qi, _):
        rows = pl.ds(pl.multiple_of(qi * t, t), t)
        q = q_ref[0, rows, :]
        zero = jnp.zeros_like(q)
        q_ext = (jnp.concatenate([jnp.where(lo, q, zero), ones], axis=1),
                 jnp.concatenate([jnp.where(hi, q, zero), ones], axis=1))

        def block(j, carry, extra):
            keys = pl.ds(pl.multiple_of(j * t, t), t)
            k_ext = jnp.concatenate([k_ref[0, keys, :], ka_ref[0, keys, :]], axis=1)
            vj = v_ref[0, keys, :]
            out = []
            for half in range(2):
                s = _dot_nt(q_ext[half], k_ext)
                if extra is not None:
                    s = s + extra
                out.append(_softmax_step(s, vj, *carry[half]))
            return tuple(out)

        init1 = (jnp.full((t, 1), NEG, F32), jnp.zeros((t, 1), F32), jnp.zeros((t, 2 * HEAD_DIM), F32))
        carry = lax.fori_loop(0, qi, lambda j, c: block(j, c, None), (init1, init1))
        (m0, l0, a0), (m1, l1, a1) = block(qi, carry, fix_ref[0])
        o_ref[0, rows, :] = _diff_finish(a0 / l0, a1 / l1, lam, g_ref[...], lam_init).astype(o_ref.dtype)
        return 0

    lax.fori_loop(0, q_ref.shape[1] // t, q_tile, 0)


def _diffattn(qa, ka, ka_aug, va, cslopes, lam4, subln_g, lam_init, *, t):
    b, s, _ = qa.shape
    kern = functools.partial(_diffattn_kernel, t=t, lam_init=lam_init)
    hd2 = 2 * HEAD_DIM
    return pl.pallas_call(
        kern,
        out_shape=jax.ShapeDtypeStruct((b, s, D_A), BF16),
        grid=(b, H_A),
        in_specs=[pl.BlockSpec((4, HEAD_DIM), lambda bi, h: (0, 0)),
                  pl.BlockSpec((1, hd2), lambda bi, h: (0, 0)),
                  pl.BlockSpec((1, t, t), lambda bi, h: (h, 0, 0)),
                  pl.BlockSpec((1, s, hd2), lambda bi, h: (bi, 0, h)),
                  pl.BlockSpec((1, s, hd2), lambda bi, h: (bi, 0, h)),
                  pl.BlockSpec((1, s, hd2), lambda bi, h: (h, 0, 0)),
                  pl.BlockSpec((1, s, hd2), lambda bi, h: (bi, 0, h))],
        out_specs=pl.BlockSpec((1, s, hd2), lambda bi, h: (bi, 0, h)),
        compiler_params=_cparams(("parallel", "parallel")),
        name="diffattn",
    )(lam4, subln_g.reshape(1, hd2), _own_block_fix(cslopes, t), qa, ka, ka_aug, va)


def _toeplitz_bias(rel_table, rows, cols, delta):
    length = rows + cols - 1
    rel = np.clip(delta + rows - 1 - np.arange(length), -REL_CLIP, REL_CLIP) + REL_CLIP
    u = jnp.take(rel_table.astype(F32) * LOG2E, jnp.asarray(rel, jnp.int32), axis=1)
    u = jnp.pad(u, ((0, 0), (0, 1)))
    r = jnp.tile(u, (1, rows))[:, :rows * length].reshape(-1, rows, length)
    return r[:, :, rows - 1:rows - 1 + cols]


def _band_valid(t_pos, s_pos):
    return (s_pos // CHUNK <= t_pos // CHUNK) & (s_pos // CHUNK >= t_pos // CHUNK - BAND_CHUNKS)


def _band_window(s, tq):
    return min(BAND_CHUNKS * CHUNK + tq, s)


def _band_bias_tiles(rel_table, s, tq):
    win = _band_window(s, tq)
    n_var = min(BAND_CHUNKS * CHUNK // tq + 1, s // tq)
    tiles = []
    for var in range(n_var):
        q0 = var * tq
        k0 = max(q0 - BAND_CHUNKS * CHUNK, 0)
        valid = _band_valid(q0 + np.arange(tq)[:, None], k0 + np.arange(win)[None, :])
        tiles.append(jnp.where(jnp.asarray(valid)[None], _toeplitz_bias(rel_table, tq, win, q0 - k0), NEG))
    return jnp.stack(tiles)


def _pair_softmax_out(qm, kw, vw, bias2, t):
    s = (_dot_nt(qm, kw).reshape(2, t, -1) + bias2).reshape(2 * t, -1)
    m = jnp.max(s, axis=-1, keepdims=True)
    p = jnp.exp2(s - m)
    l = jnp.sum(p, axis=-1, keepdims=True)
    o = _dot(p.astype(BF16), vw) / l
    lo, _ = _half_masks((t, 2 * HEAD_DIM))
    return jnp.where(lo, o[:t], o[t:])


def _bandattn_kernel(bias_ref, q_ref, k_ref, v_ref, o_ref, *, tq, win, n_var):
    qi = pl.program_id(2)
    var = jnp.minimum(qi, n_var - 1)
    k0 = pl.multiple_of(jnp.maximum(qi * tq - BAND_CHUNKS * CHUNK, 0), tq)
    qm = _stack_halves(q_ref[0])
    kw = k_ref[0, pl.ds(k0, win), :]
    vw = v_ref[0, pl.ds(k0, win), :]
    o_ref[0] = _pair_softmax_out(qm, kw, vw, bias_ref[var], tq).astype(o_ref.dtype)


def _bandattn(qb, kb, vb, bias_tiles, *, tq):
    b, s, _ = qb.shape
    n_var, _, _, win = bias_tiles.shape
    kern = functools.partial(_bandattn_kernel, tq=tq, win=win, n_var=n_var)
    return pl.pallas_call(
        kern,
        out_shape=jax.ShapeDtypeStruct((b, s, D_B), BF16),
        grid=(H_B // 2, b, s // tq),
        in_specs=[pl.BlockSpec((n_var, 2, tq, win), lambda hp, bi, i: (0, hp, 0, 0)),
                  pl.BlockSpec((1, tq, 2 * HEAD_DIM), lambda hp, bi, i: (bi, i, hp)),
                  pl.BlockSpec((1, s, 2 * HEAD_DIM), lambda hp, bi, i: (bi, 0, hp)),
                  pl.BlockSpec((1, s, 2 * HEAD_DIM), lambda hp, bi, i: (bi, 0, hp))],
        out_specs=pl.BlockSpec((1, tq, 2 * HEAD_DIM), lambda hp, bi, i: (bi, i, hp)),
        compiler_params=_cparams(("parallel", "parallel", "arbitrary")),
        name="bandattn",
    )(bias_tiles, qb, kb, vb)


def _two_part_softmax(s_c, s_n, vc, vn):
    m = jnp.maximum(jnp.max(s_c, axis=-1, keepdims=True), jnp.max(s_n, axis=-1, keepdims=True))
    p_c = jnp.exp2(s_c - m)
    p_n = jnp.exp2(s_n - m)
    l = jnp.sum(p_c, axis=-1, keepdims=True) + jnp.sum(p_n, axis=-1, keepdims=True)
    return (_dot(p_c.astype(BF16), vc) + _dot(p_n.astype(BF16), vn)) / l


def _sampattn_kernel(cs_ref, lam4_ref, g_ref, bbias_ref, qa_ref, ckd_ref, cvd_ref, kan_ref, van_ref,
                     qb_ref, ckb_ref, cvb_ref, kbn_ref, vbn_ref, oa_ref, ob_ref, *, n, past, lam_init):
    lam = _lam_value(lam4_ref, lam_init)
    tq_pos = past + lax.broadcasted_iota(jnp.int32, (n, past), 0)
    ts_pos = lax.broadcasted_iota(jnp.int32, (n, past), 1)
    dist_c = jnp.abs(tq_pos - ts_pos).astype(F32)
    vis_c = (ts_pos // CHUNK) <= (tq_pos // CHUNK)
    ii = lax.broadcasted_iota(jnp.int32, (n, n), 0)
    jj = lax.broadcasted_iota(jnp.int32, (n, n), 1)
    dist_n = jnp.abs(ii - jj).astype(F32)
    vis_n = ((past + jj) // CHUNK) <= ((past + ii) // CHUNK)
    for h in range(H_A):
        sl = slice(h * 2 * HEAD_DIM, (h + 1) * 2 * HEAD_DIM)
        cs = cs_ref[h]
        qm = _stack_halves(qa_ref[0, :, sl])
        bias_c = jnp.where(vis_c, -cs * dist_c, NEG)
        bias_n = jnp.where(vis_n, -cs * dist_n, NEG)
        s_c = (_dot_nt(qm, ckd_ref[0, :, sl].astype(BF16)).reshape(2, n, past) + bias_c[None]).reshape(2 * n, past)
        s_n = (_dot_nt(qm, kan_ref[0, :, sl]).reshape(2, n, n) + bias_n[None]).reshape(2 * n, n)
        o = _two_part_softmax(s_c, s_n, cvd_ref[0, :, sl].astype(BF16), van_ref[0, :, sl])
        oa_ref[0, :, sl] = _diff_finish(o[:n], o[n:], lam, g_ref[...], lam_init).astype(oa_ref.dtype)
    lb = ckb_ref.shape[1]
    for hp in range(H_B // 2):
        sl = slice(hp * 2 * HEAD_DIM, (hp + 1) * 2 * HEAD_DIM)
        qm = _stack_halves(qb_ref[0, :, sl])
        bias = bbias_ref[2 * hp:2 * hp + 2]
        s_c = (_dot_nt(qm, ckb_ref[0, :, sl].astype(BF16)).reshape(2, n, lb) + bias[:, :, :lb]).reshape(2 * n, lb)
        s_n = (_dot_nt(qm, kbn_ref[0, :, sl]).reshape(2, n, n) + bias[:, :, lb:]).reshape(2 * n, n)
        o = _two_part_softmax(s_c, s_n, cvb_ref[0, :, sl].astype(BF16), vbn_ref[0, :, sl])
        lo, _ = _half_masks((n, 2 * HEAD_DIM))
        ob_ref[0, :, sl] = jnp.where(lo, o[:n], o[n:]).astype(ob_ref.dtype)


def _sample_band_bias(rel_table, n, past, lb):
    valid = _band_valid(past + np.arange(n)[:, None], (past - lb) + np.arange(lb + n)[None, :])
    return jnp.where(jnp.asarray(valid)[None], _toeplitz_bias(rel_table, n, lb + n, lb), NEG)


def _sampattn(qa, ckd, cvd, kan, van, qb, ckb, cvb, kbn, vbn, cslopes, lam4, subln_g, bbias, lam_init):
    b, n, _ = qa.shape
    past = ckd.shape[1]
    lb = ckb.shape[1]
    kern = functools.partial(_sampattn_kernel, n=n, past=past, lam_init=lam_init)
    full = lambda shape: pl.BlockSpec(shape, lambda bi: (0,) * len(shape))
    per_b = lambda r, w: pl.BlockSpec((1, r, w), lambda bi: (bi, 0, 0))
    return pl.pallas_call(
        kern,
        out_shape=(jax.ShapeDtypeStruct((b, n, D_A), BF16), jax.ShapeDtypeStruct((b, n, D_B), BF16)),
        grid=(b,),
        in_specs=[pl.BlockSpec(memory_space=pltpu.SMEM), full((4, HEAD_DIM)), full((1, 2 * HEAD_DIM)),
                  full((H_B, n, lb + n)),
                  per_b(n, D_A), per_b(past, D_A), per_b(past, D_A), per_b(n, D_A), per_b(n, D_A),
                  per_b(n, D_B), per_b(lb, D_B), per_b(lb, D_B), per_b(n, D_B), per_b(n, D_B)],
        out_specs=(per_b(n, D_A), per_b(n, D_B)),
        compiler_params=_cparams(("parallel",)),
        name="sampattn",
    )(cslopes, lam4, subln_g.reshape(1, 2 * HEAD_DIM), bbias, qa, ckd, cvd, kan, van, qb, ckb, cvb, kbn, vbn)


def _outproj_kernel(oa_ref, ob_ref, sga_ref, sgb_ref, x_ref, g1_ref, sh2_ref, sc2_ref, pa_ref, pb_ref, wo_ref,
                    lng_ref, lnb_ref, x1_ref, h_ref, *, alpha):
    mix = (sga_ref[...].astype(F32) * _dot(oa_ref[...], pa_ref[...])
           + sgb_ref[...].astype(F32) * _dot(ob_ref[...], pb_ref[...]))
    y = _dot(mix.astype(BF16), wo_ref[...])
    x1 = _ln(alpha * x_ref[...] + _rows(g1_ref) * y) * lng_ref[...] + lnb_ref[...]
    x1_ref[...] = x1
    u2 = _ln(x1) * (1.0 + _rows(sc2_ref)) + _rows(sh2_ref)
    h_ref[...] = _pack_pair(u2[:, :PACKED], u2[:, PACKED:])


def _outproj(oa, ob, sga, sgb, x, gate1, shift2, scale2, pa16, pb16, wo16, ln_g, ln_b, *, tm, rows_per_seq,
             per_row, alpha):
    r = x.shape[0]
    row = lambda w: pl.BlockSpec((tm, w), lambda i: (i, 0))
    full = lambda a, b: pl.BlockSpec((a, b), lambda i: (0, 0))
    mod = _mod_spec(per_row, tm, rows_per_seq)
    return pl.pallas_call(
        functools.partial(_outproj_kernel, alpha=alpha),
        out_shape=(jax.ShapeDtypeStruct((r, D_MODEL), F32), jax.ShapeDtypeStruct((r, PACKED), jnp.uint32)),
        grid=(r // tm,),
        in_specs=[row(D_A), row(D_B), row(D_MODEL), row(D_MODEL), row(D_MODEL), mod, mod, mod,
                  full(D_A, D_MODEL), full(D_B, D_MODEL), full(D_MODEL, D_MODEL),
                  full(1, D_MODEL), full(1, D_MODEL)],
        out_specs=(row(D_MODEL), row(PACKED)),
        compiler_params=_cparams(("parallel",)),
        name="outproj",
    )(oa, ob, sga, sgb, x, gate1, shift2, scale2, pa16, pb16, wo16, ln_g.reshape(1, -1), ln_b.reshape(1, -1))


def _first_index_of_max(x, iota, axis, size):
    m = jnp.max(x, axis=axis, keepdims=True)
    idx = jnp.min(jnp.where(x == m, iota, size), axis=axis, keepdims=True)
    return m, idx


def _router_kernel(hp_ref, hs_ref, wr_hi_ref, wr_lo_ref, eb_ref, wt_ref, pos_ref, tab_ref, cnt_ref, run_ref, *, tr,
                   n_first):
    step = pl.program_id(0)

    @pl.when(step == 0)
    def _():
        run_ref[...] = jnp.zeros_like(run_ref)

    ha, hb = _unpack_pair(_two_group_rows(hp_ref, hs_ref, n_first))
    wr_hi, wr_lo = wr_hi_ref[...], wr_lo_ref[...]
    logits = (_dot_nt(wr_hi[:, :PACKED], ha) + _dot_nt(wr_hi[:, PACKED:], hb)
              + _dot_nt(wr_lo[:, :PACKED], ha) + _dot_nt(wr_lo[:, PACKED:], hb))
    scores = jax.nn.sigmoid(logits)
    biased = scores + eb_ref[...]

    x3 = biased.reshape(N_GROUP, GROUP_SIZE, tr)
    io3 = lax.broadcasted_iota(jnp.int32, x3.shape, 1)
    m1, i1 = _first_index_of_max(x3, io3, 1, GROUP_SIZE)
    m2 = jnp.max(jnp.where(io3 == i1, -jnp.inf, x3), axis=1, keepdims=True)
    grp = (m1 + m2).reshape(N_GROUP, tr)

    iog = lax.broadcasted_iota(jnp.int32, grp.shape, 0)
    gsel = jnp.zeros(grp.shape, jnp.bool_)
    for _ in range(TOPK_GROUP):
        _, gi = _first_index_of_max(grp, iog, 0, N_GROUP)
        hit = iog == gi
        gsel = gsel | hit
        grp = jnp.where(hit, -jnp.inf, grp)
    emask = jnp.broadcast_to(gsel.reshape(N_GROUP, 1, tr), (N_GROUP, GROUP_SIZE, tr)).reshape(N_EXPERTS, tr)
    cand = jnp.where(emask, biased, -jnp.inf)

    ioe = lax.broadcasted_iota(jnp.int32, cand.shape, 0)
    hits = []
    sel = jnp.zeros(cand.shape, jnp.bool_)
    for _ in range(TOP_K):
        _, ei = _first_index_of_max(cand, ioe, 0, N_EXPERTS)
        hit = ioe == ei
        hits.append((ei, hit))
        sel = sel | hit
        cand = jnp.where(hit, -jnp.inf, cand)
    self32 = sel.astype(F32)
    ra = lax.broadcasted_iota(jnp.int32, (tr, tr), 0)
    rb = lax.broadcasted_iota(jnp.int32, (tr, tr), 1)
    upper = (ra < rb).astype(BF16)
    prefix = _dot(self32.astype(BF16), upper)

    count = jnp.sum(self32, axis=1, keepdims=True)
    run_len = jnp.floor((count + (ROW_ALIGN - 1)) * (1.0 / ROW_ALIGN)) * ROW_ALIGN
    ea = lax.broadcasted_iota(jnp.int32, (N_EXPERTS, N_EXPERTS), 0)
    eb = lax.broadcasted_iota(jnp.int32, (N_EXPERTS, N_EXPERTS), 1)
    before = (eb < ea).astype(BF16)
    local_start = _dot(before, jnp.broadcast_to(run_len, (N_EXPERTS, 128)).astype(BF16))[:, :1]
    local_pos = local_start + prefix

    ws = [jnp.sum(jnp.where(hit, scores, 0.0), axis=0, keepdims=True) for _, hit in hits]
    wsum = ws[0]
    for w in ws[1:]:
        wsum = wsum + w
    for k, (_, hit) in enumerate(hits):
        wt_ref[k:k + 1, :] = ws[k] / wsum * ROUTED_SCALE
        pos_ref[k:k + 1, :] = jnp.sum(jnp.where(hit, local_pos, 0.0), axis=0, keepdims=True).astype(jnp.int32)
    run_old = run_ref[...]
    lane = lax.broadcasted_iota(jnp.int32, (N_EXPERTS, 128), 1)
    tab = jnp.where(lane == 0, run_old, jnp.where(lane == 1, local_start, jnp.where(lane == 2, run_len, 0.0)))
    tab_ref[0] = tab.astype(jnp.int32)
    run_new = run_old + run_len
    run_ref[...] = run_new
    cnt_ref[...] = jnp.broadcast_to(run_new, cnt_ref.shape).astype(jnp.int32)


def _router(hp, hs, w_router, e_bias, *, tr):
    t = hp.shape[0] + hs.shape[0]
    n_first = hp.shape[0] // tr
    wr_t = w_router.T
    hi32 = lax.bitcast_convert_type(lax.bitcast_convert_type(wr_t, jnp.uint32) & jnp.uint32(0xFFFF0000), F32)
    wr_hi = hi32.astype(BF16)
    wr_lo = (wr_t - hi32).astype(BF16)
    full = lambda a, b: pl.BlockSpec((a, b), lambda i: (0, 0))
    col = lambda: pl.BlockSpec((TOP_K, tr), lambda i: (0, i))
    return pl.pallas_call(
        functools.partial(_router_kernel, tr=tr, n_first=n_first),
        out_shape=(jax.ShapeDtypeStruct((TOP_K, t), F32),
                   jax.ShapeDtypeStruct((TOP_K, t), jnp.int32),
                   jax.ShapeDtypeStruct((t // tr, N_EXPERTS, 128), jnp.int32),
                   jax.ShapeDtypeStruct((N_EXPERTS, 128), jnp.int32)),
        grid=(t // tr,),
        in_specs=[*_two_group_specs(tr, n_first), full(N_EXPERTS, D_MODEL),
                  full(N_EXPERTS, D_MODEL), full(N_EXPERTS, 1)],
        out_specs=(col(), col(), pl.BlockSpec((1, N_EXPERTS, 128), lambda i: (i, 0, 0)), full(N_EXPERTS, 128)),
        scratch_shapes=[pltpu.VMEM((N_EXPERTS, 1), F32)],
        compiler_params=_cparams(("arbitrary",)),
        name="router",
    )(hp, hs, wr_hi, wr_lo, e_bias.reshape(N_EXPERTS, 1).astype(F32))


TAB_GLOBAL, TAB_LOCAL, TAB_LEN = 0, 1, 2


def _sorted_rows(tm):
    need = tm * TOP_K + N_EXPERTS * (ROW_ALIGN - 1)
    return -(-need // SORT_CHUNK) * SORT_CHUNK


def _run(tab_smem, s, e):
    return tuple(pl.multiple_of(tab_smem[s, r * N_EXPERTS + e], ROW_ALIGN) for r in (TAB_GLOBAL, TAB_LOCAL, TAB_LEN))


def _tile_rows(tab_smem, s):
    _, local, length = _run(tab_smem, s, N_EXPERTS - 1)
    return pl.multiple_of(local + length, ROW_ALIGN)


def _rows_copy(src_ref, src_row, dst_ref, dst_row, rows, sem):
    return pltpu.make_async_copy(src_ref.at[pl.ds(src_row, rows)], dst_ref.at[pl.ds(dst_row, rows)], sem)


def _dispatch_kernel(fill_lo_ref, fill_hi_ref, tab_hbm, pos_ref, hp_ref, hs_ref, xs_hbm, tab_smem, rows_smem,
                     sorted_ref, zero_ref, tsem, csem, zsem, *, tm, n_first):
    i = pl.program_id(0)
    n = pl.num_programs(0)
    slot = i % 2

    def tab_copy(tile, s):
        return pltpu.make_async_copy(tab_hbm.at[tile], tab_smem.at[s], tsem.at[s])

    def wait_runs(s):
        rows = rows_smem[s]

        @pl.when(rows > 0)
        def _():
            _rows_copy(sorted_ref.at[s], 0, xs_hbm, 0, pl.multiple_of(rows, ROW_ALIGN), csem.at[s]).wait()

    @pl.when(i == 0)
    def _():
        tab_copy(0, 0).start()
        zero_ref[...] = jnp.zeros_like(zero_ref)
        blk = zero_ref.shape[0]
        for e in range(N_EXPERTS):
            lo = pl.multiple_of(fill_lo_ref[e], ROW_ALIGN)
            rows = pl.multiple_of(fill_hi_ref[e] - lo, ROW_ALIGN)

            @pl.when(rows > 0)
            def _():
                cp = _rows_copy(zero_ref, 0, xs_hbm, lo, rows, zsem)
                cp.start()
                cp.wait()

        def tail(b, c):
            cp = _rows_copy(zero_ref, 0, xs_hbm, pl.multiple_of(b * blk, blk), blk, zsem)
            cp.start()
            cp.wait()
            return c

        lax.fori_loop(fill_hi_ref[N_EXPERTS - 1] // blk, xs_hbm.shape[0] // blk, tail, 0)

    tab_copy(i, slot).wait()

    @pl.when(i + 1 < n)
    def _():
        tab_copy(i + 1, 1 - slot).start()

    ha, hb = _unpack_pair(_two_group_rows(hp_ref, hs_ref, n_first))
    pos = pos_ref[...]
    rows = _tile_rows(tab_smem, slot)

    def sort_chunk(c, carry):
        r0 = pl.multiple_of(c * SORT_CHUNK, SORT_CHUNK)
        row = r0 + lax.broadcasted_iota(jnp.int32, (SORT_CHUNK, tm), 0)
        pick = jnp.zeros((SORT_CHUNK, tm), F32)
        for k in range(TOP_K):
            pick = jnp.where(row == pos[k:k + 1, :], 1.0, pick)
        pick = pick.astype(BF16)
        sorted_ref[slot, pl.ds(r0, SORT_CHUNK), :] = _pack_pair(_dot(pick, ha), _dot(pick, hb))
        return carry

    lax.fori_loop(0, (rows + SORT_CHUNK - 1) // SORT_CHUNK, sort_chunk, 0)

    for e in range(N_EXPERTS):
        dst, src, length = _run(tab_smem, slot, e)

        @pl.when(length > 0)
        def _():
            _rows_copy(sorted_ref.at[slot], src, xs_hbm, dst, length, csem.at[slot]).start()

    rows_smem[slot] = rows

    @pl.when(i > 0)
    def _():
        wait_runs(1 - slot)

    @pl.when(i == n - 1)
    def _():
        wait_runs(slot)


def _dispatch(hp, hs, pos_t, tab, fill_lo, fill_hi, n_slots, *, tm):
    t = hp.shape[0] + hs.shape[0]
    n_first = hp.shape[0] // tm
    return pl.pallas_call(
        functools.partial(_dispatch_kernel, tm=tm, n_first=n_first),
        out_shape=jax.ShapeDtypeStruct((n_slots, PACKED), jnp.uint32),
        grid_spec=pltpu.PrefetchScalarGridSpec(
            num_scalar_prefetch=2,
            grid=(t // tm,),
            in_specs=[pl.BlockSpec(memory_space=pl.ANY),
                      pl.BlockSpec((TOP_K, tm), lambda i, lo, hi: (0, i)),
                      *_two_group_specs(tm, n_first)],
            out_specs=pl.BlockSpec(memory_space=pl.ANY),
            scratch_shapes=[pltpu.SMEM((2, 3 * N_EXPERTS), jnp.int32), pltpu.SMEM((2,), jnp.int32),
                            pltpu.VMEM((2, _sorted_rows(tm), PACKED), jnp.uint32),
                            pltpu.VMEM((EXPERT_BLOCK, PACKED), jnp.uint32),
                            pltpu.SemaphoreType.DMA((2,)), pltpu.SemaphoreType.DMA((2,)),
                            pltpu.SemaphoreType.DMA(())],
        ),
        compiler_params=_cparams(("arbitrary",)),
        name="dispatch",
    )(fill_lo, fill_hi, tab, pos_t, hp, hs)


def _experts_kernel(be_ref, nu_ref, x_ref, w13_ref, w2_ref, y_ref):
    @pl.when(pl.program_id(0) < nu_ref[0])
    def _():
        a = _dot_packed(x_ref[...], w13_ref[0])
        hid = (a[:, :D_EXPERT] * jax.nn.sigmoid(a[:, :D_EXPERT])) * a[:, D_EXPERT:]
        y = _dot(hid.astype(BF16), w2_ref[0])
        y_ref[...] = _pack_pair(y[:, :PACKED], y[:, PACKED:])

    @pl.when(pl.program_id(0) >= nu_ref[0])
    def _():
        y_ref[...] = jnp.zeros_like(y_ref)


def _experts(xs, w13, w2, block_e, n_used):
    n_slots = xs.shape[0]
    nb = n_slots // EXPERT_BLOCK
    last = lambda i, nu: jnp.minimum(i, nu[0] - 1)
    return pl.pallas_call(
        _experts_kernel,
        out_shape=jax.ShapeDtypeStruct((n_slots, PACKED), jnp.uint32),
        grid_spec=pltpu.PrefetchScalarGridSpec(
            num_scalar_prefetch=2,
            grid=(nb,),
            in_specs=[pl.BlockSpec((EXPERT_BLOCK, PACKED), lambda i, be, nu: (last(i, nu), 0)),
                      pl.BlockSpec((1, D_MODEL, 2 * D_EXPERT), lambda i, be, nu: (be[last(i, nu)], 0, 0)),
                      pl.BlockSpec((1, D_EXPERT, D_MODEL), lambda i, be, nu: (be[last(i, nu)], 0, 0))],
            out_specs=pl.BlockSpec((EXPERT_BLOCK, PACKED), lambda i, be, nu: (i, 0)),
        ),
        compiler_params=_cparams(("arbitrary",)),
        name="experts",
    )(block_e, n_used, xs, w13, w2)


def _final_kernel(tab_hbm, yb_hbm, h_ref, x1_ref, pos_ref, wt_ref, g2_ref, ws13_ref, ws2_ref, lng_ref, lnb_ref,
                  o_ref, tab_smem, rows_smem, ybuf, tsem, gsem, *, alpha, tm, off):
    i = pl.program_id(0)
    n = pl.num_programs(0)
    slot = i % 2

    def tab_copy(tile, s):
        return pltpu.make_async_copy(tab_hbm.at[tile + off], tab_smem.at[s], tsem.at[s])

    def fetch_runs(s):
        for e in range(N_EXPERTS):
            src, dst, length = _run(tab_smem, s, e)

            @pl.when(length > 0)
            def _():
                _rows_copy(yb_hbm, src, ybuf.at[s], dst, length, gsem.at[s]).start()

        rows_smem[s] = _tile_rows(tab_smem, s)

    @pl.when(i == 0)
    def _():
        ybuf[...] = jnp.zeros_like(ybuf)
        tab_copy(0, 0).start()
        tab_copy(0, 0).wait()
        fetch_runs(0)

        @pl.when(n > 1)
        def _():
            tab_copy(1, 1).start()

    @pl.when(i + 1 < n)
    def _():
        tab_copy(i + 1, 1 - slot).wait()
        fetch_runs(1 - slot)

        @pl.when(i + 2 < n)
        def _():
            tab_copy(i + 2, slot).start()

    a = _dot_packed(h_ref[...], ws13_ref[...])
    hid = (a[:, :D_EXPERT] * jax.nn.sigmoid(a[:, :D_EXPERT])) * a[:, D_EXPERT:]
    y = _dot(hid.astype(BF16), ws2_ref[...])

    rows = rows_smem[slot]

    @pl.when(rows > 0)
    def _():
        _rows_copy(yb_hbm, 0, ybuf.at[slot], 0, pl.multiple_of(rows, ROW_ALIGN), gsem.at[slot]).wait()

    pos = pos_ref[...]
    wt = wt_ref[...]
    lanes = 128
    pos_b = [jnp.broadcast_to(pos[:, k:k + 1], (tm, lanes)) for k in range(TOP_K)]
    wt_b = [jnp.broadcast_to(wt[:, k:k + 1], (tm, lanes)) for k in range(TOP_K)]
    lane_id = lax.broadcasted_iota(jnp.int32, (tm, lanes), 1)

    def combine_chunk(c, acc):
        r0 = pl.multiple_of(c * SORT_CHUNK, SORT_CHUNK)
        parts = []
        for part in range(SORT_CHUNK // lanes):
            col = lane_id + (r0 + part * lanes)
            w = jnp.zeros((tm, lanes), F32)
            for k in range(TOP_K):
                w = jnp.where(col == pos_b[k], wt_b[k], w)
            parts.append(w)
        w16 = jnp.concatenate(parts, axis=1).astype(BF16)
        ea, eb = _unpack_pair(ybuf[slot, pl.ds(r0, SORT_CHUNK), :])
        return acc[0] + _dot(w16, ea), acc[1] + _dot(w16, eb)

    zero = jnp.zeros((tm, PACKED), F32)
    ya, yb = lax.fori_loop(0, (rows + SORT_CHUNK - 1) // SORT_CHUNK, combine_chunk, (zero, zero))
    y = y + jnp.concatenate([ya, yb], axis=1)
    o_ref[...] = _ln(alpha * x1_ref[...] + _rows(g2_ref) * y) * lng_ref[...] + lnb_ref[...]


def _final(tab, yb, h, x1, pos, wt, gate2, ws13, ws2, ln_g, ln_b, *, tm, row0, rows_per_seq, per_row, alpha):
    r = x1.shape[0]
    off = row0 // tm
    row = lambda w: pl.BlockSpec((tm, w), lambda i: (i, 0))
    row_off = lambda w: pl.BlockSpec((tm, w), lambda i: (i + off, 0))
    full = lambda a, b: pl.BlockSpec((a, b), lambda i: (0, 0))
    any_spec = pl.BlockSpec(memory_space=pl.ANY)
    return pl.pallas_call(
        functools.partial(_final_kernel, alpha=alpha, tm=tm, off=off),
        out_shape=jax.ShapeDtypeStruct((r, D_MODEL), F32),
        grid=(r // tm,),
        in_specs=[any_spec, any_spec, row(PACKED), row(D_MODEL), row_off(TOP_K), row_off(TOP_K),
                  _mod_spec(per_row, tm, rows_per_seq),
                  full(D_MODEL, 2 * D_EXPERT), full(D_EXPERT, D_MODEL), full(1, D_MODEL), full(1, D_MODEL)],
        out_specs=row(D_MODEL),
        scratch_shapes=[pltpu.SMEM((2, 3 * N_EXPERTS), jnp.int32), pltpu.SMEM((2,), jnp.int32),
                        pltpu.VMEM((2, _sorted_rows(tm), PACKED), jnp.uint32),
                        pltpu.SemaphoreType.DMA((2,)), pltpu.SemaphoreType.DMA((2,))],
        compiler_params=_cparams(("arbitrary",)),
        name="final",
    )(tab, yb, h, x1, pos, wt, gate2, ws13, ws2, ln_g.reshape(1, -1), ln_b.reshape(1, -1))


def _pick_tile(n, pref):
    t = min(pref, n)
    while n % t:
        t //= 2
    return t


def kernel(x_prompt, x_sample, cache_k_diff, cache_v_diff, cache_k_band, cache_v_band, c_prompt, c_sample,
           w_ada, b_ada, w_in, b_in, lambda_q1, lambda_k1, lambda_q2, lambda_k2, subln_g, rel_bias, p_a, p_b,
           w_out, ln1_g, ln1_b, w_router, e_bias, w1, w3, w2, ws1, ws3, ws2, ln2_g, ln2_b):
    depth = w_in.shape[0]
    alpha = (2 * depth) ** 0.25
    bp, sp, d = x_prompt.shape
    bs, ss, _ = x_sample.shape
    past = cache_k_diff.shape[2]
    lb = cache_k_band.shape[2]
    tp, ts_ = bp * sp, bs * ss
    t_all = tp + ts_
    slopes = (2.0 ** (-8.0 * np.arange(1, H_A + 1) / H_A)).astype(np.float32)
    cslopes = jnp.asarray(np.float32(LOG2E) * slopes)

    yp = x_prompt.reshape(tp, d)
    ys = x_sample.reshape(ts_, d)
    outs = [[] for _ in range(8)]
    tail_p = min(BAND_CHUNKS * CHUNK, sp)
    tm_p = _pick_tile(math.gcd(sp, tail_p), 256)
    tq = _pick_tile(sp, 256)
    ta = _pick_tile(sp, 512)
    ka_aug = _key_bias_columns(slopes, sp)

    for l in range(depth):
        lam_init = 0.8 - 0.6 * math.exp(-0.3 * l)
        lam4 = jnp.stack([lambda_q1[l], lambda_k1[l], lambda_q2[l], lambda_k2[l]]).astype(F32)
        w_in16 = w_in[l].astype(BF16)
        b_in2 = b_in[l].reshape(1, N_IN)
        pa16, pb16, wo16 = p_a[l].astype(BF16), p_b[l].astype(BF16), w_out[l].astype(BF16)

        mod = _ada(jnp.concatenate([c_prompt, c_sample], axis=0), w_ada[l], b_ada[l])
        mod_p = [m.reshape(bp, 1, d) for m in jnp.split(mod[:bp], 6, axis=-1)]
        mod_s = [jnp.repeat(m, ss, axis=0) for m in jnp.split(mod[bp:], 6, axis=-1)]

        (qa, ka32, va32, ka16, va16, qb, kb16, vb16, sga, sgb, kbt, vbt) = _inproj(
            yp, mod_p[0], mod_p[1], w_in16, b_in2, tm=tm_p, rows_per_seq=sp, tail_rows=tail_p, per_row=False)
        r3 = lambda a: a.reshape(bp, sp, a.shape[-1])
        oa = _diffattn(r3(qa), r3(ka16), ka_aug, r3(va16), cslopes, lam4, subln_g[l], lam_init, t=ta)
        ob = _bandattn(r3(qb), r3(kb16), r3(vb16), _band_bias_tiles(rel_bias[l], sp, tq), tq=tq)
        x1p, hp = _outproj(oa.reshape(tp, D_A), ob.reshape(tp, D_B), sga, sgb, yp, mod_p[2], mod_p[3], mod_p[4],
                           pa16, pb16, wo16, ln1_g[l], ln1_b[l], tm=_pick_tile(sp, 512), rows_per_seq=sp,
                           per_row=False, alpha=alpha)
        outs[0].append(ka32.reshape(bp, sp, H_A, 2, HEAD_DIM))
        outs[1].append(va32.reshape(bp, sp, H_A, 2 * HEAD_DIM))
        outs[2].append(kbt.reshape(bp, tail_p, H_B, HEAD_DIM))
        outs[3].append(vbt.reshape(bp, tail_p, H_B, HEAD_DIM))

        (qa_s, ka32_s, va32_s, ka16_s, va16_s, qb_s, kb16_s, vb16_s, sga_s, sgb_s, kbt_s, vbt_s) = _inproj(
            ys, mod_s[0], mod_s[1], w_in16, b_in2, tm=ts_, rows_per_seq=ts_, tail_rows=ts_, per_row=True)
        s3 = lambda a: a.reshape(bs, ss, a.shape[-1])
        oa_s, ob_s = _sampattn(
            s3(qa_s), cache_k_diff[l].reshape(bs, past, D_A), cache_v_diff[l].reshape(bs, past, D_A),
            s3(ka16_s), s3(va16_s), s3(qb_s), cache_k_band[l].reshape(bs, lb, D_B),
            cache_v_band[l].reshape(bs, lb, D_B), s3(kb16_s), s3(vb16_s), cslopes, lam4, subln_g[l],
            _sample_band_bias(rel_bias[l], ss, past, lb), lam_init)
        x1s, hs = _outproj(oa_s.reshape(ts_, D_A), ob_s.reshape(ts_, D_B), sga_s, sgb_s, ys, mod_s[2], mod_s[3],
                           mod_s[4], pa16, pb16, wo16, ln1_g[l], ln1_b[l], tm=ts_, rows_per_seq=ts_, per_row=True,
                           alpha=alpha)
        outs[4].append(ka32_s.reshape(bs, ss, H_A, 2, HEAD_DIM))
        outs[5].append(va32_s.reshape(bs, ss, H_A, 2 * HEAD_DIM))
        outs[6].append(kbt_s.reshape(bs, ss, H_B, HEAD_DIM))
        outs[7].append(vbt_s.reshape(bs, ss, H_B, HEAD_DIM))

        tm_r = _pick_tile(math.gcd(sp, ts_), 256)
        n_tiles = t_all // tm_r
        wt_t, pos_t, tab_raw, cnt = _router(hp, hs, w_router[l], e_bias[l], tr=tm_r)
        counts = cnt[:, 0]
        padded = (counts + EXPERT_BLOCK - 1) // EXPERT_BLOCK * EXPERT_BLOCK
        pad_end = jnp.cumsum(padded)
        pad_start = pad_end - padded
        max_slots = t_all * TOP_K + n_tiles * N_EXPERTS * (ROW_ALIGN - 1)
        nb = -(-max_slots // EXPERT_BLOCK) + N_EXPERTS
        n_slots = nb * EXPERT_BLOCK
        block_starts = jnp.arange(nb, dtype=jnp.int32) * EXPERT_BLOCK
        block_e = jnp.minimum(jnp.sum(pad_end[None, :] <= block_starts[:, None], axis=1),
                              N_EXPERTS - 1).astype(jnp.int32)
        n_used = (pad_end[-1:] // EXPERT_BLOCK).astype(jnp.int32)
        tab = jnp.stack([tab_raw[:, :, 0] + pad_start[None, :], tab_raw[:, :, 1], tab_raw[:, :, 2]], axis=1)
        tab = tab.reshape(n_tiles, 3 * N_EXPERTS).astype(jnp.int32)
        xs = _dispatch(hp, hs, pos_t, tab, (pad_start + counts).astype(jnp.int32), pad_end.astype(jnp.int32),
                       n_slots, tm=tm_r)
        w13 = jnp.concatenate([w1[l], w3[l]], axis=-1).astype(BF16)
        yb = _experts(xs, w13, w2[l].astype(BF16), block_e, n_used)
        wt, pos = wt_t.T, pos_t.T
        ws13 = jnp.concatenate([ws1[l], ws3[l]], axis=-1).astype(BF16)
        ws2b = ws2[l].astype(BF16)
        yp = _final(tab, yb, hp, x1p, pos, wt, mod_p[5], ws13, ws2b, ln2_g[l], ln2_b[l], tm=tm_r, row0=0,
                    rows_per_seq=sp, per_row=False, alpha=alpha)
        ys = _final(tab, yb, hs, x1s, pos, wt, mod_s[5], ws13, ws2b, ln2_g[l], ln2_b[l], tm=tm_r, row0=tp,
                    rows_per_seq=ts_, per_row=True, alpha=alpha)

    return (yp.reshape(bp, sp, d), ys.reshape(bs, ss, d)) + tuple(jnp.stack(o) for o in outs)
```

```python
import functools
import math

import jax
import jax.numpy as jnp
import numpy as np
from jax import lax
from jax.experimental import pallas as pl
from jax.experimental.pallas import tpu as pltpu

F32 = jnp.float32
BF16 = jnp.bfloat16

D_MODEL = 1024
CHUNK = 64
HEAD_DIM = 64
H_A = 8
H_B = 8
D_A = H_A * 2 * HEAD_DIM
D_B = H_B * HEAD_DIM
BAND_CHUNKS = 8
REL_CLIP = 128
N_IN = 3 * D_A + 3 * D_B + 2 * D_MODEL
N_EXPERTS = 64
TOP_K = 8
N_GROUP = 8
TOPK_GROUP = 4
GROUP_SIZE = N_EXPERTS // N_GROUP
D_EXPERT = 256
ROUTED_SCALE = 2.5
EXPERT_BLOCK = 1024
ROW_ALIGN = 8
SORT_CHUNK = 256
LN_EPS = 1e-5
NEG = -1e30
LOG2E = math.log2(math.e)
Q_SCALE = HEAD_DIM ** -0.5 * LOG2E

VMEM_LIMIT = 56 * 1024 * 1024


def _cparams(sem):
    return pltpu.CompilerParams(dimension_semantics=sem, vmem_limit_bytes=VMEM_LIMIT)


def _ln(x):
    mu = jnp.mean(x, axis=-1, keepdims=True)
    xc = x - mu
    var = jnp.mean(xc * xc, axis=-1, keepdims=True)
    return xc * lax.rsqrt(var + LN_EPS)


def _rows(ref):
    v = ref[...]
    return v.reshape(v.shape[-2], v.shape[-1])


def _split_bf16(x):
    hi = x.astype(BF16)
    lo = (x - hi.astype(F32)).astype(BF16)
    return hi, lo


def _dot(a, b):
    return jnp.dot(a, b, preferred_element_type=F32)


def _dot_nt(a, b):
    return lax.dot_general(a, b, (((1,), (1,)), ((), ())), preferred_element_type=F32)


PACKED = D_MODEL // 2


def _pack_pair(a, b):
    ua = lax.bitcast_convert_type(a.astype(BF16).astype(F32), jnp.uint32)
    ub = lax.bitcast_convert_type(b.astype(BF16).astype(F32), jnp.uint32)
    return ua | (ub >> 16)


def _unpack_pair(u):
    a = lax.bitcast_convert_type(u & jnp.uint32(0xFFFF0000), F32).astype(BF16)
    b = lax.bitcast_convert_type(u << 16, F32).astype(BF16)
    return a, b


def _two_group_specs(tm, n_first):
    return (pl.BlockSpec((tm, PACKED), lambda i, *_: (jnp.minimum(i, n_first - 1), 0)),
            pl.BlockSpec((tm, PACKED), lambda i, *_: (jnp.maximum(i - n_first, 0), 0)))


def _two_group_rows(first_ref, second_ref, n_first):
    return jnp.where(pl.program_id(0) < n_first, first_ref[...], second_ref[...])


def _dot_packed(u, w):
    a, b = _unpack_pair(u)
    return _dot(a, w[:PACKED]) + _dot(b, w[PACKED:])


def _ada_kernel(c_ref, w_ref, b_ref, o_ref):
    c = c_ref[...]
    a = c * jax.nn.sigmoid(c)
    a_hi, a_lo = _split_bf16(a)
    w_hi, w_lo = _split_bf16(w_ref[...])
    o_ref[...] = _dot(a_hi, w_hi) + _dot(a_lo, w_hi) + _dot(a_hi, w_lo) + b_ref[...]


def _ada(c, w_ada, b_ada):
    n, d = c.shape
    nout = w_ada.shape[1]
    tn = 1024
    return pl.pallas_call(
        _ada_kernel,
        out_shape=jax.ShapeDtypeStruct((n, nout), F32),
        grid=(nout // tn,),
        in_specs=[pl.BlockSpec((n, d), lambda j: (0, 0)),
                  pl.BlockSpec((d, tn), lambda j: (0, j)),
                  pl.BlockSpec((1, tn), lambda j: (0, j))],
        out_specs=pl.BlockSpec((n, tn), lambda j: (0, j)),
        compiler_params=_cparams(("parallel",)),
        name="ada",
    )(c, w_ada, b_ada.reshape(1, nout))


_SEG_QA = (0, D_A)
_SEG_KA = (D_A, 2 * D_A)
_SEG_VA = (2 * D_A, 3 * D_A)
_SEG_QB = (3 * D_A, 3 * D_A + D_B)
_SEG_KB = (3 * D_A + D_B, 3 * D_A + 2 * D_B)
_SEG_VB = (3 * D_A + 2 * D_B, 3 * D_A + 3 * D_B)
_SEG_GA = (3 * D_A + 3 * D_B, 3 * D_A + 3 * D_B + D_MODEL)
_SEG_GB = (3 * D_A + 3 * D_B + D_MODEL, N_IN)


def _inproj_kernel(x_ref, sh_ref, sc_ref, w_ref, b_ref,
                   qa_ref, ka32_ref, va32_ref, ka16_ref, va16_ref,
                   qb_ref, kb16_ref, vb16_ref, sga_ref, sgb_ref, kbt_ref, vbt_ref, *, tiles_per_seq, tail_tiles):
    u = (_ln(x_ref[...]) * (1.0 + _rows(sc_ref)) + _rows(sh_ref)).astype(BF16)

    def seg(lo_hi):
        lo, hi = lo_hi
        return _dot(u, w_ref[:, lo:hi]) + b_ref[:, lo:hi]

    qa_ref[...] = (seg(_SEG_QA) * Q_SCALE).astype(BF16)
    ka = seg(_SEG_KA)
    for hj in range(2 * H_A):
        ka32_ref[:, hj // 2, hj % 2, :] = ka[:, hj * HEAD_DIM:(hj + 1) * HEAD_DIM]
    ka16_ref[...] = ka.astype(BF16)
    va = seg(_SEG_VA)
    va32_ref[...] = va
    va16_ref[...] = va.astype(BF16)
    qb_ref[...] = (seg(_SEG_QB) * Q_SCALE).astype(BF16)
    kb = seg(_SEG_KB)
    kb16_ref[...] = kb.astype(BF16)
    vb = seg(_SEG_VB)
    vb16_ref[...] = vb.astype(BF16)
    @pl.when(pl.program_id(0) % tiles_per_seq >= tiles_per_seq - tail_tiles)
    def _():
        for h in range(H_B):
            kbt_ref[:, h, :] = kb[:, h * HEAD_DIM:(h + 1) * HEAD_DIM]
            vbt_ref[:, h, :] = vb[:, h * HEAD_DIM:(h + 1) * HEAD_DIM]
    sga_ref[...] = jax.nn.sigmoid(seg(_SEG_GA)).astype(BF16)
    sgb_ref[...] = jax.nn.sigmoid(seg(_SEG_GB)).astype(BF16)


def _mod_spec(per_row, tm, rows_per_seq):
    if per_row:
        return pl.BlockSpec((tm, D_MODEL), lambda i: (i, 0))
    tiles_per_seq = rows_per_seq // tm
    return pl.BlockSpec((1, 1, D_MODEL), lambda i: (i // tiles_per_seq, 0, 0))


def _inproj(x, shift, scale, w_in16, b_in, *, tm, rows_per_seq, tail_rows, per_row):
    r = x.shape[0]
    nseq = r // rows_per_seq
    n_i = rows_per_seq // tm
    n_t = tail_rows // tm

    def tail_map(i):
        return ((i // n_i) * n_t + jnp.maximum(i % n_i - (n_i - n_t), 0), 0, 0)

    row = lambda w: pl.BlockSpec((tm, w), lambda i: (i, 0))
    out_shape = (
        jax.ShapeDtypeStruct((r, D_A), BF16),
        jax.ShapeDtypeStruct((r, H_A, 2, HEAD_DIM), F32),
        jax.ShapeDtypeStruct((r, D_A), F32),
        jax.ShapeDtypeStruct((r, D_A), BF16),
        jax.ShapeDtypeStruct((r, D_A), BF16),
        jax.ShapeDtypeStruct((r, D_B), BF16),
        jax.ShapeDtypeStruct((r, D_B), BF16),
        jax.ShapeDtypeStruct((r, D_B), BF16),
        jax.ShapeDtypeStruct((r, D_MODEL), BF16),
        jax.ShapeDtypeStruct((r, D_MODEL), BF16),
        jax.ShapeDtypeStruct((nseq * tail_rows, H_B, HEAD_DIM), F32),
        jax.ShapeDtypeStruct((nseq * tail_rows, H_B, HEAD_DIM), F32),
    )
    out_specs = (row(D_A), pl.BlockSpec((tm, H_A, 2, HEAD_DIM), lambda i: (i, 0, 0, 0)), row(D_A), row(D_A), row(D_A),
                 row(D_B), row(D_B), row(D_B), row(D_MODEL), row(D_MODEL),
                 pl.BlockSpec((tm, H_B, HEAD_DIM), tail_map), pl.BlockSpec((tm, H_B, HEAD_DIM), tail_map))
    return pl.pallas_call(
        functools.partial(_inproj_kernel, tiles_per_seq=n_i, tail_tiles=n_t),
        out_shape=out_shape,
        grid=(r // tm,),
        in_specs=[row(D_MODEL), _mod_spec(per_row, tm, rows_per_seq), _mod_spec(per_row, tm, rows_per_seq),
                  pl.BlockSpec((D_MODEL, N_IN), lambda i: (0, 0)),
                  pl.BlockSpec((1, N_IN), lambda i: (0, 0))],
        out_specs=out_specs,
        compiler_params=_cparams(("arbitrary",)),
        name="inproj",
    )(x, shift, scale, w_in16, b_in)


N_AUG = 3


def _lam_value(lam4_ref, lam_init):
    v = lam4_ref[...]
    a = jnp.sum(v[0:1, :] * v[1:2, :], axis=-1, keepdims=True)
    b = jnp.sum(v[2:3, :] * v[3:4, :], axis=-1, keepdims=True)
    return jnp.exp(a) - jnp.exp(b) + lam_init


def _half_masks(shape):
    lane = lax.broadcasted_iota(jnp.int32, shape, 1)
    return lane < HEAD_DIM, lane >= HEAD_DIM


def _stack_halves(q):
    lo, hi = _half_masks(q.shape)
    zero = jnp.zeros_like(q)
    return jnp.concatenate([jnp.where(lo, q, zero), jnp.where(hi, q, zero)], axis=0)


def _softmax_step(s, v, m, l, acc):
    m_new = jnp.maximum(m, jnp.max(s, axis=-1, keepdims=True))
    alpha = jnp.exp2(m - m_new)
    p = jnp.exp2(s - m_new)
    l_new = alpha * l + jnp.sum(p, axis=-1, keepdims=True)
    acc_new = alpha * acc + _dot(p.astype(BF16), v)
    return m_new, l_new, acc_new


def _diff_finish(o0, o1, lam, g, lam_init):
    o = o0 - lam * o1
    o = o * lax.rsqrt(jnp.mean(o * o, axis=-1, keepdims=True) + LN_EPS)
    return o * g * (1.0 - lam_init)


def _key_bias_columns(slopes, s):
    x = (np.float32(LOG2E) * slopes.astype(np.float32))[:, None] * np.arange(s, dtype=np.float32)[None, :]
    cols = np.zeros(x.shape + (2 * HEAD_DIM,), np.float32)
    for c in range(N_AUG):
        cols[:, :, c] = x.astype(BF16).astype(np.float32)
        x = x - cols[:, :, c]
    return jnp.asarray(cols, BF16)


def _own_block_fix(cslopes, t):
    ii = lax.broadcasted_iota(jnp.int32, (t, t), 0)
    jj = lax.broadcasted_iota(jnp.int32, (t, t), 1)
    ahead = jnp.where(jj > ii, 2.0 * (ii - jj).astype(F32), 0.0)
    return jnp.where(((jj // CHUNK) <= (ii // CHUNK))[None], cslopes[:, None, None] * ahead[None], NEG)


def _diffattn_kernel(lam4_ref, g_ref, fix_ref, q_ref, k_ref, ka_ref, v_ref, o_ref, *, t, lam_init):
    lo, hi = _half_masks((t, 2 * HEAD_DIM))
    lane = lax.broadcasted_iota(jnp.int32, (t, 2 * HEAD_DIM), 1)
    ones = (lane < N_AUG).astype(BF16)
    lam = _lam_value(lam4_ref, lam_init)

    def q_tile(qi, _):
        rows = pl.ds(pl.multiple_of(qi * t, t), t)
        q = q_ref[0, rows, :]
        zero = jnp.zeros_like(q)
        q_ext = (jnp.concatenate([jnp.where(lo, q, zero), ones], axis=1),
                 jnp.concatenate([jnp.where(hi, q, zero), ones], axis=1))

        def block(j, carry, extra):
            keys = pl.ds(pl.multiple_of(j * t, t), t)
            k_ext = jnp.concatenate([k_ref[0, keys, :], ka_ref[0, keys, :]], axis=1)
            vj = v_ref[0, keys, :]
            out = []
            for half in range(2):
                s = _dot_nt(q_ext[half], k_ext)
                if extra is not None:
                    s = s + extra
                out.append(_softmax_step(s, vj, *carry[half]))
            return tuple(out)

        init1 = (jnp.full((t, 1), NEG, F32), jnp.zeros((t, 1), F32), jnp.zeros((t, 2 * HEAD_DIM), F32))
        carry = lax.fori_loop(0, qi, lambda j, c: block(j, c, None), (init1, init1))
        (m0, l0, a0), (m1, l1, a1) = block(qi, carry, fix_ref[0])
        o_ref[0, rows, :] = _diff_finish(a0 / l0, a1 / l1, lam, g_ref[...], lam_init).astype(o_ref.dtype)
        return 0

    lax.fori_loop(0, q_ref.shape[1] // t, q_tile, 0)


def _diffattn(qa, ka, ka_aug, va, cslopes, lam4, subln_g, lam_init, *, t):
    b, s, _ = qa.shape
    kern = functools.partial(_diffattn_kernel, t=t, lam_init=lam_init)
    hd2 = 2 * HEAD_DIM
    return pl.pallas_call(
        kern,
        out_shape=jax.ShapeDtypeStruct((b, s, D_A), BF16),
        grid=(b, H_A),
        in_specs=[pl.BlockSpec((4, HEAD_DIM), lambda bi, h: (0, 0)),
                  pl.BlockSpec((1, hd2), lambda bi, h: (0, 0)),
                  pl.BlockSpec((1, t, t), lambda bi, h: (h, 0, 0)),
                  pl.BlockSpec((1, s, hd2), lambda bi, h: (bi, 0, h)),
                  pl.BlockSpec((1, s, hd2), lambda bi, h: (bi, 0, h)),
                  pl.BlockSpec((1, s, hd2), lambda bi, h: (h, 0, 0)),
                  pl.BlockSpec((1, s, hd2), lambda bi, h: (bi, 0, h))],
        out_specs=pl.BlockSpec((1, s, hd2), lambda bi, h: (bi, 0, h)),
        compiler_params=_cparams(("parallel", "parallel")),
        name="diffattn",
    )(lam4, subln_g.reshape(1, hd2), _own_block_fix(cslopes, t), qa, ka, ka_aug, va)


def _toeplitz_bias(rel_table, rows, cols, delta):
    length = rows + cols - 1
    rel = np.clip(delta + rows - 1 - np.arange(length), -REL_CLIP, REL_CLIP) + REL_CLIP
    u = jnp.take(rel_table.astype(F32) * LOG2E, jnp.asarray(rel, jnp.int32), axis=1)
    u = jnp.pad(u, ((0, 0), (0, 1)))
    r = jnp.tile(u, (1, rows))[:, :rows * length].reshape(-1, rows, length)
    return r[:, :, rows - 1:rows - 1 + cols]


def _band_valid(t_pos, s_pos):
    return (s_pos // CHUNK <= t_pos // CHUNK) & (s_pos // CHUNK >= t_pos // CHUNK - BAND_CHUNKS)


def _band_window(s, tq):
    return min(BAND_CHUNKS * CHUNK + tq, s)


def _band_bias_tiles(rel_table, s, tq):
    win = _band_window(s, tq)
    n_var = min(BAND_CHUNKS * CHUNK // tq + 1, s // tq)
    tiles = []
    for var in range(n_var):
        q0 = var * tq
        k0 = max(q0 - BAND_CHUNKS * CHUNK, 0)
        valid = _band_valid(q0 + np.arange(tq)[:, None], k0 + np.arange(win)[None, :])
        tiles.append(jnp.where(jnp.asarray(valid)[None], _toeplitz_bias(rel_table, tq, win, q0 - k0), NEG))
    return jnp.stack(tiles)


def _pair_softmax_out(qm, kw, vw, bias2, t):
    s = (_dot_nt(qm, kw).reshape(2, t, -1) + bias2).reshape(2 * t, -1)
    m = jnp.max(s, axis=-1, keepdims=True)
    p = jnp.exp2(s - m)
    l = jnp.sum(p, axis=-1, keepdims=True)
    o = _dot(p.astype(BF16), vw) / l
    lo, _ = _half_masks((t, 2 * HEAD_DIM))
    return jnp.where(lo, o[:t], o[t:])


def _bandattn_kernel(bias_ref, q_ref, k_ref, v_ref, o_ref, *, tq, win, n_var):
    qi = pl.program_id(2)
    var = jnp.minimum(qi, n_var - 1)
    k0 = pl.multiple_of(jnp.maximum(qi * tq - BAND_CHUNKS * CHUNK, 0), tq)
    qm = _stack_halves(q_ref[0])
    kw = k_ref[0, pl.ds(k0, win), :]
    vw = v_ref[0, pl.ds(k0, win), :]
    o_ref[0] = _pair_softmax_out(qm, kw, vw, bias_ref[var], tq).astype(o_ref.dtype)


def _bandattn(qb, kb, vb, bias_tiles, *, tq):
    b, s, _ = qb.shape
    n_var, _, _, win = bias_tiles.shape
    kern = functools.partial(_bandattn_kernel, tq=tq, win=win, n_var=n_var)
    return pl.pallas_call(
        kern,
        out_shape=jax.ShapeDtypeStruct((b, s, D_B), BF16),
        grid=(H_B // 2, b, s // tq),
        in_specs=[pl.BlockSpec((n_var, 2, tq, win), lambda hp, bi, i: (0, hp, 0, 0)),
                  pl.BlockSpec((1, tq, 2 * HEAD_DIM), lambda hp, bi, i: (bi, i, hp)),
                  pl.BlockSpec((1, s, 2 * HEAD_DIM), lambda hp, bi, i: (bi, 0, hp)),
                  pl.BlockSpec((1, s, 2 * HEAD_DIM), lambda hp, bi, i: (bi, 0, hp))],
        out_specs=pl.BlockSpec((1, tq, 2 * HEAD_DIM), lambda hp, bi, i: (bi, i, hp)),
        compiler_params=_cparams(("parallel", "parallel", "arbitrary")),
        name="bandattn",
    )(bias_tiles, qb, kb, vb)


def _two_part_softmax(s_c, s_n, vc, vn):
    m = jnp.maximum(jnp.max(s_c, axis=-1, keepdims=True), jnp.max(s_n, axis=-1, keepdims=True))
    p_c = jnp.exp2(s_c - m)
    p_n = jnp.exp2(s_n - m)
    l = jnp.sum(p_c, axis=-1, keepdims=True) + jnp.sum(p_n, axis=-1, keepdims=True)
    return (_dot(p_c.astype(BF16), vc) + _dot(p_n.astype(BF16), vn)) / l


def _sampattn_kernel(cs_ref, lam4_ref, g_ref, bbias_ref, qa_ref, ckd_ref, cvd_ref, kan_ref, van_ref,
                     qb_ref, ckb_ref, cvb_ref, kbn_ref, vbn_ref, oa_ref, ob_ref, *, n, past, lam_init):
    lam = _lam_value(lam4_ref, lam_init)
    tq_pos = past + lax.broadcasted_iota(jnp.int32, (n, past), 0)
    ts_pos = lax.broadcasted_iota(jnp.int32, (n, past), 1)
    dist_c = jnp.abs(tq_pos - ts_pos).astype(F32)
    vis_c = (ts_pos // CHUNK) <= (tq_pos // CHUNK)
    ii = lax.broadcasted_iota(jnp.int32, (n, n), 0)
    jj = lax.broadcasted_iota(jnp.int32, (n, n), 1)
    dist_n = jnp.abs(ii - jj).astype(F32)
    vis_n = ((past + jj) // CHUNK) <= ((past + ii) // CHUNK)
    for h in range(H_A):
        sl = slice(h * 2 * HEAD_DIM, (h + 1) * 2 * HEAD_DIM)
        cs = cs_ref[h]
        qm = _stack_halves(qa_ref[0, :, sl])
        bias_c = jnp.where(vis_c, -cs * dist_c, NEG)
        bias_n = jnp.where(vis_n, -cs * dist_n, NEG)
        s_c = (_dot_nt(qm, ckd_ref[0, :, sl].astype(BF16)).reshape(2, n, past) + bias_c[None]).reshape(2 * n, past)
        s_n = (_dot_nt(qm, kan_ref[0, :, sl]).reshape(2, n, n) + bias_n[None]).reshape(2 * n, n)
        vc = cvd_ref[0, pl.ds(h, past, stride=H_A), :].astype(BF16)
        o = _two_part_softmax(s_c, s_n, vc, van_ref[0, :, sl])
        oa_ref[0, :, sl] = _diff_finish(o[:n], o[n:], lam, g_ref[...], lam_init).astype(oa_ref.dtype)
    lb = ckb_ref.shape[1] // H_B
    outs = []
    for h in range(H_B):
        sl = slice(h * HEAD_DIM, (h + 1) * HEAD_DIM)
        q = qb_ref[0, :, sl]
        bias = bbias_ref[h]
        kc = ckb_ref[0, pl.ds(h, lb, stride=H_B), :].astype(BF16)
        vc = cvb_ref[0, pl.ds(h, lb, stride=H_B), :].astype(BF16)
        s_c = _dot_nt(q, kc) + bias[:, :lb]
        s_n = _dot_nt(q, kbn_ref[0, :, sl]) + bias[:, lb:]
        outs.append(_two_part_softmax(s_c, s_n, vc, vbn_ref[0, :, sl]))
    ob_ref[0] = jnp.concatenate(outs, axis=1).astype(ob_ref.dtype)


def _sample_band_bias(rel_table, n, past, lb):
    valid = _band_valid(past + np.arange(n)[:, None], (past - lb) + np.arange(lb + n)[None, :])
    return jnp.where(jnp.asarray(valid)[None], _toeplitz_bias(rel_table, n, lb + n, lb), NEG)


def _sampattn(qa, ckd, cvd, kan, van, qb, ckb, cvb, kbn, vbn, cslopes, lam4, subln_g, bbias, lam_init):
    b, n, _ = qa.shape
    past = ckd.shape[1]
    lb = ckb.shape[1] // H_B
    kern = functools.partial(_sampattn_kernel, n=n, past=past, lam_init=lam_init)
    full = lambda shape: pl.BlockSpec(shape, lambda bi: (0,) * len(shape))
    per_b = lambda r, w: pl.BlockSpec((1, r, w), lambda bi: (bi, 0, 0))
    return pl.pallas_call(
        kern,
        out_shape=(jax.ShapeDtypeStruct((b, n, D_A), BF16), jax.ShapeDtypeStruct((b, n, D_B), BF16)),
        grid=(b,),
        in_specs=[pl.BlockSpec(memory_space=pltpu.SMEM), full((4, HEAD_DIM)), full((1, 2 * HEAD_DIM)),
                  full((H_B, n, lb + n)),
                  per_b(n, D_A), per_b(past, D_A), per_b(past * H_A, 2 * HEAD_DIM), per_b(n, D_A), per_b(n, D_A),
                  per_b(n, D_B), per_b(lb * H_B, HEAD_DIM), per_b(lb * H_B, HEAD_DIM), per_b(n, D_B),
                  per_b(n, D_B)],
        out_specs=(per_b(n, D_A), per_b(n, D_B)),
        compiler_params=_cparams(("parallel",)),
        name="sampattn",
    )(cslopes, lam4, subln_g.reshape(1, 2 * HEAD_DIM), bbias, qa, ckd, cvd, kan, van, qb, ckb, cvb, kbn, vbn)


def _outproj_kernel(oa_ref, ob_ref, sga_ref, sgb_ref, x_ref, g1_ref, sh2_ref, sc2_ref, pa_ref, pb_ref, wo_ref,
                    lng_ref, lnb_ref, x1_ref, h_ref, *, alpha):
    mix = (sga_ref[...].astype(F32) * _dot(oa_ref[...], pa_ref[...])
           + sgb_ref[...].astype(F32) * _dot(ob_ref[...], pb_ref[...]))
    y = _dot(mix.astype(BF16), wo_ref[...])
    x1 = _ln(alpha * x_ref[...] + _rows(g1_ref) * y) * lng_ref[...] + lnb_ref[...]
    x1_ref[...] = x1
    u2 = _ln(x1) * (1.0 + _rows(sc2_ref)) + _rows(sh2_ref)
    h_ref[...] = _pack_pair(u2[:, :PACKED], u2[:, PACKED:])


def _outproj(oa, ob, sga, sgb, x, gate1, shift2, scale2, pa16, pb16, wo16, ln_g, ln_b, *, tm, rows_per_seq,
             per_row, alpha):
    r = x.shape[0]
    row = lambda w: pl.BlockSpec((tm, w), lambda i: (i, 0))
    full = lambda a, b: pl.BlockSpec((a, b), lambda i: (0, 0))
    mod = _mod_spec(per_row, tm, rows_per_seq)
    return pl.pallas_call(
        functools.partial(_outproj_kernel, alpha=alpha),
        out_shape=(jax.ShapeDtypeStruct((r, D_MODEL), F32), jax.ShapeDtypeStruct((r, PACKED), jnp.uint32)),
        grid=(r // tm,),
        in_specs=[row(D_A), row(D_B), row(D_MODEL), row(D_MODEL), row(D_MODEL), mod, mod, mod,
                  full(D_A, D_MODEL), full(D_B, D_MODEL), full(D_MODEL, D_MODEL),
                  full(1, D_MODEL), full(1, D_MODEL)],
        out_specs=(row(D_MODEL), row(PACKED)),
        compiler_params=_cparams(("parallel",)),
        name="outproj",
    )(oa, ob, sga, sgb, x, gate1, shift2, scale2, pa16, pb16, wo16, ln_g.reshape(1, -1), ln_b.reshape(1, -1))


def _first_index_of_max(x, iota, axis, size):
    m = jnp.max(x, axis=axis, keepdims=True)
    idx = jnp.min(jnp.where(x == m, iota, size), axis=axis, keepdims=True)
    return m, idx


def _router_kernel(hp_ref, hs_ref, wr_hi_ref, wr_lo_ref, eb_ref, wt_ref, pos_ref, tab_ref, cnt_ref, run_ref, *, tr,
                   n_first):
    step = pl.program_id(0)

    @pl.when(step == 0)
    def _():
        run_ref[...] = jnp.zeros_like(run_ref)

    ha, hb = _unpack_pair(_two_group_rows(hp_ref, hs_ref, n_first))
    wr_hi, wr_lo = wr_hi_ref[...], wr_lo_ref[...]
    logits = (_dot_nt(wr_hi[:, :PACKED], ha) + _dot_nt(wr_hi[:, PACKED:], hb)
              + _dot_nt(wr_lo[:, :PACKED], ha) + _dot_nt(wr_lo[:, PACKED:], hb))
    scores = jax.nn.sigmoid(logits)
    biased = scores + eb_ref[...]

    x3 = biased.reshape(N_GROUP, GROUP_SIZE, tr)
    io3 = lax.broadcasted_iota(jnp.int32, x3.shape, 1)
    m1, i1 = _first_index_of_max(x3, io3, 1, GROUP_SIZE)
    m2 = jnp.max(jnp.where(io3 == i1, -jnp.inf, x3), axis=1, keepdims=True)
    grp = (m1 + m2).reshape(N_GROUP, tr)

    iog = lax.broadcasted_iota(jnp.int32, grp.shape, 0)
    gsel = jnp.zeros(grp.shape, jnp.bool_)
    for _ in range(TOPK_GROUP):
        _, gi = _first_index_of_max(grp, iog, 0, N_GROUP)
        hit = iog == gi
        gsel = gsel | hit
        grp = jnp.where(hit, -jnp.inf, grp)
    emask = jnp.broadcast_to(gsel.reshape(N_GROUP, 1, tr), (N_GROUP, GROUP_SIZE, tr)).reshape(N_EXPERTS, tr)
    cand = jnp.where(emask, biased, -jnp.inf)

    ioe = lax.broadcasted_iota(jnp.int32, cand.shape, 0)
    hits = []
    sel = jnp.zeros(cand.shape, jnp.bool_)
    for _ in range(TOP_K):
        _, ei = _first_index_of_max(cand, ioe, 0, N_EXPERTS)
        hit = ioe == ei
        hits.append((ei, hit))
        sel = sel | hit
        cand = jnp.where(hit, -jnp.inf, cand)
    self32 = sel.astype(F32)
    ra = lax.broadcasted_iota(jnp.int32, (tr, tr), 0)
    rb = lax.broadcasted_iota(jnp.int32, (tr, tr), 1)
    upper = (ra < rb).astype(BF16)
    prefix = _dot(self32.astype(BF16), upper)

    count = jnp.sum(self32, axis=1, keepdims=True)
    run_len = jnp.floor((count + (ROW_ALIGN - 1)) * (1.0 / ROW_ALIGN)) * ROW_ALIGN
    ea = lax.broadcasted_iota(jnp.int32, (N_EXPERTS, N_EXPERTS), 0)
    eb = lax.broadcasted_iota(jnp.int32, (N_EXPERTS, N_EXPERTS), 1)
    before = (eb < ea).astype(BF16)
    local_start = _dot(before, jnp.broadcast_to(run_len, (N_EXPERTS, 128)).astype(BF16))[:, :1]
    local_pos = local_start + prefix

    ws = [jnp.sum(jnp.where(hit, scores, 0.0), axis=0, keepdims=True) for _, hit in hits]
    wsum = ws[0]
    for w in ws[1:]:
        wsum = wsum + w
    for k, (_, hit) in enumerate(hits):
        wt_ref[k:k + 1, :] = ws[k] / wsum * ROUTED_SCALE
        pos_ref[k:k + 1, :] = jnp.sum(jnp.where(hit, local_pos, 0.0), axis=0, keepdims=True).astype(jnp.int32)
    run_old = run_ref[...]
    lane = lax.broadcasted_iota(jnp.int32, (N_EXPERTS, 128), 1)
    tab = jnp.where(lane == 0, run_old, jnp.where(lane == 1, local_start, jnp.where(lane == 2, run_len, 0.0)))
    tab_ref[0] = tab.astype(jnp.int32)
    run_new = run_old + run_len
    run_ref[...] = run_new
    cnt_ref[...] = jnp.broadcast_to(run_new, cnt_ref.shape).astype(jnp.int32)


def _router(hp, hs, w_router, e_bias, *, tr):
    t = hp.shape[0] + hs.shape[0]
    n_first = hp.shape[0] // tr
    wr_t = w_router.T
    hi32 = lax.bitcast_convert_type(lax.bitcast_convert_type(wr_t, jnp.uint32) & jnp.uint32(0xFFFF0000), F32)
    wr_hi = hi32.astype(BF16)
    wr_lo = (wr_t - hi32).astype(BF16)
    full = lambda a, b: pl.BlockSpec((a, b), lambda i: (0, 0))
    col = lambda: pl.BlockSpec((TOP_K, tr), lambda i: (0, i))
    return pl.pallas_call(
        functools.partial(_router_kernel, tr=tr, n_first=n_first),
        out_shape=(jax.ShapeDtypeStruct((TOP_K, t), F32),
                   jax.ShapeDtypeStruct((TOP_K, t), jnp.int32),
                   jax.ShapeDtypeStruct((t // tr, N_EXPERTS, 128), jnp.int32),
                   jax.ShapeDtypeStruct((N_EXPERTS, 128), jnp.int32)),
        grid=(t // tr,),
        in_specs=[*_two_group_specs(tr, n_first), full(N_EXPERTS, D_MODEL),
                  full(N_EXPERTS, D_MODEL), full(N_EXPERTS, 1)],
        out_specs=(col(), col(), pl.BlockSpec((1, N_EXPERTS, 128), lambda i: (i, 0, 0)), full(N_EXPERTS, 128)),
        scratch_shapes=[pltpu.VMEM((N_EXPERTS, 1), F32)],
        compiler_params=_cparams(("arbitrary",)),
        name="router",
    )(hp, hs, wr_hi, wr_lo, e_bias.reshape(N_EXPERTS, 1).astype(F32))


TAB_GLOBAL, TAB_LOCAL, TAB_LEN = 0, 1, 2


def _sorted_rows(tm):
    need = tm * TOP_K + N_EXPERTS * (ROW_ALIGN - 1)
    return -(-need // SORT_CHUNK) * SORT_CHUNK


def _run(tab_smem, s, e):
    return tuple(pl.multiple_of(tab_smem[s, r * N_EXPERTS + e], ROW_ALIGN) for r in (TAB_GLOBAL, TAB_LOCAL, TAB_LEN))


def _tile_rows(tab_smem, s):
    _, local, length = _run(tab_smem, s, N_EXPERTS - 1)
    return pl.multiple_of(local + length, ROW_ALIGN)


def _rows_copy(src_ref, src_row, dst_ref, dst_row, rows, sem):
    return pltpu.make_async_copy(src_ref.at[pl.ds(src_row, rows)], dst_ref.at[pl.ds(dst_row, rows)], sem)


def _dispatch_kernel(fill_lo_ref, fill_hi_ref, tab_hbm, pos_ref, hp_ref, hs_ref, xs_hbm, tab_smem, rows_smem,
                     sorted_ref, zero_ref, tsem, csem, zsem, *, tm, n_first):
    i = pl.program_id(0)
    n = pl.num_programs(0)
    slot = i % 2

    def tab_copy(tile, s):
        return pltpu.make_async_copy(tab_hbm.at[tile], tab_smem.at[s], tsem.at[s])

    def wait_runs(s):
        rows = rows_smem[s]

        @pl.when(rows > 0)
        def _():
            _rows_copy(sorted_ref.at[s], 0, xs_hbm, 0, pl.multiple_of(rows, ROW_ALIGN), csem.at[s]).wait()

    @pl.when(i == 0)
    def _():
        tab_copy(0, 0).start()
        zero_ref[...] = jnp.zeros_like(zero_ref)
        blk = zero_ref.shape[0]
        for e in range(N_EXPERTS):
            lo = pl.multiple_of(fill_lo_ref[e], ROW_ALIGN)
            rows = pl.multiple_of(fill_hi_ref[e] - lo, ROW_ALIGN)

            @pl.when(rows > 0)
            def _():
                cp = _rows_copy(zero_ref, 0, xs_hbm, lo, rows, zsem)
                cp.start()
                cp.wait()

        def tail(b, c):
            cp = _rows_copy(zero_ref, 0, xs_hbm, pl.multiple_of(b * blk, blk), blk, zsem)
            cp.start()
            cp.wait()
            return c

        lax.fori_loop(fill_hi_ref[N_EXPERTS - 1] // blk, xs_hbm.shape[0] // blk, tail, 0)

    tab_copy(i, slot).wait()

    @pl.when(i + 1 < n)
    def _():
        tab_copy(i + 1, 1 - slot).start()

    ha, hb = _unpack_pair(_two_group_rows(hp_ref, hs_ref, n_first))
    pos = pos_ref[...]
    rows = _tile_rows(tab_smem, slot)

    def sort_chunk(c, carry):
        r0 = pl.multiple_of(c * SORT_CHUNK, SORT_CHUNK)
        row = r0 + lax.broadcasted_iota(jnp.int32, (SORT_CHUNK, tm), 0)
        pick = jnp.zeros((SORT_CHUNK, tm), F32)
        for k in range(TOP_K):
            pick = jnp.where(row == pos[k:k + 1, :], 1.0, pick)
        pick = pick.astype(BF16)
        sorted_ref[slot, pl.ds(r0, SORT_CHUNK), :] = _pack_pair(_dot(pick, ha), _dot(pick, hb))
        return carry

    lax.fori_loop(0, (rows + SORT_CHUNK - 1) // SORT_CHUNK, sort_chunk, 0)

    for e in range(N_EXPERTS):
        dst, src, length = _run(tab_smem, slot, e)

        @pl.when(length > 0)
        def _():
            _rows_copy(sorted_ref.at[slot], src, xs_hbm, dst, length, csem.at[slot]).start()

    rows_smem[slot] = rows

    @pl.when(i > 0)
    def _():
        wait_runs(1 - slot)

    @pl.when(i == n - 1)
    def _():
        wait_runs(slot)


def _dispatch(hp, hs, pos_t, tab, fill_lo, fill_hi, n_slots, *, tm):
    t = hp.shape[0] + hs.shape[0]
    n_first = hp.shape[0] // tm
    return pl.pallas_call(
        functools.partial(_dispatch_kernel, tm=tm, n_first=n_first),
        out_shape=jax.ShapeDtypeStruct((n_slots, PACKED), jnp.uint32),
        grid_spec=pltpu.PrefetchScalarGridSpec(
            num_scalar_prefetch=2,
            grid=(t // tm,),
            in_specs=[pl.BlockSpec(memory_space=pl.ANY),
                      pl.BlockSpec((TOP_K, tm), lambda i, lo, hi: (0, i)),
                      *_two_group_specs(tm, n_first)],
            out_specs=pl.BlockSpec(memory_space=pl.ANY),
            scratch_shapes=[pltpu.SMEM((2, 3 * N_EXPERTS), jnp.int32), pltpu.SMEM((2,), jnp.int32),
                            pltpu.VMEM((2, _sorted_rows(tm), PACKED), jnp.uint32),
                            pltpu.VMEM((EXPERT_BLOCK, PACKED), jnp.uint32),
                            pltpu.SemaphoreType.DMA((2,)), pltpu.SemaphoreType.DMA((2,)),
                            pltpu.SemaphoreType.DMA(())],
        ),
        compiler_params=_cparams(("arbitrary",)),
        name="dispatch",
    )(fill_lo, fill_hi, tab, pos_t, hp, hs)


def _experts_kernel(be_ref, nu_ref, x_ref, w13_ref, w2_ref, y_ref):
    @pl.when(pl.program_id(0) < nu_ref[0])
    def _():
        a = _dot_packed(x_ref[...], w13_ref[0])
        hid = (a[:, :D_EXPERT] * jax.nn.sigmoid(a[:, :D_EXPERT])) * a[:, D_EXPERT:]
        y = _dot(hid.astype(BF16), w2_ref[0])
        y_ref[...] = _pack_pair(y[:, :PACKED], y[:, PACKED:])

    @pl.when(pl.program_id(0) >= nu_ref[0])
    def _():
        y_ref[...] = jnp.zeros_like(y_ref)


def _experts(xs, w13, w2, block_e, n_used):
    n_slots = xs.shape[0]
    nb = n_slots // EXPERT_BLOCK
    last = lambda i, nu: jnp.minimum(i, nu[0] - 1)
    return pl.pallas_call(
        _experts_kernel,
        out_shape=jax.ShapeDtypeStruct((n_slots, PACKED), jnp.uint32),
        grid_spec=pltpu.PrefetchScalarGridSpec(
            num_scalar_prefetch=2,
            grid=(nb,),
            in_specs=[pl.BlockSpec((EXPERT_BLOCK, PACKED), lambda i, be, nu: (last(i, nu), 0)),
                      pl.BlockSpec((1, D_MODEL, 2 * D_EXPERT), lambda i, be, nu: (be[last(i, nu)], 0, 0)),
                      pl.BlockSpec((1, D_EXPERT, D_MODEL), lambda i, be, nu: (be[last(i, nu)], 0, 0))],
            out_specs=pl.BlockSpec((EXPERT_BLOCK, PACKED), lambda i, be, nu: (i, 0)),
        ),
        compiler_params=_cparams(("arbitrary",)),
        name="experts",
    )(block_e, n_used, xs, w13, w2)


def _final_kernel(tab_hbm, yb_hbm, h_ref, x1_ref, pos_ref, wt_ref, g2_ref, ws13_ref, ws2_ref, lng_ref, lnb_ref,
                  o_ref, tab_smem, rows_smem, ybuf, tsem, gsem, *, alpha, tm, off):
    i = pl.program_id(0)
    n = pl.num_programs(0)
    slot = i % 2

    def tab_copy(tile, s):
        return pltpu.make_async_copy(tab_hbm.at[tile + off], tab_smem.at[s], tsem.at[s])

    def fetch_runs(s):
        for e in range(N_EXPERTS):
            src, dst, length = _run(tab_smem, s, e)

            @pl.when(length > 0)
            def _():
                _rows_copy(yb_hbm, src, ybuf.at[s], dst, length, gsem.at[s]).start()

        rows_smem[s] = _tile_rows(tab_smem, s)

    @pl.when(i == 0)
    def _():
        ybuf[...] = jnp.zeros_like(ybuf)
        tab_copy(0, 0).start()
        tab_copy(0, 0).wait()
        fetch_runs(0)

        @pl.when(n > 1)
        def _():
            tab_copy(1, 1).start()

    @pl.when(i + 1 < n)
    def _():
        tab_copy(i + 1, 1 - slot).wait()
        fetch_runs(1 - slot)

        @pl.when(i + 2 < n)
        def _():
            tab_copy(i + 2, slot).start()

    a = _dot_packed(h_ref[...], ws13_ref[...])
    hid = (a[:, :D_EXPERT] * jax.nn.sigmoid(a[:, :D_EXPERT])) * a[:, D_EXPERT:]
    y = _dot(hid.astype(BF16), ws2_ref[...])

    rows = rows_smem[slot]

    @pl.when(rows > 0)
    def _():
        _rows_copy(yb_hbm, 0, ybuf.at[slot], 0, pl.multiple_of(rows, ROW_ALIGN), gsem.at[slot]).wait()

    pos = pos_ref[...]
    wt = wt_ref[...]
    lanes = 128
    pos_b = [jnp.broadcast_to(pos[:, k:k + 1], (tm, lanes)) for k in range(TOP_K)]
    wt_b = [jnp.broadcast_to(wt[:, k:k + 1], (tm, lanes)) for k in range(TOP_K)]
    lane_id = lax.broadcasted_iota(jnp.int32, (tm, lanes), 1)

    def combine_chunk(c, acc):
        r0 = pl.multiple_of(c * SORT_CHUNK, SORT_CHUNK)
        parts = []
        for part in range(SORT_CHUNK // lanes):
            col = lane_id + (r0 + part * lanes)
            w = jnp.zeros((tm, lanes), F32)
            for k in range(TOP_K):
                w = jnp.where(col == pos_b[k], wt_b[k], w)
            parts.append(w)
        w16 = jnp.concatenate(parts, axis=1).astype(BF16)
        ea, eb = _unpack_pair(ybuf[slot, pl.ds(r0, SORT_CHUNK), :])
        return acc[0] + _dot(w16, ea), acc[1] + _dot(w16, eb)

    zero = jnp.zeros((tm, PACKED), F32)
    ya, yb = lax.fori_loop(0, (rows + SORT_CHUNK - 1) // SORT_CHUNK, combine_chunk, (zero, zero))
    y = y + jnp.concatenate([ya, yb], axis=1)
    o_ref[...] = _ln(alpha * x1_ref[...] + _rows(g2_ref) * y) * lng_ref[...] + lnb_ref[...]


def _final(tab, yb, h, x1, pos, wt, gate2, ws13, ws2, ln_g, ln_b, *, tm, row0, rows_per_seq, per_row, alpha):
    r = x1.shape[0]
    off = row0 // tm
    row = lambda w: pl.BlockSpec((tm, w), lambda i: (i, 0))
    row_off = lambda w: pl.BlockSpec((tm, w), lambda i: (i + off, 0))
    full = lambda a, b: pl.BlockSpec((a, b), lambda i: (0, 0))
    any_spec = pl.BlockSpec(memory_space=pl.ANY)
    return pl.pallas_call(
        functools.partial(_final_kernel, alpha=alpha, tm=tm, off=off),
        out_shape=jax.ShapeDtypeStruct((r, D_MODEL), F32),
        grid=(r // tm,),
        in_specs=[any_spec, any_spec, row(PACKED), row(D_MODEL), row_off(TOP_K), row_off(TOP_K),
                  _mod_spec(per_row, tm, rows_per_seq),
                  full(D_MODEL, 2 * D_EXPERT), full(D_EXPERT, D_MODEL), full(1, D_MODEL), full(1, D_MODEL)],
        out_specs=row(D_MODEL),
        scratch_shapes=[pltpu.SMEM((2, 3 * N_EXPERTS), jnp.int32), pltpu.SMEM((2,), jnp.int32),
                        pltpu.VMEM((2, _sorted_rows(tm), PACKED), jnp.uint32),
                        pltpu.SemaphoreType.DMA((2,)), pltpu.SemaphoreType.DMA((2,))],
        compiler_params=_cparams(("arbitrary",)),
        name="final",
    )(tab, yb, h, x1, pos, wt, gate2, ws13, ws2, ln_g.reshape(1, -1), ln_b.reshape(1, -1))


def _pick_tile(n, pref):
    t = min(pref, n)
    while n % t:
        t //= 2
    return t


def kernel(x_prompt, x_sample, cache_k_diff, cache_v_diff, cache_k_band, cache_v_band, c_prompt, c_sample,
           w_ada, b_ada, w_in, b_in, lambda_q1, lambda_k1, lambda_q2, lambda_k2, subln_g, rel_bias, p_a, p_b,
           w_out, ln1_g, ln1_b, w_router, e_bias, w1, w3, w2, ws1, ws3, ws2, ln2_g, ln2_b):
    depth = w_in.shape[0]
    alpha = (2 * depth) ** 0.25
    bp, sp, d = x_prompt.shape
    bs, ss, _ = x_sample.shape
    past = cache_k_diff.shape[2]
    lb = cache_k_band.shape[2]
    tp, ts_ = bp * sp, bs * ss
    t_all = tp + ts_
    slopes = (2.0 ** (-8.0 * np.arange(1, H_A + 1) / H_A)).astype(np.float32)
    cslopes = jnp.asarray(np.float32(LOG2E) * slopes)

    yp = x_prompt.reshape(tp, d)
    ys = x_sample.reshape(ts_, d)
    outs = [[] for _ in range(8)]
    tail_p = min(BAND_CHUNKS * CHUNK, sp)
    tm_p = _pick_tile(math.gcd(sp, tail_p), 256)
    tq = _pick_tile(sp, 256)
    ta = _pick_tile(sp, 512)
    ka_aug = _key_bias_columns(slopes, sp)

    for l in range(depth):
        lam_init = 0.8 - 0.6 * math.exp(-0.3 * l)
        lam4 = jnp.stack([lambda_q1[l], lambda_k1[l], lambda_q2[l], lambda_k2[l]]).astype(F32)
        w_in16 = w_in[l].astype(BF16)
        b_in2 = b_in[l].reshape(1, N_IN)
        pa16, pb16, wo16 = p_a[l].astype(BF16), p_b[l].astype(BF16), w_out[l].astype(BF16)

        mod = _ada(jnp.concatenate([c_prompt, c_sample], axis=0), w_ada[l], b_ada[l])
        mod_p = [m.reshape(bp, 1, d) for m in jnp.split(mod[:bp], 6, axis=-1)]
        mod_s = [jnp.repeat(m, ss, axis=0) for m in jnp.split(mod[bp:], 6, axis=-1)]

        (qa, ka32, va32, ka16, va16, qb, kb16, vb16, sga, sgb, kbt, vbt) = _inproj(
            yp, mod_p[0], mod_p[1], w_in16, b_in2, tm=tm_p, rows_per_seq=sp, tail_rows=tail_p, per_row=False)
        r3 = lambda a: a.reshape(bp, sp, a.shape[-1])
        oa = _diffattn(r3(qa), r3(ka16), ka_aug, r3(va16), cslopes, lam4, subln_g[l], lam_init, t=ta)
        ob = _bandattn(r3(qb), r3(kb16), r3(vb16), _band_bias_tiles(rel_bias[l], sp, tq), tq=tq)
        x1p, hp = _outproj(oa.reshape(tp, D_A), ob.reshape(tp, D_B), sga, sgb, yp, mod_p[2], mod_p[3], mod_p[4],
                           pa16, pb16, wo16, ln1_g[l], ln1_b[l], tm=_pick_tile(sp, 512), rows_per_seq=sp,
                           per_row=False, alpha=alpha)
        outs[0].append(ka32.reshape(bp, sp, H_A, 2, HEAD_DIM))
        outs[1].append(va32.reshape(bp, sp, H_A, 2 * HEAD_DIM))
        outs[2].append(kbt.reshape(bp, tail_p, H_B, HEAD_DIM))
        outs[3].append(vbt.reshape(bp, tail_p, H_B, HEAD_DIM))

        (qa_s, ka32_s, va32_s, ka16_s, va16_s, qb_s, kb16_s, vb16_s, sga_s, sgb_s, kbt_s, vbt_s) = _inproj(
            ys, mod_s[0], mod_s[1], w_in16, b_in2, tm=ts_, rows_per_seq=ts_, tail_rows=ts_, per_row=True)
        s3 = lambda a: a.reshape(bs, ss, a.shape[-1])
        oa_s, ob_s = _sampattn(
            s3(qa_s), cache_k_diff[l].reshape(bs, past, D_A), cache_v_diff[l].reshape(bs, past * H_A, 2 * HEAD_DIM),
            s3(ka16_s), s3(va16_s), s3(qb_s), cache_k_band[l].reshape(bs, lb * H_B, HEAD_DIM),
            cache_v_band[l].reshape(bs, lb * H_B, HEAD_DIM), s3(kb16_s), s3(vb16_s), cslopes, lam4, subln_g[l],
            _sample_band_bias(rel_bias[l], ss, past, lb), lam_init)
        x1s, hs = _outproj(oa_s.reshape(ts_, D_A), ob_s.reshape(ts_, D_B), sga_s, sgb_s, ys, mod_s[2], mod_s[3],
                           mod_s[4], pa16, pb16, wo16, ln1_g[l], ln1_b[l], tm=ts_, rows_per_seq=ts_, per_row=True,
                           alpha=alpha)
        outs[4].append(ka32_s.reshape(bs, ss, H_A, 2, HEAD_DIM))
        outs[5].append(va32_s.reshape(bs, ss, H_A, 2 * HEAD_DIM))
        outs[6].append(kbt_s.reshape(bs, ss, H_B, HEAD_DIM))
        outs[7].append(vbt_s.reshape(bs, ss, H_B, HEAD_DIM))

        tm_r = _pick_tile(math.gcd(sp, ts_), 256)
        n_tiles = t_all // tm_r
        wt_t, pos_t, tab_raw, cnt = _router(hp, hs, w_router[l], e_bias[l], tr=tm_r)
        counts = cnt[:, 0]
        padded = (counts + EXPERT_BLOCK - 1) // EXPERT_BLOCK * EXPERT_BLOCK
        pad_end = jnp.cumsum(padded)
        pad_start = pad_end - padded
        max_slots = t_all * TOP_K + n_tiles * N_EXPERTS * (ROW_ALIGN - 1)
        nb = -(-max_slots // EXPERT_BLOCK) + N_EXPERTS
        n_slots = nb * EXPERT_BLOCK
        block_starts = jnp.arange(nb, dtype=jnp.int32) * EXPERT_BLOCK
        block_e = jnp.minimum(jnp.sum(pad_end[None, :] <= block_starts[:, None], axis=1),
                              N_EXPERTS - 1).astype(jnp.int32)
        n_used = (pad_end[-1:] // EXPERT_BLOCK).astype(jnp.int32)
        tab = jnp.stack([tab_raw[:, :, 0] + pad_start[None, :], tab_raw[:, :, 1], tab_raw[:, :, 2]], axis=1)
        tab = tab.reshape(n_tiles, 3 * N_EXPERTS).astype(jnp.int32)
        xs = _dispatch(hp, hs, pos_t, tab, (pad_start + counts).astype(jnp.int32), pad_end.astype(jnp.int32),
                       n_slots, tm=tm_r)
        w13 = jnp.concatenate([w1[l], w3[l]], axis=-1).astype(BF16)
        yb = _experts(xs, w13, w2[l].astype(BF16), block_e, n_used)
        wt, pos = wt_t.T, pos_t.T
        ws13 = jnp.concatenate([ws1[l], ws3[l]], axis=-1).astype(BF16)
        ws2b = ws2[l].astype(BF16)
        yp = _final(tab, yb, hp, x1p, pos, wt, mod_p[5], ws13, ws2b, ln2_g[l], ln2_b[l], tm=tm_r, row0=0,
                    rows_per_seq=sp, per_row=False, alpha=alpha)
        ys = _final(tab, yb, hs, x1s, pos, wt, mod_s[5], ws13, ws2b, ln2_g[l], ln2_b[l], tm=tm_r, row0=tp,
                    rows_per_seq=ts_, per_row=True, alpha=alpha)

    return (yp.reshape(bp, sp, d), ys.reshape(bs, ss, d)) + tuple(jnp.stack(o) for o in outs)
```

```python
import functools
import math

import jax
import jax.numpy as jnp
import numpy as np
from jax import lax
from jax.experimental import pallas as pl
from jax.experimental.pallas import tpu as pltpu

F32 = jnp.float32
BF16 = jnp.bfloat16

D_MODEL = 1024
CHUNK = 64
HEAD_DIM = 64
H_A = 8
H_B = 8
D_A = H_A * 2 * HEAD_DIM
D_B = H_B * HEAD_DIM
BAND_CHUNKS = 8
REL_CLIP = 128
N_IN = 3 * D_A + 3 * D_B + 2 * D_MODEL
N_EXPERTS = 64
TOP_K = 8
N_GROUP = 8
TOPK_GROUP = 4
GROUP_SIZE = N_EXPERTS // N_GROUP
D_EXPERT = 256
ROUTED_SCALE = 2.5
EXPERT_BLOCK = 1024
ROW_ALIGN = 8
SORT_CHUNK = 512
LN_EPS = 1e-5
NEG = -1e30
LOG2E = math.log2(math.e)
Q_SCALE = HEAD_DIM ** -0.5 * LOG2E

VMEM_LIMIT = 56 * 1024 * 1024


def _cparams(sem):
    return pltpu.CompilerParams(dimension_semantics=sem, vmem_limit_bytes=VMEM_LIMIT)


def _ln(x):
    mu = jnp.mean(x, axis=-1, keepdims=True)
    xc = x - mu
    var = jnp.mean(xc * xc, axis=-1, keepdims=True)
    return xc * lax.rsqrt(var + LN_EPS)


def _rows(ref):
    v = ref[...]
    return v.reshape(v.shape[-2], v.shape[-1])


def _split_bf16(x):
    hi = x.astype(BF16)
    lo = (x - hi.astype(F32)).astype(BF16)
    return hi, lo


def _dot(a, b):
    return jnp.dot(a, b, preferred_element_type=F32)


def _dot_nt(a, b):
    return lax.dot_general(a, b, (((1,), (1,)), ((), ())), preferred_element_type=F32)


PACKED = D_MODEL // 2


def _pack_pair(a, b):
    ua = lax.bitcast_convert_type(a.astype(BF16).astype(F32), jnp.uint32)
    ub = lax.bitcast_convert_type(b.astype(BF16).astype(F32), jnp.uint32)
    return ua | (ub >> 16)


def _unpack_pair(u):
    a = lax.bitcast_convert_type(u & jnp.uint32(0xFFFF0000), F32).astype(BF16)
    b = lax.bitcast_convert_type(u << 16, F32).astype(BF16)
    return a, b


def _two_group_specs(tm, n_first):
    return (pl.BlockSpec((tm, PACKED), lambda i, *_: (jnp.minimum(i, n_first - 1), 0)),
            pl.BlockSpec((tm, PACKED), lambda i, *_: (jnp.maximum(i - n_first, 0), 0)))


def _two_group_rows(first_ref, second_ref, n_first):
    return jnp.where(pl.program_id(0) < n_first, first_ref[...], second_ref[...])


def _dot_packed(u, w):
    a, b = _unpack_pair(u)
    return _dot(a, w[:PACKED]) + _dot(b, w[PACKED:])


def _ada_kernel(c_ref, w_ref, b_ref, o_ref):
    c = c_ref[...]
    a = c * jax.nn.sigmoid(c)
    a_hi, a_lo = _split_bf16(a)
    w_hi, w_lo = _split_bf16(w_ref[...])
    o_ref[...] = _dot(a_hi, w_hi) + _dot(a_lo, w_hi) + _dot(a_hi, w_lo) + b_ref[...]


def _ada(c, w_ada, b_ada):
    n, d = c.shape
    nout = w_ada.shape[1]
    tn = 1024
    return pl.pallas_call(
        _ada_kernel,
        out_shape=jax.ShapeDtypeStruct((n, nout), F32),
        grid=(nout // tn,),
        in_specs=[pl.BlockSpec((n, d), lambda j: (0, 0)),
                  pl.BlockSpec((d, tn), lambda j: (0, j)),
                  pl.BlockSpec((1, tn), lambda j: (0, j))],
        out_specs=pl.BlockSpec((n, tn), lambda j: (0, j)),
        compiler_params=_cparams(("parallel",)),
        name="ada",
    )(c, w_ada, b_ada.reshape(1, nout))


_SEG_QA = (0, D_A)
_SEG_KA = (D_A, 2 * D_A)
_SEG_VA = (2 * D_A, 3 * D_A)
_SEG_QB = (3 * D_A, 3 * D_A + D_B)
_SEG_KB = (3 * D_A + D_B, 3 * D_A + 2 * D_B)
_SEG_VB = (3 * D_A + 2 * D_B, 3 * D_A + 3 * D_B)
_SEG_GA = (3 * D_A + 3 * D_B, 3 * D_A + 3 * D_B + D_MODEL)
_SEG_GB = (3 * D_A + 3 * D_B + D_MODEL, N_IN)


def _inproj_kernel(x_ref, sh_ref, sc_ref, w_ref, b_ref,
                   qa_ref, ka32_ref, va32_ref, ka16_ref, va16_ref,
                   qb_ref, kb16_ref, vb16_ref, sga_ref, sgb_ref, kbt_ref, vbt_ref, *, tiles_per_seq, tail_tiles):
    u = (_ln(x_ref[...]) * (1.0 + _rows(sc_ref)) + _rows(sh_ref)).astype(BF16)

    def seg(lo_hi):
        lo, hi = lo_hi
        return _dot(u, w_ref[:, lo:hi]) + b_ref[:, lo:hi]

    qa_ref[...] = (seg(_SEG_QA) * Q_SCALE).astype(BF16)
    ka = seg(_SEG_KA)
    for hj in range(2 * H_A):
        ka32_ref[:, hj // 2, hj % 2, :] = ka[:, hj * HEAD_DIM:(hj + 1) * HEAD_DIM]
    ka16_ref[...] = ka.astype(BF16)
    va = seg(_SEG_VA)
    va32_ref[...] = va
    va16_ref[...] = va.astype(BF16)
    qb_ref[...] = (seg(_SEG_QB) * Q_SCALE).astype(BF16)
    kb = seg(_SEG_KB)
    kb16_ref[...] = kb.astype(BF16)
    vb = seg(_SEG_VB)
    vb16_ref[...] = vb.astype(BF16)
    @pl.when(pl.program_id(0) % tiles_per_seq >= tiles_per_seq - tail_tiles)
    def _():
        for h in range(H_B):
            kbt_ref[:, h, :] = kb[:, h * HEAD_DIM:(h + 1) * HEAD_DIM]
            vbt_ref[:, h, :] = vb[:, h * HEAD_DIM:(h + 1) * HEAD_DIM]
    sga_ref[...] = jax.nn.sigmoid(seg(_SEG_GA)).astype(BF16)
    sgb_ref[...] = jax.nn.sigmoid(seg(_SEG_GB)).astype(BF16)


def _mod_spec(per_row, tm, rows_per_seq):
    if per_row:
        return pl.BlockSpec((tm, D_MODEL), lambda i: (i, 0))
    tiles_per_seq = rows_per_seq // tm
    return pl.BlockSpec((1, 1, D_MODEL), lambda i: (i // tiles_per_seq, 0, 0))


def _inproj(x, shift, scale, w_in16, b_in, *, tm, rows_per_seq, tail_rows, per_row):
    r = x.shape[0]
    nseq = r // rows_per_seq
    n_i = rows_per_seq // tm
    n_t = tail_rows // tm

    def tail_map(i):
        return ((i // n_i) * n_t + jnp.maximum(i % n_i - (n_i - n_t), 0), 0, 0)

    row = lambda w: pl.BlockSpec((tm, w), lambda i: (i, 0))
    out_shape = (
        jax.ShapeDtypeStruct((r, D_A), BF16),
        jax.ShapeDtypeStruct((r, H_A, 2, HEAD_DIM), F32),
        jax.ShapeDtypeStruct((r, D_A), F32),
        jax.ShapeDtypeStruct((r, D_A), BF16),
        jax.ShapeDtypeStruct((r, D_A), BF16),
        jax.ShapeDtypeStruct((r, D_B), BF16),
        jax.ShapeDtypeStruct((r, D_B), BF16),
        jax.ShapeDtypeStruct((r, D_B), BF16),
        jax.ShapeDtypeStruct((r, D_MODEL), BF16),
        jax.ShapeDtypeStruct((r, D_MODEL), BF16),
        jax.ShapeDtypeStruct((nseq * tail_rows, H_B, HEAD_DIM), F32),
        jax.ShapeDtypeStruct((nseq * tail_rows, H_B, HEAD_DIM), F32),
    )
    out_specs = (row(D_A), pl.BlockSpec((tm, H_A, 2, HEAD_DIM), lambda i: (i, 0, 0, 0)), row(D_A), row(D_A), row(D_A),
                 row(D_B), row(D_B), row(D_B), row(D_MODEL), row(D_MODEL),
                 pl.BlockSpec((tm, H_B, HEAD_DIM), tail_map), pl.BlockSpec((tm, H_B, HEAD_DIM), tail_map))
    return pl.pallas_call(
        functools.partial(_inproj_kernel, tiles_per_seq=n_i, tail_tiles=n_t),
        out_shape=out_shape,
        grid=(r // tm,),
        in_specs=[row(D_MODEL), _mod_spec(per_row, tm, rows_per_seq), _mod_spec(per_row, tm, rows_per_seq),
                  pl.BlockSpec((D_MODEL, N_IN), lambda i: (0, 0)),
                  pl.BlockSpec((1, N_IN), lambda i: (0, 0))],
        out_specs=out_specs,
        compiler_params=_cparams(("arbitrary",)),
        name="inproj",
    )(x, shift, scale, w_in16, b_in)


N_AUG = 3


def _lam_value(lam4_ref, lam_init):
    v = lam4_ref[...]
    a = jnp.sum(v[0:1, :] * v[1:2, :], axis=-1, keepdims=True)
    b = jnp.sum(v[2:3, :] * v[3:4, :], axis=-1, keepdims=True)
    return jnp.exp(a) - jnp.exp(b) + lam_init


def _half_masks(shape):
    lane = lax.broadcasted_iota(jnp.int32, shape, 1)
    return lane < HEAD_DIM, lane >= HEAD_DIM


def _stack_halves(q):
    lo, hi = _half_masks(q.shape)
    zero = jnp.zeros_like(q)
    return jnp.concatenate([jnp.where(lo, q, zero), jnp.where(hi, q, zero)], axis=0)


def _softmax_step(s, v, m, l, acc):
    m_new = jnp.maximum(m, jnp.max(s, axis=-1, keepdims=True))
    alpha = jnp.exp2(m - m_new)
    p = jnp.exp2(s - m_new)
    l_new = alpha * l + jnp.sum(p, axis=-1, keepdims=True)
    acc_new = alpha * acc + _dot(p.astype(BF16), v)
    return m_new, l_new, acc_new


def _diff_finish(o0, o1, lam, g, lam_init):
    o = o0 - lam * o1
    o = o * lax.rsqrt(jnp.mean(o * o, axis=-1, keepdims=True) + LN_EPS)
    return o * g * (1.0 - lam_init)


def _key_bias_columns(slopes, s):
    x = (np.float32(LOG2E) * slopes.astype(np.float32))[:, None] * np.arange(s, dtype=np.float32)[None, :]
    cols = np.zeros(x.shape + (2 * HEAD_DIM,), np.float32)
    for c in range(N_AUG):
        cols[:, :, c] = x.astype(BF16).astype(np.float32)
        x = x - cols[:, :, c]
    return jnp.asarray(cols, BF16)


def _own_block_fix(cslopes, t):
    ii = lax.broadcasted_iota(jnp.int32, (t, t), 0)
    jj = lax.broadcasted_iota(jnp.int32, (t, t), 1)
    ahead = jnp.where(jj > ii, 2.0 * (ii - jj).astype(F32), 0.0)
    return jnp.where(((jj // CHUNK) <= (ii // CHUNK))[None], cslopes[:, None, None] * ahead[None], NEG)


def _diffattn_kernel(lam4_ref, g_ref, fix_ref, q_ref, k_ref, ka_ref, v_ref, o_ref, *, t, lam_init):
    lo, hi = _half_masks((t, 2 * HEAD_DIM))
    lane = lax.broadcasted_iota(jnp.int32, (t, 2 * HEAD_DIM), 1)
    ones = (lane < N_AUG).astype(BF16)
    lam = _lam_value(lam4_ref, lam_init)

    def q_tile(qi, _):
        rows = pl.ds(pl.multiple_of(qi * t, t), t)
        q = q_ref[0, rows, :]
        zero = jnp.zeros_like(q)
        q_ext = (jnp.concatenate([jnp.where(lo, q, zero), ones], axis=1),
                 jnp.concatenate([jnp.where(hi, q, zero), ones], axis=1))

        def block(j, carry, extra):
            keys = pl.ds(pl.multiple_of(j * t, t), t)
            k_ext = jnp.concatenate([k_ref[0, keys, :], ka_ref[0, keys, :]], axis=1)
            vj = v_ref[0, keys, :]
            out = []
            for half in range(2):
                s = _dot_nt(q_ext[half], k_ext)
                if extra is not None:
                    s = s + extra
                out.append(_softmax_step(s, vj, *carry[half]))
            return tuple(out)

        init1 = (jnp.full((t, 1), NEG, F32), jnp.zeros((t, 1), F32), jnp.zeros((t, 2 * HEAD_DIM), F32))
        carry = lax.fori_loop(0, qi, lambda j, c: block(j, c, None), (init1, init1))
        (m0, l0, a0), (m1, l1, a1) = block(qi, carry, fix_ref[0])
        o_ref[0, rows, :] = _diff_finish(a0 / l0, a1 / l1, lam, g_ref[...], lam_init).astype(o_ref.dtype)
        return 0

    lax.fori_loop(0, q_ref.shape[1] // t, q_tile, 0)


def _diffattn(qa, ka, ka_aug, va, cslopes, lam4, subln_g, lam_init, *, t):
    b, s, _ = qa.shape
    kern = functools.partial(_diffattn_kernel, t=t, lam_init=lam_init)
    hd2 = 2 * HEAD_DIM
    return pl.pallas_call(
        kern,
        out_shape=jax.ShapeDtypeStruct((b, s, D_A), BF16),
        grid=(b, H_A),
        in_specs=[pl.BlockSpec((4, HEAD_DIM), lambda bi, h: (0, 0)),
                  pl.BlockSpec((1, hd2), lambda bi, h: (0, 0)),
                  pl.BlockSpec((1, t, t), lambda bi, h: (h, 0, 0)),
                  pl.BlockSpec((1, s, hd2), lambda bi, h: (bi, 0, h)),
                  pl.BlockSpec((1, s, hd2), lambda bi, h: (bi, 0, h)),
                  pl.BlockSpec((1, s, hd2), lambda bi, h: (h, 0, 0)),
                  pl.BlockSpec((1, s, hd2), lambda bi, h: (bi, 0, h))],
        out_specs=pl.BlockSpec((1, s, hd2), lambda bi, h: (bi, 0, h)),
        compiler_params=_cparams(("parallel", "parallel")),
        name="diffattn",
    )(lam4, subln_g.reshape(1, hd2), _own_block_fix(cslopes, t), qa, ka, ka_aug, va)


def _toeplitz_bias(rel_table, rows, cols, delta):
    length = rows + cols - 1
    rel = np.clip(delta + rows - 1 - np.arange(length), -REL_CLIP, REL_CLIP) + REL_CLIP
    u = jnp.take(rel_table.astype(F32) * LOG2E, jnp.asarray(rel, jnp.int32), axis=1)
    u = jnp.pad(u, ((0, 0), (0, 1)))
    r = jnp.tile(u, (1, rows))[:, :rows * length].reshape(-1, rows, length)
    return r[:, :, rows - 1:rows - 1 + cols]


def _band_valid(t_pos, s_pos):
    return (s_pos // CHUNK <= t_pos // CHUNK) & (s_pos // CHUNK >= t_pos // CHUNK - BAND_CHUNKS)


def _band_window(s, tq):
    return min(BAND_CHUNKS * CHUNK + tq, s)


def _band_bias_tiles(rel_table, s, tq):
    win = _band_window(s, tq)
    n_var = min(BAND_CHUNKS * CHUNK // tq + 1, s // tq)
    tiles = []
    for var in range(n_var):
        q0 = var * tq
        k0 = max(q0 - BAND_CHUNKS * CHUNK, 0)
        valid = _band_valid(q0 + np.arange(tq)[:, None], k0 + np.arange(win)[None, :])
        tiles.append(jnp.where(jnp.asarray(valid)[None], _toeplitz_bias(rel_table, tq, win, q0 - k0), NEG))
    return jnp.stack(tiles)


def _pair_softmax_out(qm, kw, vw, bias2, t):
    s = (_dot_nt(qm, kw).reshape(2, t, -1) + bias2).reshape(2 * t, -1)
    m = jnp.max(s, axis=-1, keepdims=True)
    p = jnp.exp2(s - m)
    l = jnp.sum(p, axis=-1, keepdims=True)
    o = _dot(p.astype(BF16), vw) / l
    lo, _ = _half_masks((t, 2 * HEAD_DIM))
    return jnp.where(lo, o[:t], o[t:])


def _bandattn_kernel(bias_ref, q_ref, k_ref, v_ref, o_ref, *, tq, win, n_var):
    qi = pl.program_id(2)
    var = jnp.minimum(qi, n_var - 1)
    k0 = pl.multiple_of(jnp.maximum(qi * tq - BAND_CHUNKS * CHUNK, 0), tq)
    qm = _stack_halves(q_ref[0])
    kw = k_ref[0, pl.ds(k0, win), :]
    vw = v_ref[0, pl.ds(k0, win), :]
    o_ref[0] = _pair_softmax_out(qm, kw, vw, bias_ref[var], tq).astype(o_ref.dtype)


def _bandattn(qb, kb, vb, bias_tiles, *, tq):
    b, s, _ = qb.shape
    n_var, _, _, win = bias_tiles.shape
    kern = functools.partial(_bandattn_kernel, tq=tq, win=win, n_var=n_var)
    return pl.pallas_call(
        kern,
        out_shape=jax.ShapeDtypeStruct((b, s, D_B), BF16),
        grid=(H_B // 2, b, s // tq),
        in_specs=[pl.BlockSpec((n_var, 2, tq, win), lambda hp, bi, i: (0, hp, 0, 0)),
                  pl.BlockSpec((1, tq, 2 * HEAD_DIM), lambda hp, bi, i: (bi, i, hp)),
                  pl.BlockSpec((1, s, 2 * HEAD_DIM), lambda hp, bi, i: (bi, 0, hp)),
                  pl.BlockSpec((1, s, 2 * HEAD_DIM), lambda hp, bi, i: (bi, 0, hp))],
        out_specs=pl.BlockSpec((1, tq, 2 * HEAD_DIM), lambda hp, bi, i: (bi, i, hp)),
        compiler_params=_cparams(("parallel", "parallel", "arbitrary")),
        name="bandattn",
    )(bias_tiles, qb, kb, vb)


def _two_part_softmax(s_c, s_n, vc, vn):
    m = jnp.maximum(jnp.max(s_c, axis=-1, keepdims=True), jnp.max(s_n, axis=-1, keepdims=True))
    p_c = jnp.exp2(s_c - m)
    p_n = jnp.exp2(s_n - m)
    l = jnp.sum(p_c, axis=-1, keepdims=True) + jnp.sum(p_n, axis=-1, keepdims=True)
    return (_dot(p_c.astype(BF16), vc) + _dot(p_n.astype(BF16), vn)) / l


def _sampattn_kernel(cs_ref, lam4_ref, g_ref, bbias_ref, qa_ref, ckd_ref, cvd_ref, kan_ref, van_ref,
                     qb_ref, ckb_ref, cvb_ref, kbn_ref, vbn_ref, oa_ref, ob_ref, *, n, past, lam_init):
    lam = _lam_value(lam4_ref, lam_init)
    tq_pos = past + lax.broadcasted_iota(jnp.int32, (n, past), 0)
    ts_pos = lax.broadcasted_iota(jnp.int32, (n, past), 1)
    dist_c = jnp.abs(tq_pos - ts_pos).astype(F32)
    vis_c = (ts_pos // CHUNK) <= (tq_pos // CHUNK)
    ii = lax.broadcasted_iota(jnp.int32, (n, n), 0)
    jj = lax.broadcasted_iota(jnp.int32, (n, n), 1)
    dist_n = jnp.abs(ii - jj).astype(F32)
    vis_n = ((past + jj) // CHUNK) <= ((past + ii) // CHUNK)
    for h in range(H_A):
        sl = slice(h * 2 * HEAD_DIM, (h + 1) * 2 * HEAD_DIM)
        cs = cs_ref[h]
        qm = _stack_halves(qa_ref[0, :, sl])
        bias_c = jnp.where(vis_c, -cs * dist_c, NEG)
        bias_n = jnp.where(vis_n, -cs * dist_n, NEG)
        s_c = (_dot_nt(qm, ckd_ref[0, :, sl].astype(BF16)).reshape(2, n, past) + bias_c[None]).reshape(2 * n, past)
        s_n = (_dot_nt(qm, kan_ref[0, :, sl]).reshape(2, n, n) + bias_n[None]).reshape(2 * n, n)
        o = _two_part_softmax(s_c, s_n, cvd_ref[0, :, sl].astype(BF16), van_ref[0, :, sl])
        oa_ref[0, :, sl] = _diff_finish(o[:n], o[n:], lam, g_ref[...], lam_init).astype(oa_ref.dtype)
    lb = ckb_ref.shape[1]
    for hp in range(H_B // 2):
        sl = slice(hp * 2 * HEAD_DIM, (hp + 1) * 2 * HEAD_DIM)
        qm = _stack_halves(qb_ref[0, :, sl])
        bias = bbias_ref[2 * hp:2 * hp + 2]
        s_c = (_dot_nt(qm, ckb_ref[0, :, sl].astype(BF16)).reshape(2, n, lb) + bias[:, :, :lb]).reshape(2 * n, lb)
        s_n = (_dot_nt(qm, kbn_ref[0, :, sl]).reshape(2, n, n) + bias[:, :, lb:]).reshape(2 * n, n)
        o = _two_part_softmax(s_c, s_n, cvb_ref[0, :, sl].astype(BF16), vbn_ref[0, :, sl])
        lo, _ = _half_masks((n, 2 * HEAD_DIM))
        ob_ref[0, :, sl] = jnp.where(lo, o[:n], o[n:]).astype(ob_ref.dtype)


def _sample_band_bias(rel_table, n, past, lb):
    valid = _band_valid(past + np.arange(n)[:, None], (past - lb) + np.arange(lb + n)[None, :])
    return jnp.where(jnp.asarray(valid)[None], _toeplitz_bias(rel_table, n, lb + n, lb), NEG)


def _sampattn(qa, ckd, cvd, kan, van, qb, ckb, cvb, kbn, vbn, cslopes, lam4, subln_g, bbias, lam_init):
    b, n, _ = qa.shape
    past = ckd.shape[1]
    lb = ckb.shape[1]
    kern = functools.partial(_sampattn_kernel, n=n, past=past, lam_init=lam_init)
    full = lambda shape: pl.BlockSpec(shape, lambda bi: (0,) * len(shape))
    per_b = lambda r, w: pl.BlockSpec((1, r, w), lambda bi: (bi, 0, 0))
    return pl.pallas_call(
        kern,
        out_shape=(jax.ShapeDtypeStruct((b, n, D_A), BF16), jax.ShapeDtypeStruct((b, n, D_B), BF16)),
        grid=(b,),
        in_specs=[pl.BlockSpec(memory_space=pltpu.SMEM), full((4, HEAD_DIM)), full((1, 2 * HEAD_DIM)),
                  full((H_B, n, lb + n)),
                  per_b(n, D_A), per_b(past, D_A), per_b(past, D_A), per_b(n, D_A), per_b(n, D_A),
                  per_b(n, D_B), per_b(lb, D_B), per_b(lb, D_B), per_b(n, D_B), per_b(n, D_B)],
        out_specs=(per_b(n, D_A), per_b(n, D_B)),
        compiler_params=_cparams(("parallel",)),
        name="sampattn",
    )(cslopes, lam4, subln_g.reshape(1, 2 * HEAD_DIM), bbias, qa, ckd, cvd, kan, van, qb, ckb, cvb, kbn, vbn)


def _outproj_kernel(oa_ref, ob_ref, sga_ref, sgb_ref, x_ref, g1_ref, sh2_ref, sc2_ref, pa_ref, pb_ref, wo_ref,
                    lng_ref, lnb_ref, x1_ref, h_ref, *, alpha):
    mix = (sga_ref[...].astype(F32) * _dot(oa_ref[...], pa_ref[...])
           + sgb_ref[...].astype(F32) * _dot(ob_ref[...], pb_ref[...]))
    y = _dot(mix.astype(BF16), wo_ref[...])
    x1 = _ln(alpha * x_ref[...] + _rows(g1_ref) * y) * lng_ref[...] + lnb_ref[...]
    x1_ref[...] = x1
    u2 = _ln(x1) * (1.0 + _rows(sc2_ref)) + _rows(sh2_ref)
    h_ref[...] = _pack_pair(u2[:, :PACKED], u2[:, PACKED:])


def _outproj(oa, ob, sga, sgb, x, gate1, shift2, scale2, pa16, pb16, wo16, ln_g, ln_b, *, tm, rows_per_seq,
             per_row, alpha):
    r = x.shape[0]
    row = lambda w: pl.BlockSpec((tm, w), lambda i: (i, 0))
    full = lambda a, b: pl.BlockSpec((a, b), lambda i: (0, 0))
    mod = _mod_spec(per_row, tm, rows_per_seq)
    return pl.pallas_call(
        functools.partial(_outproj_kernel, alpha=alpha),
        out_shape=(jax.ShapeDtypeStruct((r, D_MODEL), F32), jax.ShapeDtypeStruct((r, PACKED), jnp.uint32)),
        grid=(r // tm,),
        in_specs=[row(D_A), row(D_B), row(D_MODEL), row(D_MODEL), row(D_MODEL), mod, mod, mod,
                  full(D_A, D_MODEL), full(D_B, D_MODEL), full(D_MODEL, D_MODEL),
                  full(1, D_MODEL), full(1, D_MODEL)],
        out_specs=(row(D_MODEL), row(PACKED)),
        compiler_params=_cparams(("parallel",)),
        name="outproj",
    )(oa, ob, sga, sgb, x, gate1, shift2, scale2, pa16, pb16, wo16, ln_g.reshape(1, -1), ln_b.reshape(1, -1))


def _first_index_of_max(x, iota, axis, size):
    m = jnp.max(x, axis=axis, keepdims=True)
    idx = jnp.min(jnp.where(x == m, iota, size), axis=axis, keepdims=True)
    return m, idx


def _router_kernel(hp_ref, hs_ref, wr_hi_ref, wr_lo_ref, eb_ref, wt_ref, pos_ref, tab_ref, cnt_ref, run_ref, *, tr,
                   n_first):
    step = pl.program_id(0)

    @pl.when(step == 0)
    def _():
        run_ref[...] = jnp.zeros_like(run_ref)

    ha, hb = _unpack_pair(_two_group_rows(hp_ref, hs_ref, n_first))
    wr_hi, wr_lo = wr_hi_ref[...], wr_lo_ref[...]
    logits = (_dot_nt(wr_hi[:, :PACKED], ha) + _dot_nt(wr_hi[:, PACKED:], hb)
              + _dot_nt(wr_lo[:, :PACKED], ha) + _dot_nt(wr_lo[:, PACKED:], hb))
    scores = jax.nn.sigmoid(logits)
    biased = scores + eb_ref[...]

    x3 = biased.reshape(N_GROUP, GROUP_SIZE, tr)
    io3 = lax.broadcasted_iota(jnp.int32, x3.shape, 1)
    m1, i1 = _first_index_of_max(x3, io3, 1, GROUP_SIZE)
    m2 = jnp.max(jnp.where(io3 == i1, -jnp.inf, x3), axis=1, keepdims=True)
    grp = (m1 + m2).reshape(N_GROUP, tr)

    iog = lax.broadcasted_iota(jnp.int32, grp.shape, 0)
    gsel = jnp.zeros(grp.shape, jnp.bool_)
    for _ in range(TOPK_GROUP):
        _, gi = _first_index_of_max(grp, iog, 0, N_GROUP)
        hit = iog == gi
        gsel = gsel | hit
        grp = jnp.where(hit, -jnp.inf, grp)
    emask = jnp.broadcast_to(gsel.reshape(N_GROUP, 1, tr), (N_GROUP, GROUP_SIZE, tr)).reshape(N_EXPERTS, tr)
    cand = jnp.where(emask, biased, -jnp.inf)

    ioe = lax.broadcasted_iota(jnp.int32, cand.shape, 0)
    hits = []
    sel = jnp.zeros(cand.shape, jnp.bool_)
    for _ in range(TOP_K):
        _, ei = _first_index_of_max(cand, ioe, 0, N_EXPERTS)
        hit = ioe == ei
        hits.append((ei, hit))
        sel = sel | hit
        cand = jnp.where(hit, -jnp.inf, cand)
    self32 = sel.astype(F32)
    ra = lax.broadcasted_iota(jnp.int32, (tr, tr), 0)
    rb = lax.broadcasted_iota(jnp.int32, (tr, tr), 1)
    upper = (ra < rb).astype(BF16)
    prefix = _dot(self32.astype(BF16), upper)

    count = jnp.sum(self32, axis=1, keepdims=True)
    run_len = jnp.floor((count + (ROW_ALIGN - 1)) * (1.0 / ROW_ALIGN)) * ROW_ALIGN
    ea = lax.broadcasted_iota(jnp.int32, (N_EXPERTS, N_EXPERTS), 0)
    eb = lax.broadcasted_iota(jnp.int32, (N_EXPERTS, N_EXPERTS), 1)
    before = (eb < ea).astype(BF16)
    local_start = _dot(before, jnp.broadcast_to(run_len, (N_EXPERTS, 128)).astype(BF16))[:, :1]
    local_pos = local_start + prefix

    ws = [jnp.sum(jnp.where(hit, scores, 0.0), axis=0, keepdims=True) for _, hit in hits]
    wsum = ws[0]
    for w in ws[1:]:
        wsum = wsum + w
    for k, (_, hit) in enumerate(hits):
        wt_ref[k:k + 1, :] = ws[k] / wsum * ROUTED_SCALE
        pos_ref[k:k + 1, :] = jnp.sum(jnp.where(hit, local_pos, 0.0), axis=0, keepdims=True).astype(jnp.int32)
    run_old = run_ref[...]
    lane = lax.broadcasted_iota(jnp.int32, (N_EXPERTS, 128), 1)
    tab = jnp.where(lane == 0, run_old, jnp.where(lane == 1, local_start, jnp.where(lane == 2, run_len, 0.0)))
    tab_ref[0] = tab.astype(jnp.int32)
    run_new = run_old + run_len
    run_ref[...] = run_new
    cnt_ref[...] = jnp.broadcast_to(run_new, cnt_ref.shape).astype(jnp.int32)


def _router(hp, hs, w_router, e_bias, *, tr):
    t = hp.shape[0] + hs.shape[0]
    n_first = hp.shape[0] // tr
    wr_t = w_router.T
    hi32 = lax.bitcast_convert_type(lax.bitcast_convert_type(wr_t, jnp.uint32) & jnp.uint32(0xFFFF0000), F32)
    wr_hi = hi32.astype(BF16)
    wr_lo = (wr_t - hi32).astype(BF16)
    full = lambda a, b: pl.BlockSpec((a, b), lambda i: (0, 0))
    col = lambda: pl.BlockSpec((TOP_K, tr), lambda i: (0, i))
    return pl.pallas_call(
        functools.partial(_router_kernel, tr=tr, n_first=n_first),
        out_shape=(jax.ShapeDtypeStruct((TOP_K, t), F32),
                   jax.ShapeDtypeStruct((TOP_K, t), jnp.int32),
                   jax.ShapeDtypeStruct((t // tr, N_EXPERTS, 128), jnp.int32),
                   jax.ShapeDtypeStruct((N_EXPERTS, 128), jnp.int32)),
        grid=(t // tr,),
        in_specs=[*_two_group_specs(tr, n_first), full(N_EXPERTS, D_MODEL),
                  full(N_EXPERTS, D_MODEL), full(N_EXPERTS, 1)],
        out_specs=(col(), col(), pl.BlockSpec((1, N_EXPERTS, 128), lambda i: (i, 0, 0)), full(N_EXPERTS, 128)),
        scratch_shapes=[pltpu.VMEM((N_EXPERTS, 1), F32)],
        compiler_params=_cparams(("arbitrary",)),
        name="router",
    )(hp, hs, wr_hi, wr_lo, e_bias.reshape(N_EXPERTS, 1).astype(F32))


TAB_GLOBAL, TAB_LOCAL, TAB_LEN = 0, 1, 2


def _sorted_rows(tm):
    need = tm * TOP_K + N_EXPERTS * (ROW_ALIGN - 1)
    return -(-need // SORT_CHUNK) * SORT_CHUNK


def _run(tab_smem, s, e):
    return tuple(pl.multiple_of(tab_smem[s, r * N_EXPERTS + e], ROW_ALIGN) for r in (TAB_GLOBAL, TAB_LOCAL, TAB_LEN))


def _tile_rows(tab_smem, s):
    _, local, length = _run(tab_smem, s, N_EXPERTS - 1)
    return pl.multiple_of(local + length, ROW_ALIGN)


def _rows_copy(src_ref, src_row, dst_ref, dst_row, rows, sem):
    return pltpu.make_async_copy(src_ref.at[pl.ds(src_row, rows)], dst_ref.at[pl.ds(dst_row, rows)], sem)


def _dispatch_kernel(fill_lo_ref, fill_hi_ref, tab_hbm, pos_ref, hp_ref, hs_ref, xs_hbm, tab_smem, rows_smem,
                     sorted_ref, zero_ref, tsem, csem, zsem, *, tm, n_first):
    i = pl.program_id(0)
    n = pl.num_programs(0)
    slot = i % 2

    def tab_copy(tile, s):
        return pltpu.make_async_copy(tab_hbm.at[tile], tab_smem.at[s], tsem.at[s])

    def wait_runs(s):
        rows = rows_smem[s]

        @pl.when(rows > 0)
        def _():
            _rows_copy(sorted_ref.at[s], 0, xs_hbm, 0, pl.multiple_of(rows, ROW_ALIGN), csem.at[s]).wait()

    @pl.when(i == 0)
    def _():
        tab_copy(0, 0).start()
        zero_ref[...] = jnp.zeros_like(zero_ref)
        blk = zero_ref.shape[0]
        for e in range(N_EXPERTS):
            lo = pl.multiple_of(fill_lo_ref[e], ROW_ALIGN)
            rows = pl.multiple_of(fill_hi_ref[e] - lo, ROW_ALIGN)

            @pl.when(rows > 0)
            def _():
                cp = _rows_copy(zero_ref, 0, xs_hbm, lo, rows, zsem)
                cp.start()
                cp.wait()

        def tail(b, c):
            cp = _rows_copy(zero_ref, 0, xs_hbm, pl.multiple_of(b * blk, blk), blk, zsem)
            cp.start()
            cp.wait()
            return c

        lax.fori_loop(fill_hi_ref[N_EXPERTS - 1] // blk, xs_hbm.shape[0] // blk, tail, 0)

    tab_copy(i, slot).wait()

    @pl.when(i + 1 < n)
    def _():
        tab_copy(i + 1, 1 - slot).start()

    ha, hb = _unpack_pair(_two_group_rows(hp_ref, hs_ref, n_first))
    pos = pos_ref[...]
    rows = _tile_rows(tab_smem, slot)

    def sort_chunk(c, carry):
        r0 = pl.multiple_of(c * SORT_CHUNK, SORT_CHUNK)
        row = r0 + lax.broadcasted_iota(jnp.int32, (SORT_CHUNK, tm), 0)
        pick = jnp.zeros((SORT_CHUNK, tm), F32)
        for k in range(TOP_K):
            pick = jnp.where(row == pos[k:k + 1, :], 1.0, pick)
        pick = pick.astype(BF16)
        sorted_ref[slot, pl.ds(r0, SORT_CHUNK), :] = _pack_pair(_dot(pick, ha), _dot(pick, hb))
        return carry

    lax.fori_loop(0, (rows + SORT_CHUNK - 1) // SORT_CHUNK, sort_chunk, 0)

    for e in range(N_EXPERTS):
        dst, src, length = _run(tab_smem, slot, e)

        @pl.when(length > 0)
        def _():
            _rows_copy(sorted_ref.at[slot], src, xs_hbm, dst, length, csem.at[slot]).start()

    rows_smem[slot] = rows

    @pl.when(i > 0)
    def _():
        wait_runs(1 - slot)

    @pl.when(i == n - 1)
    def _():
        wait_runs(slot)


def _dispatch(hp, hs, pos_t, tab, fill_lo, fill_hi, n_slots, *, tm):
    t = hp.shape[0] + hs.shape[0]
    n_first = hp.shape[0] // tm
    return pl.pallas_call(
        functools.partial(_dispatch_kernel, tm=tm, n_first=n_first),
        out_shape=jax.ShapeDtypeStruct((n_slots, PACKED), jnp.uint32),
        grid_spec=pltpu.PrefetchScalarGridSpec(
            num_scalar_prefetch=2,
            grid=(t // tm,),
            in_specs=[pl.BlockSpec(memory_space=pl.ANY),
                      pl.BlockSpec((TOP_K, tm), lambda i, lo, hi: (0, i)),
                      *_two_group_specs(tm, n_first)],
            out_specs=pl.BlockSpec(memory_space=pl.ANY),
            scratch_shapes=[pltpu.SMEM((2, 3 * N_EXPERTS), jnp.int32), pltpu.SMEM((2,), jnp.int32),
                            pltpu.VMEM((2, _sorted_rows(tm), PACKED), jnp.uint32),
                            pltpu.VMEM((EXPERT_BLOCK, PACKED), jnp.uint32),
                            pltpu.SemaphoreType.DMA((2,)), pltpu.SemaphoreType.DMA((2,)),
                            pltpu.SemaphoreType.DMA(())],
        ),
        compiler_params=_cparams(("arbitrary",)),
        name="dispatch",
    )(fill_lo, fill_hi, tab, pos_t, hp, hs)


def _experts_kernel(be_ref, nu_ref, x_ref, w13_ref, w2_ref, y_ref):
    @pl.when(pl.program_id(0) < nu_ref[0])
    def _():
        a = _dot_packed(x_ref[...], w13_ref[0])
        hid = (a[:, :D_EXPERT] * jax.nn.sigmoid(a[:, :D_EXPERT])) * a[:, D_EXPERT:]
        y = _dot(hid.astype(BF16), w2_ref[0])
        y_ref[...] = _pack_pair(y[:, :PACKED], y[:, PACKED:])

    @pl.when(pl.program_id(0) >= nu_ref[0])
    def _():
        y_ref[...] = jnp.zeros_like(y_ref)


def _experts(xs, w13, w2, block_e, n_used):
    n_slots = xs.shape[0]
    nb = n_slots // EXPERT_BLOCK
    last = lambda i, nu: jnp.minimum(i, nu[0] - 1)
    return pl.pallas_call(
        _experts_kernel,
        out_shape=jax.ShapeDtypeStruct((n_slots, PACKED), jnp.uint32),
        grid_spec=pltpu.PrefetchScalarGridSpec(
            num_scalar_prefetch=2,
            grid=(nb,),
            in_specs=[pl.BlockSpec((EXPERT_BLOCK, PACKED), lambda i, be, nu: (last(i, nu), 0)),
                      pl.BlockSpec((1, D_MODEL, 2 * D_EXPERT), lambda i, be, nu: (be[last(i, nu)], 0, 0)),
                      pl.BlockSpec((1, D_EXPERT, D_MODEL), lambda i, be, nu: (be[last(i, nu)], 0, 0))],
            out_specs=pl.BlockSpec((EXPERT_BLOCK, PACKED), lambda i, be, nu: (i, 0)),
        ),
        compiler_params=_cparams(("arbitrary",)),
        name="experts",
    )(block_e, n_used, xs, w13, w2)


def _final_kernel(tab_hbm, yb_hbm, h_ref, x1_ref, pos_ref, wt_ref, g2_ref, ws13_ref, ws2_ref, lng_ref, lnb_ref,
                  o_ref, tab_smem, rows_smem, ybuf, tsem, gsem, *, alpha, tm, off):
    i = pl.program_id(0)
    n = pl.num_programs(0)
    slot = i % 2

    def tab_copy(tile, s):
        return pltpu.make_async_copy(tab_hbm.at[tile + off], tab_smem.at[s], tsem.at[s])

    def fetch_runs(s):
        for e in range(N_EXPERTS):
            src, dst, length = _run(tab_smem, s, e)

            @pl.when(length > 0)
            def _():
                _rows_copy(yb_hbm, src, ybuf.at[s], dst, length, gsem.at[s]).start()

        rows_smem[s] = _tile_rows(tab_smem, s)

    @pl.when(i == 0)
    def _():
        ybuf[...] = jnp.zeros_like(ybuf)
        tab_copy(0, 0).start()
        tab_copy(0, 0).wait()
        fetch_runs(0)

        @pl.when(n > 1)
        def _():
            tab_copy(1, 1).start()

    @pl.when(i + 1 < n)
    def _():
        tab_copy(i + 1, 1 - slot).wait()
        fetch_runs(1 - slot)

        @pl.when(i + 2 < n)
        def _():
            tab_copy(i + 2, slot).start()

    a = _dot_packed(h_ref[...], ws13_ref[...])
    hid = (a[:, :D_EXPERT] * jax.nn.sigmoid(a[:, :D_EXPERT])) * a[:, D_EXPERT:]
    y = _dot(hid.astype(BF16), ws2_ref[...])

    rows = rows_smem[slot]

    @pl.when(rows > 0)
    def _():
        _rows_copy(yb_hbm, 0, ybuf.at[slot], 0, pl.multiple_of(rows, ROW_ALIGN), gsem.at[slot]).wait()

    pos = pos_ref[...]
    wt = wt_ref[...]
    lanes = 128
    pos_b = [jnp.broadcast_to(pos[:, k:k + 1], (tm, lanes)) for k in range(TOP_K)]
    wt_b = [jnp.broadcast_to(wt[:, k:k + 1], (tm, lanes)) for k in range(TOP_K)]
    lane_id = lax.broadcasted_iota(jnp.int32, (tm, lanes), 1)

    def combine_chunk(c, acc):
        r0 = pl.multiple_of(c * SORT_CHUNK, SORT_CHUNK)
        parts = []
        for part in range(SORT_CHUNK // lanes):
            col = lane_id + (r0 + part * lanes)
            w = jnp.zeros((tm, lanes), F32)
            for k in range(TOP_K):
                w = jnp.where(col == pos_b[k], wt_b[k], w)
            parts.append(w)
        w16 = jnp.concatenate(parts, axis=1).astype(BF16)
        ea, eb = _unpack_pair(ybuf[slot, pl.ds(r0, SORT_CHUNK), :])
        return acc[0] + _dot(w16, ea), acc[1] + _dot(w16, eb)

    zero = jnp.zeros((tm, PACKED), F32)
    ya, yb = lax.fori_loop(0, (rows + SORT_CHUNK - 1) // SORT_CHUNK, combine_chunk, (zero, zero))
    y = y + jnp.concatenate([ya, yb], axis=1)
    o_ref[...] = _ln(alpha * x1_ref[...] + _rows(g2_ref) * y) * lng_ref[...] + lnb_ref[...]


def _final(tab, yb, h, x1, pos, wt, gate2, ws13, ws2, ln_g, ln_b, *, tm, row0, rows_per_seq, per_row, alpha):
    r = x1.shape[0]
    off = row0 // tm
    row = lambda w: pl.BlockSpec((tm, w), lambda i: (i, 0))
    row_off = lambda w: pl.BlockSpec((tm, w), lambda i: (i + off, 0))
    full = lambda a, b: pl.BlockSpec((a, b), lambda i: (0, 0))
    any_spec = pl.BlockSpec(memory_space=pl.ANY)
    return pl.pallas_call(
        functools.partial(_final_kernel, alpha=alpha, tm=tm, off=off),
        out_shape=jax.ShapeDtypeStruct((r, D_MODEL), F32),
        grid=(r // tm,),
        in_specs=[any_spec, any_spec, row(PACKED), row(D_MODEL), row_off(TOP_K), row_off(TOP_K),
                  _mod_spec(per_row, tm, rows_per_seq),
                  full(D_MODEL, 2 * D_EXPERT), full(D_EXPERT, D_MODEL), full(1, D_MODEL), full(1, D_MODEL)],
        out_specs=row(D_MODEL),
        scratch_shapes=[pltpu.SMEM((2, 3 * N_EXPERTS), jnp.int32), pltpu.SMEM((2,), jnp.int32),
                        pltpu.VMEM((2, _sorted_rows(tm), PACKED), jnp.uint32),
                        pltpu.SemaphoreType.DMA((2,)), pltpu.SemaphoreType.DMA((2,))],
        compiler_params=_cparams(("arbitrary",)),
        name="final",
    )(tab, yb, h, x1, pos, wt, gate2, ws13, ws2, ln_g.reshape(1, -1), ln_b.reshape(1, -1))


def _pick_tile(n, pref):
    t = min(pref, n)
    while n % t:
        t //= 2
    return t


def kernel(x_prompt, x_sample, cache_k_diff, cache_v_diff, cache_k_band, cache_v_band, c_prompt, c_sample,
           w_ada, b_ada, w_in, b_in, lambda_q1, lambda_k1, lambda_q2, lambda_k2, subln_g, rel_bias, p_a, p_b,
           w_out, ln1_g, ln1_b, w_router, e_bias, w1, w3, w2, ws1, ws3, ws2, ln2_g, ln2_b):
    depth = w_in.shape[0]
    alpha = (2 * depth) ** 0.25
    bp, sp, d = x_prompt.shape
    bs, ss, _ = x_sample.shape
    past = cache_k_diff.shape[2]
    lb = cache_k_band.shape[2]
    tp, ts_ = bp * sp, bs * ss
    t_all = tp + ts_
    slopes = (2.0 ** (-8.0 * np.arange(1, H_A + 1) / H_A)).astype(np.float32)
    cslopes = jnp.asarray(np.float32(LOG2E) * slopes)

    yp = x_prompt.reshape(tp, d)
    ys = x_sample.reshape(ts_, d)
    outs = [[] for _ in range(8)]
    tail_p = min(BAND_CHUNKS * CHUNK, sp)
    tm_p = _pick_tile(math.gcd(sp, tail_p), 256)
    tq = _pick_tile(sp, 256)
    ta = _pick_tile(sp, 512)
    ka_aug = _key_bias_columns(slopes, sp)

    for l in range(depth):
        lam_init = 0.8 - 0.6 * math.exp(-0.3 * l)
        lam4 = jnp.stack([lambda_q1[l], lambda_k1[l], lambda_q2[l], lambda_k2[l]]).astype(F32)
        w_in16 = w_in[l].astype(BF16)
        b_in2 = b_in[l].reshape(1, N_IN)
        pa16, pb16, wo16 = p_a[l].astype(BF16), p_b[l].astype(BF16), w_out[l].astype(BF16)

        mod = _ada(jnp.concatenate([c_prompt, c_sample], axis=0), w_ada[l], b_ada[l])
        mod_p = [m.reshape(bp, 1, d) for m in jnp.split(mod[:bp], 6, axis=-1)]
        mod_s = [jnp.repeat(m, ss, axis=0) for m in jnp.split(mod[bp:], 6, axis=-1)]

        (qa, ka32, va32, ka16, va16, qb, kb16, vb16, sga, sgb, kbt, vbt) = _inproj(
            yp, mod_p[0], mod_p[1], w_in16, b_in2, tm=tm_p, rows_per_seq=sp, tail_rows=tail_p, per_row=False)
        r3 = lambda a: a.reshape(bp, sp, a.shape[-1])
        oa = _diffattn(r3(qa), r3(ka16), ka_aug, r3(va16), cslopes, lam4, subln_g[l], lam_init, t=ta)
        ob = _bandattn(r3(qb), r3(kb16), r3(vb16), _band_bias_tiles(rel_bias[l], sp, tq), tq=tq)
        x1p, hp = _outproj(oa.reshape(tp, D_A), ob.reshape(tp, D_B), sga, sgb, yp, mod_p[2], mod_p[3], mod_p[4],
                           pa16, pb16, wo16, ln1_g[l], ln1_b[l], tm=_pick_tile(sp, 512), rows_per_seq=sp,
                           per_row=False, alpha=alpha)
        outs[0].append(ka32.reshape(bp, sp, H_A, 2, HEAD_DIM))
        outs[1].append(va32.reshape(bp, sp, H_A, 2 * HEAD_DIM))
        outs[2].append(kbt.reshape(bp, tail_p, H_B, HEAD_DIM))
        outs[3].append(vbt.reshape(bp, tail_p, H_B, HEAD_DIM))

        (qa_s, ka32_s, va32_s, ka16_s, va16_s, qb_s, kb16_s, vb16_s, sga_s, sgb_s, kbt_s, vbt_s) = _inproj(
            ys, mod_s[0], mod_s[1], w_in16, b_in2, tm=ts_, rows_per_seq=ts_, tail_rows=ts_, per_row=True)
        s3 = lambda a: a.reshape(bs, ss, a.shape[-1])
        oa_s, ob_s = _sampattn(
            s3(qa_s), cache_k_diff[l].reshape(bs, past, D_A), cache_v_diff[l].reshape(bs, past, D_A),
            s3(ka16_s), s3(va16_s), s3(qb_s), cache_k_band[l].reshape(bs, lb, D_B),
            cache_v_band[l].reshape(bs, lb, D_B), s3(kb16_s), s3(vb16_s), cslopes, lam4, subln_g[l],
            _sample_band_bias(rel_bias[l], ss, past, lb), lam_init)
        x1s, hs = _outproj(oa_s.reshape(ts_, D_A), ob_s.reshape(ts_, D_B), sga_s, sgb_s, ys, mod_s[2], mod_s[3],
                           mod_s[4], pa16, pb16, wo16, ln1_g[l], ln1_b[l], tm=ts_, rows_per_seq=ts_, per_row=True,
                           alpha=alpha)
        outs[4].append(ka32_s.reshape(bs, ss, H_A, 2, HEAD_DIM))
        outs[5].append(va32_s.reshape(bs, ss, H_A, 2 * HEAD_DIM))
        outs[6].append(kbt_s.reshape(bs, ss, H_B, HEAD_DIM))
        outs[7].append(vbt_s.reshape(bs, ss, H_B, HEAD_DIM))

        tm_r = _pick_tile(math.gcd(sp, ts_), 256)
        n_tiles = t_all // tm_r
        wt_t, pos_t, tab_raw, cnt = _router(hp, hs, w_router[l], e_bias[l], tr=tm_r)
        counts = cnt[:, 0]
        padded = (counts + EXPERT_BLOCK - 1) // EXPERT_BLOCK * EXPERT_BLOCK
        pad_end = jnp.cumsum(padded)
        pad_start = pad_end - padded
        max_slots = t_all * TOP_K + n_tiles * N_EXPERTS * (ROW_ALIGN - 1)
        nb = -(-max_slots // EXPERT_BLOCK) + N_EXPERTS
        n_slots = nb * EXPERT_BLOCK
        block_starts = jnp.arange(nb, dtype=jnp.int32) * EXPERT_BLOCK
        block_e = jnp.minimum(jnp.sum(pad_end[None, :] <= block_starts[:, None], axis=1),
                              N_EXPERTS - 1).astype(jnp.int32)
        n_used = (pad_end[-1:] // EXPERT_BLOCK).astype(jnp.int32)
        tab = jnp.stack([tab_raw[:, :, 0] + pad_start[None, :], tab_raw[:, :, 1], tab_raw[:, :, 2]], axis=1)
        tab = tab.reshape(n_tiles, 3 * N_EXPERTS).astype(jnp.int32)
        xs = _dispatch(hp, hs, pos_t, tab, (pad_start + counts).astype(jnp.int32), pad_end.astype(jnp.int32),
                       n_slots, tm=tm_r)
        w13 = jnp.concatenate([w1[l], w3[l]], axis=-1).astype(BF16)
        yb = _experts(xs, w13, w2[l].astype(BF16), block_e, n_used)
        wt, pos = wt_t.T, pos_t.T
        ws13 = jnp.concatenate([ws1[l], ws3[l]], axis=-1).astype(BF16)
        ws2b = ws2[l].astype(BF16)
        yp = _final(tab, yb, hp, x1p, pos, wt, mod_p[5], ws13, ws2b, ln2_g[l], ln2_b[l], tm=tm_r, row0=0,
                    rows_per_seq=sp, per_row=False, alpha=alpha)
        ys = _final(tab, yb, hs, x1s, pos, wt, mod_s[5], ws13, ws2b, ln2_g[l], ln2_b[l], tm=tm_r, row0=tp,
                    rows_per_seq=ts_, per_row=True, alpha=alpha)

    return (yp.reshape(bp, sp, d), ys.reshape(bs, ss, d)) + tuple(jnp.stack(o) for o in outs)
```
